```python
import jax, jax.numpy as jnp
from jax import lax
import numpy as np

D_MODEL = 1024
BATCH = 8
SEQ = 2048
DEPTH = 4
DEC_BATCH = 128
DEC_SEQ = 1
PAST_LEN = 16384
PAGE_SIZE = 128

MIX_WIDTH = D_MODEL
MLSTM_WIDTH = MIX_WIDTH // 2
N_MLSTM_HEADS = 4
MLSTM_HEAD_DIM = MLSTM_WIDTH // N_MLSTM_HEADS
POOL_WIDTH = MIX_WIDTH - MLSTM_WIDTH
POOL_WINDOWS = (2, 4, 8, 16)
N_POOL_GROUPS = len(POOL_WINDOWS)
POOL_GROUP_DIM = POOL_WIDTH // N_POOL_GROUPS
POOL_BUF = max(POOL_WINDOWS) - 1
D_FF = -(-8 * D_MODEL // (3 * 256)) * 256
CHUNK = 128
IN_COLS = 4 * MLSTM_WIDTH + 2 * N_MLSTM_HEADS + POOL_WIDTH
EPS = 1e-6

kernel_name = "hymba_mlstm_multiscale_pool_decoder_step"


def rmsnorm(x, g):
    xf = x.astype(jnp.float32)
    r = xf * lax.rsqrt(jnp.mean(xf * xf, axis=-1, keepdims=True) + EPS)
    return (r * g.astype(jnp.float32)).astype(x.dtype)


def head_rmsnorm(h, g):
    Bx, T, H, dh = h.shape
    h = h * lax.rsqrt(jnp.mean(h * h, axis=-1, keepdims=True) + EPS)
    return h.reshape(Bx, T, H * dh) * g.astype(jnp.float32)


def mlstm_chunkwise(q, k, v, i_pre, f_pre, C0, n0, m0, chunk):
    Bx, T, H, dh = q.shape
    nc = T // chunk

    def to_chunks(a):
        a = a.reshape((Bx, nc, chunk) + a.shape[2:])
        return jnp.moveaxis(jnp.moveaxis(a, 3, 2), 1, 0)

    logf = jax.nn.log_sigmoid(f_pre)
    xs = (to_chunks(q), to_chunks(k), to_chunks(v), to_chunks(i_pre), to_chunks(logf))
    causal = jnp.tril(jnp.ones((chunk, chunk), dtype=bool))

    def step(carry, xs_c):
        C, n, m = carry
        qc, kc, vc, ic, lfc = xs_c
        b = jnp.cumsum(lfc, axis=-1)
        inter = b + m[..., None]
        Dm = b[..., :, None] - b[..., None, :] + ic[..., None, :]
        Dm = jnp.where(causal, Dm, -jnp.inf)
        m_t = jnp.maximum(inter, jnp.max(Dm, axis=-1))
        w = jnp.exp(Dm - m_t[..., None])
        g = jnp.exp(inter - m_t)
        s = jnp.einsum('bhtd,bhsd->bhts', qc, kc) * w
        num = jnp.einsum('bhts,bhsd->bhtd', s, vc) + g[..., None] * jnp.einsum('bhvk,bhtk->bhtv', C, qc)
        den = jnp.sum(s, axis=-1) + g * jnp.einsum('bhk,bhtk->bht', n, qc)
        h = num / jnp.maximum(jnp.abs(den), jnp.exp(-m_t))[..., None]
        m_new = m_t[..., -1]
        g_end = jnp.exp(b[..., -1] + m - m_new)
        w_end = w[..., -1, :]
        C_new = g_end[..., None, None] * C + jnp.einsum('bhs,bhsv,bhsk->bhvk', w_end, vc, kc)
        n_new = g_end[..., None] * n + jnp.einsum('bhs,bhsk->bhk', w_end, kc)
        return (C_new, n_new, m_new), h

    (C1, n1, m1), hs = lax.scan(step, (C0, n0, m0), xs)
    hs = jnp.moveaxis(jnp.moveaxis(hs, 0, 1), 2, 3).reshape(Bx, T, H, dh)
    return hs, C1, n1, m1


def multiscale_pool(u_ext, pos0):
    Bx, Te, Wd = u_ext.shape
    T = Te - POOL_BUF
    csum = jnp.concatenate([jnp.zeros((Bx, 1, Wd), jnp.float32), jnp.cumsum(u_ext, axis=1)], axis=1)
    end = POOL_BUF + 1 + jnp.arange(T)
    pos = pos0 + jnp.arange(T)
    outs = []
    for gi, win in enumerate(POOL_WINDOWS):
        cs = csum[..., gi * POOL_GROUP_DIM:(gi + 1) * POOL_GROUP_DIM]
        cnt = jnp.minimum(win, pos + 1).astype(jnp.float32)
        outs.append((cs[:, end] - cs[:, end - win]) / cnt[None, :, None])
    return jnp.concatenate(outs, axis=-1) - u_ext[:, POOL_BUF:]


def trunk_layer(x, pos0, chunk, C0, n0, m0, buf0, g_mix, w_in, b_gate, g_head, w_pool,
                pool_scale, w_out, g_ffn, w_gate, w_up, w_down):
    Bx, T, _ = x.shape
    W, H, dh = MLSTM_WIDTH, N_MLSTM_HEADS, MLSTM_HEAD_DIM
    hn = rmsnorm(x, g_mix)
    proj = jnp.einsum('btd,dc->btc', hn, w_in).astype(jnp.float32)
    q = proj[..., 0:W].reshape(Bx, T, H, dh)
    k = proj[..., W:2 * W].reshape(Bx, T, H, dh) * (dh ** -0.5)
    v = proj[..., 2 * W:3 * W].reshape(Bx, T, H, dh)
    o = jax.nn.sigmoid(proj[..., 3 * W:4 * W])
    gates = proj[..., 4 * W:4 * W + 2 * H] + b_gate.astype(jnp.float32)
    u = proj[..., 4 * W + 2 * H:]
    h_cell, C1, n1, m1 = mlstm_chunkwise(q, k, v, gates[..., :H], gates[..., H:],
                                         C0.astype(jnp.float32), n0.astype(jnp.float32),
                                         m0.astype(jnp.float32), chunk)
    h_m = head_rmsnorm(h_cell, g_head) * o
    u_ext = jnp.concatenate([buf0.astype(jnp.float32), u], axis=1)
    p = multiscale_pool(u_ext, pos0).reshape(Bx, T, N_POOL_GROUPS, POOL_GROUP_DIM)
    p = jnp.einsum('btgc,gcd->btgd', p, w_pool).reshape(Bx, T, POOL_WIDTH) * pool_scale
    mix = jnp.concatenate([h_m, p], axis=-1).astype(x.dtype)
    x = x + jnp.einsum('btc,cd->btd', mix, w_out).astype(x.dtype)
    h2 = rmsnorm(x, g_ffn)
    a = jnp.einsum('btd,df->btf', h2, w_gate)
    bu = jnp.einsum('btd,df->btf', h2, w_up)
    x = x + jnp.einsum('btf,fd->btd', jax.nn.silu(a) * bu, w_down).astype(x.dtype)
    return x, C1, n1, m1, u_ext[:, -POOL_BUF:]


def setup_inputs(seed: int = 0) -> dict:
    key = jax.random.key(seed)
    ks = jax.random.split(key, 20)
    H, dh = N_MLSTM_HEADS, MLSTM_HEAD_DIM
    nrm = jax.random.normal
    f32 = jnp.float32
    b_i = 0.1 * nrm(ks[6], (DEPTH, H), f32)
    b_f = 3.0 + 0.5 * nrm(ks[7], (DEPTH, H), f32)
    return {
        "x_prompt": nrm(ks[0], (BATCH, SEQ, D_MODEL), f32),
        "x_sample": nrm(ks[1], (DEC_BATCH, DEC_SEQ, D_MODEL), f32),
        "state_mlstm_C": 0.05 * nrm(ks[2], (DEPTH, DEC_BATCH, H, dh, dh), f32),
        "state_mlstm_n": 0.5 * nrm(ks[3], (DEPTH, DEC_BATCH, H, dh), f32),
        "state_mlstm_m": nrm(ks[4], (DEPTH, DEC_BATCH, H), f32),
        "state_pool_buf": nrm(ks[5], (DEPTH, DEC_BATCH, POOL_BUF, POOL_WIDTH), f32),
        "g_mix": 1.0 + 0.05 * nrm(ks[8], (DEPTH, D_MODEL), f32),
        "w_in": nrm(ks[9], (DEPTH, D_MODEL, IN_COLS), f32) * D_MODEL ** -0.5,
        "b_gate": jnp.concatenate([b_i, b_f], axis=-1),
        "g_head": 1.0 + 0.05 * nrm(ks[10], (DEPTH, MLSTM_WIDTH), f32),
        "w_pool": nrm(ks[11], (DEPTH, N_POOL_GROUPS, POOL_GROUP_DIM, POOL_GROUP_DIM), f32) * POOL_GROUP_DIM ** -0.5,
        "pool_scale": 1.0 + 0.1 * nrm(ks[12], (DEPTH, POOL_WIDTH), f32),
        "w_out": nrm(ks[13], (DEPTH, MIX_WIDTH, D_MODEL), f32) * MIX_WIDTH ** -0.5,
        "g_ffn": 1.0 + 0.05 * nrm(ks[14], (DEPTH, D_MODEL), f32),
        "w_gate": nrm(ks[15], (DEPTH, D_MODEL, D_FF), f32) * D_MODEL ** -0.5,
        "w_up": nrm(ks[16], (DEPTH, D_MODEL, D_FF), f32) * D_MODEL ** -0.5,
        "w_down": nrm(ks[17], (DEPTH, D_FF, D_MODEL), f32) * D_FF ** -0.5,
        "g_final": 1.0 + 0.05 * nrm(ks[18], (D_MODEL,), f32),
    }


def reference(x_prompt, x_sample, state_mlstm_C, state_mlstm_n, state_mlstm_m, state_pool_buf,
              g_mix, w_in, b_gate, g_head, w_pool, pool_scale, w_out, g_ffn, w_gate, w_up,
              w_down, g_final):
    H, dh = N_MLSTM_HEADS, MLSTM_HEAD_DIM
    xp = x_prompt
    Cp, npl, mp, bp = [], [], [], []
    chunk_p = min(CHUNK, xp.shape[1])
    for l in range(DEPTH):
        C0 = jnp.zeros((xp.shape[0], H, dh, dh), jnp.float32)
        n0 = jnp.zeros((xp.shape[0], H, dh), jnp.float32)
        m0 = jnp.zeros((xp.shape[0], H), jnp.float32)
        buf0 = jnp.zeros((xp.shape[0], POOL_BUF, POOL_WIDTH), jnp.float32)
        xp, C1, n1, m1, b1 = trunk_layer(xp, 0, chunk_p, C0, n0, m0, buf0, g_mix[l], w_in[l],
                                         b_gate[l], g_head[l], w_pool[l], pool_scale[l], w_out[l],
                                         g_ffn[l], w_gate[l], w_up[l], w_down[l])
        Cp.append(C1); npl.append(n1); mp.append(m1); bp.append(b1)
    y_prompt = rmsnorm(xp, g_final)
    xs = x_sample
    Cs, ns, ms, bs = [], [], [], []
    for l in range(DEPTH):
        xs, C1, n1, m1, b1 = trunk_layer(xs, PAST_LEN, xs.shape[1], state_mlstm_C[l], state_mlstm_n[l],
                                         state_mlstm_m[l], state_pool_buf[l], g_mix[l], w_in[l],
                                         b_gate[l], g_head[l], w_pool[l], pool_scale[l], w_out[l],
                                         g_ffn[l], w_gate[l], w_up[l], w_down[l])
        Cs.append(C1); ns.append(n1); ms.append(m1); bs.append(b1)
    y_sample = rmsnorm(xs, g_final)
    sd = state_mlstm_C.dtype
    C_prompt = jnp.stack(Cp, 0).astype(sd)
    n_prompt = jnp.stack(npl, 0).astype(sd)
    m_prompt = jnp.stack(mp, 0).astype(sd)
    pool_buf_prompt = jnp.stack(bp, 0).astype(state_pool_buf.dtype)
    C_sample = jnp.stack(Cs, 0).astype(sd)
    n_sample = jnp.stack(ns, 0).astype(sd)
    m_sample = jnp.stack(ms, 0).astype(sd)
    pool_buf_sample = jnp.stack(bs, 0).astype(state_pool_buf.dtype)
    return (y_prompt, y_sample, C_prompt, n_prompt, m_prompt, pool_buf_prompt,
            C_sample, n_sample, m_sample, pool_buf_sample)
```

```python
import functools

import jax
import jax.numpy as jnp
from jax import lax
from jax.experimental import pallas as pl
from jax.experimental.pallas import tpu as pltpu

F32 = jnp.float32
BF16 = jnp.bfloat16

EPS = 1e-6
N_HEADS = 4
HEAD_DIM = 128
MLSTM_WIDTH = N_HEADS * HEAD_DIM
POOL_WINDOWS = (2, 4, 8, 16)
POOL_GROUP = 128
POOL_WIDTH = POOL_GROUP * len(POOL_WINDOWS)
POOL_BUF = max(POOL_WINDOWS) - 1
POOL_CARRY = POOL_BUF + 1
CHUNK = 128
LANES = 128
GATE_SLAB = LANES

Q0, K0, V0, O0 = 0, MLSTM_WIDTH, 2 * MLSTM_WIDTH, 3 * MLSTM_WIDTH
U0 = 4 * MLSTM_WIDTH
G0 = U0 + POOL_WIDTH
IN_COLS_PACKED = G0 + GATE_SLAB

V7X_VMEM_BYTES = 64 * 1024 * 1024

_NT = (((1,), (1,)), ((), ()))


def _dot(a, b):
    return jnp.dot(a, b, preferred_element_type=F32)


def _dot_nt(a, b):
    return lax.dot_general(a, b, _NT, preferred_element_type=F32)


def _rms(x, g):
    return x * lax.rsqrt(jnp.mean(x * x, axis=-1, keepdims=True) + EPS) * g


def _log_sigmoid(x):
    return jnp.minimum(x, 0.0) - jnp.log1p(jnp.exp(-jnp.abs(x)))


def _lane_col(slab, j):
    lane = lax.broadcasted_iota(jnp.int32, slab.shape, 1)
    return jnp.sum(jnp.where(lane == j, slab, 0.0), axis=-1, keepdims=True)


def _split3(x):
    hi = x.astype(BF16)
    r1 = x - hi.astype(F32)
    mid = r1.astype(BF16)
    lo = (r1 - mid.astype(F32)).astype(BF16)
    return hi, mid, lo


def _cumsum_rows(x):
    r = lax.broadcasted_iota(jnp.int32, (CHUNK, CHUNK), 0)
    c = lax.broadcasted_iota(jnp.int32, (CHUNK, CHUNK), 1)
    tril = (r >= c).astype(BF16)
    hi, mid, lo = _split3(x)
    return _dot(tril, hi) + _dot(tril, mid) + _dot(tril, lo)


def _mlstm_chunk(qf, kf, vf, i_col, b_col, i_row, b_row, C, n, m, causal):
    qb, kb, vb = qf.astype(BF16), kf.astype(BF16), vf.astype(BF16)
    inter = b_col + m
    dm = (b_col - b_row) + i_row
    dm = jnp.where(causal, dm, -jnp.inf)
    m_t = jnp.maximum(inter, jnp.max(dm, axis=-1, keepdims=True))
    w = jnp.exp(dm - m_t)
    g = jnp.exp(inter - m_t)
    s = _dot_nt(qb, kb) * w
    num = _dot(s.astype(BF16), vb) + g * _dot_nt(qb, C.astype(BF16))
    den = jnp.sum(s, axis=-1, keepdims=True) + g * jnp.sum(qf * n, axis=-1, keepdims=True)
    h = num * (1.0 / jnp.maximum(jnp.abs(den), jnp.exp(-m_t)))
    m_new = m_t[CHUNK - 1:CHUNK, :]
    b_last = b_col[CHUNK - 1:CHUNK, :]
    g_end = jnp.exp(b_last + m - m_new)
    w_end = jnp.exp((b_last - b_col) + i_col - m_new)
    wv_t = (w_end * vf).T.astype(BF16)
    C_new = g_end * C + _dot(wv_t, kb)
    n_new = g_end * n + jnp.sum(w_end * kf, axis=0, keepdims=True)
    return h, C_new, n_new, m_new


def _head_out(h, g_head, o):
    hn = h * lax.rsqrt(jnp.mean(h * h, axis=-1, keepdims=True) + EPS)
    return hn * g_head * jax.nn.sigmoid(o)


def _prompt_mixer_kernel(x_ref, gmix_ref, win_ref, bslab_ref, ghead_ref, wpool_ref,
                         pscale_ref, wout_ref,
                         xo_ref, C_ref, n_ref, m_ref, pb_ref, *, tm):
    t = pl.program_id(1)

    @pl.when(t == 0)
    def _():
        C_ref[...] = jnp.zeros_like(C_ref)
        n_ref[...] = jnp.zeros_like(n_ref)
        m_ref[...] = jnp.zeros_like(m_ref)
        pb_ref[...] = jnp.zeros_like(pb_ref)

    x = x_ref[...]
    hn = _rms(x, gmix_ref[...]).astype(BF16)

    def proj(c0, width):
        return _dot(hn, win_ref[:, c0:c0 + width])

    q = proj(Q0, MLSTM_WIDTH)
    k = proj(K0, MLSTM_WIDTH) * (HEAD_DIM ** -0.5)
    v = proj(V0, MLSTM_WIDTH)
    o = proj(O0, MLSTM_WIDTH)
    u = proj(U0, POOL_WIDTH)
    gates = proj(G0, GATE_SLAB) + bslab_ref[...]

    r = lax.broadcasted_iota(jnp.int32, (CHUNK, CHUNK), 0)
    c = lax.broadcasted_iota(jnp.int32, (CHUNK, CHUNK), 1)
    causal = r >= c

    hm_rows = []
    for ci in range(tm // CHUNK):
        rows = slice(ci * CHUNK, (ci + 1) * CHUNK)
        gc = gates[rows, :]
        cs = _cumsum_rows(_log_sigmoid(gc))
        gc_t = gc.T
        cs_t = cs.T
        heads = []
        for h in range(N_HEADS):
            cols = slice(h * HEAD_DIM, (h + 1) * HEAD_DIM)
            hc, C_new, n_new, m_new = _mlstm_chunk(
                q[rows, cols], k[rows, cols], v[rows, cols],
                _lane_col(gc, h), _lane_col(cs, N_HEADS + h),
                gc_t[h:h + 1, :], cs_t[N_HEADS + h:N_HEADS + h + 1, :],
                C_ref[h], n_ref[h:h + 1, :], m_ref[h:h + 1, 0:1], causal)
            C_ref[h] = C_new
            n_ref[h:h + 1, :] = n_new
            m_ref[h:h + 1, :] = jnp.broadcast_to(m_new, (1, LANES))
            heads.append(_head_out(hc, ghead_ref[:, cols], o[rows, cols]))
        hm_rows.append(jnp.concatenate(heads, axis=-1))
    hm = jnp.concatenate(hm_rows, axis=0)

    ext = jnp.concatenate([pb_ref[...], u], axis=0)
    pb_ref[...] = u[tm - POOL_CARRY:, :]
    pos = t * tm + lax.broadcasted_iota(jnp.int32, (tm, POOL_GROUP), 0)
    pooled = []
    for gi, win in enumerate(POOL_WINDOWS):
        cols = slice(gi * POOL_GROUP, (gi + 1) * POOL_GROUP)
        a = ext[:, cols]
        step = 1
        while step < win:
            a = a[step:, :] + a[:a.shape[0] - step, :]
            step *= 2
        a = a[a.shape[0] - tm:, :]
        cnt = jnp.minimum(win, pos + 1).astype(F32)
        p = a / cnt - u[:, cols]
        pooled.append(_dot(p.astype(BF16), wpool_ref[gi]) * pscale_ref[:, cols])
    mix = jnp.concatenate([hm] + pooled, axis=-1).astype(BF16)
    xo_ref[...] = x + _dot(mix, wout_ref[...])


def _const_spec(shape):
    nd = len(shape)
    return pl.BlockSpec(shape, lambda *_: (0,) * nd, pipeline_mode=pl.Buffered(1))


def _vmem_limit(nbytes):
    return int(min(V7X_VMEM_BYTES - (4 << 20), max(nbytes, 16 << 20)))


def _prompt_mixer(x, gmix, win, bslab, ghead, wpool, pscale, wout, *, tm):
    B, T, D = x.shape
    nt = T // tm
    grid = (B, nt)
    tile = pl.BlockSpec((None, tm, D), lambda b, t: (b, t, 0))
    out_shapes = (
        jax.ShapeDtypeStruct((B, T, D), F32),
        jax.ShapeDtypeStruct((B, N_HEADS, HEAD_DIM, HEAD_DIM), F32),
        jax.ShapeDtypeStruct((B, 8, HEAD_DIM), F32),
        jax.ShapeDtypeStruct((B, 8, LANES), F32),
        jax.ShapeDtypeStruct((B, POOL_CARRY, POOL_WIDTH), F32),
    )
    out_specs = (
        tile,
        pl.BlockSpec((None, N_HEADS, HEAD_DIM, HEAD_DIM), lambda b, t: (b, 0, 0, 0)),
        pl.BlockSpec((None, 8, HEAD_DIM), lambda b, t: (b, 0, 0)),
        pl.BlockSpec((None, 8, LANES), lambda b, t: (b, 0, 0)),
        pl.BlockSpec((None, POOL_CARRY, POOL_WIDTH), lambda b, t: (b, 0, 0)),
    )
    in_specs = [tile] + [_const_spec(a.shape) for a in (gmix, win, bslab, ghead, wpool, pscale, wout)]
    weights = sum(a.size * a.dtype.itemsize for a in (win, wpool, wout))
    est = weights + 4 * tm * D * 4 + 16 * tm * IN_COLS_PACKED * 4
    return pl.pallas_call(
        functools.partial(_prompt_mixer_kernel, tm=tm),
        grid=grid, in_specs=in_specs, out_specs=out_specs, out_shape=out_shapes,
        compiler_params=pltpu.CompilerParams(
            dimension_semantics=("arbitrary", "arbitrary"),
            vmem_limit_bytes=_vmem_limit(est)),
        name="prompt_mixer",
    )(x, gmix, win, bslab, ghead, wpool, pscale, wout)


def _ffn_kernel(*refs, has_mix, final):
    refs = list(refs)
    x_ref = refs.pop(0)
    x = x_ref[...]
    if has_mix:
        mix_ref, wout_ref = refs.pop(0), refs.pop(0)
        x = x + _dot(mix_ref[...].astype(BF16), wout_ref[...])
    gffn_ref, wg_ref, wu_ref, wd_ref = refs[:4]
    refs = refs[4:]
    h2 = _rms(x, gffn_ref[...]).astype(BF16)
    a = _dot(h2, wg_ref[...])
    bu = _dot(h2, wu_ref[...])
    act = (a * jax.nn.sigmoid(a) * bu).astype(BF16)
    y = x + _dot(act, wd_ref[...])
    if final:
        gfin_ref, out_ref = refs
        out_ref[...] = _rms(y, gfin_ref[...])
    else:
        (out_ref,) = refs
        out_ref[...] = y


def _ffn(x, gffn, wg, wu, wd, *, tm, mix=None, wout=None, gfin=None):
    M, D = x.shape
    dff = wg.shape[1]
    tile = pl.BlockSpec((tm, D), lambda i: (i, 0))
    args, in_specs = [x], [tile]
    if mix is not None:
        args += [mix, wout]
        in_specs += [tile, _const_spec(wout.shape)]
    args += [gffn, wg, wu, wd]
    in_specs += [_const_spec(a.shape) for a in (gffn, wg, wu, wd)]
    if gfin is not None:
        args.append(gfin)
        in_specs.append(_const_spec(gfin.shape))
    weights = sum(a.size * a.dtype.itemsize for a in args[1:] if a.dtype == BF16)
    est = weights + 6 * tm * D * 4 + 4 * tm * dff * 4
    return pl.pallas_call(
        functools.partial(_ffn_kernel, has_mix=mix is not None, final=gfin is not None),
        grid=(M // tm,), in_specs=in_specs, out_specs=tile,
        out_shape=jax.ShapeDtypeStruct((M, D), F32),
        compiler_params=pltpu.CompilerParams(
            dimension_semantics=("arbitrary",), vmem_limit_bytes=_vmem_limit(est)),
        name="ffn_final" if gfin is not None else "ffn",
    )(*args)


def _sample_proj_kernel(x_ref, gmix_ref, win_ref, proj_ref):
    hn = _rms(x_ref[...], gmix_ref[...]).astype(BF16)
    proj_ref[...] = _dot(hn, win_ref[...])


def _sample_proj(x, gmix, win):
    M = x.shape[0]
    est = 2 * win.size * 2 + 8 * M * IN_COLS_PACKED * 4
    return pl.pallas_call(
        _sample_proj_kernel,
        out_shape=jax.ShapeDtypeStruct((M, IN_COLS_PACKED), F32),
        compiler_params=pltpu.CompilerParams(vmem_limit_bytes=_vmem_limit(est)),
        name="sample_proj",
    )(x, gmix, win)


def _sample_step_kernel(proj_ref, bslab_ref, ghead_ref, wpool_ref, pscale_ref,
                        C_ref, n_ref, m_ref, buf_ref,
                        mix_ref, Co_ref, no_ref, mo_ref, bufo_ref,
                        g_s, wvt_s, kb_s, qb_s, cq_s, *, bb):
    i = pl.program_id(0)
    nb = proj_ref.shape[0]
    scale = HEAD_DIM ** -0.5

    def gate_terms(h):
        gs = proj_ref[:, G0:G0 + GATE_SLAB] + bslab_ref[...]
        i_c = _lane_col(gs, h)
        lf = _lane_col(_log_sigmoid(gs), N_HEADS + h)
        m0 = _lane_col(m_ref[...], h)
        inter = lf + m0
        m_t = jnp.maximum(inter, i_c)
        return jnp.exp(i_c - m_t), jnp.exp(inter - m_t), m_t

    @pl.when(i == 0)
    def _():
        for h in range(N_HEADS):
            cols = slice(h * HEAD_DIM, (h + 1) * HEAD_DIM)
            w, g, _ = gate_terms(h)
            g_s[h] = jnp.broadcast_to(g, (nb, LANES))
            v = proj_ref[:, V0 + h * HEAD_DIM:V0 + (h + 1) * HEAD_DIM]
            wvt_s[h] = (w * v).T
            kb_s[h] = (proj_ref[:, K0 + h * HEAD_DIM:K0 + (h + 1) * HEAD_DIM] * scale).astype(BF16)
            qb_s[h] = proj_ref[:, Q0 + h * HEAD_DIM:Q0 + (h + 1) * HEAD_DIM].astype(BF16)
            cq_s[h] = jnp.zeros((HEAD_DIM, nb), F32)

    lane = lax.broadcasted_iota(jnp.int32, (HEAD_DIM, nb), 1)

    def pair_body(j, carry):
        b = i * bb + j
        sel = lane == b
        for h in range(N_HEADS):
            Cb = C_ref[j, h]
            r = _dot_nt(Cb.astype(BF16), qb_s[h])
            cq_s[h] = jnp.where(sel, r, cq_s[h])
            lhs = jnp.where(sel, wvt_s[h], 0.0).astype(BF16)
            outer = _dot(lhs, kb_s[h])
            Co_ref[j, h] = g_s[h, pl.ds(b, 1), :] * Cb + outer
        return carry

    lax.fori_loop(0, bb, pair_body, 0)

    @pl.when(i == pl.num_programs(0) - 1)
    def _():
        heads, n_heads = [], []
        lane_m = lax.broadcasted_iota(jnp.int32, (nb, LANES), 1)
        m_out = jnp.zeros((nb, LANES), F32)
        for h in range(N_HEADS):
            cols = slice(h * HEAD_DIM, (h + 1) * HEAD_DIM)
            w, g, m_t = gate_terms(h)
            qf = proj_ref[:, Q0 + h * HEAD_DIM:Q0 + (h + 1) * HEAD_DIM]
            kf = proj_ref[:, K0 + h * HEAD_DIM:K0 + (h + 1) * HEAD_DIM] * scale
            vf = proj_ref[:, V0 + h * HEAD_DIM:V0 + (h + 1) * HEAD_DIM]
            of = proj_ref[:, O0 + h * HEAD_DIM:O0 + (h + 1) * HEAD_DIM]
            n0 = n_ref[:, cols]
            qk = jnp.sum(qf.astype(BF16).astype(F32) * kf.astype(BF16).astype(F32),
                         axis=-1, keepdims=True)
            s = qk * w
            num = s * vf + g * cq_s[h].T
            den = s + g * jnp.sum(n0 * qf, axis=-1, keepdims=True)
            hc = num * (1.0 / jnp.maximum(jnp.abs(den), jnp.exp(-m_t)))
            heads.append(_head_out(hc, ghead_ref[:, cols], of))
            n_heads.append(g * n0 + w * kf)
            m_out = jnp.where(lane_m == h, m_t, m_out)
        no_ref[...] = jnp.concatenate(n_heads, axis=-1)
        mo_ref[...] = m_out

        u = proj_ref[:, U0:U0 + POOL_WIDTH]
        pooled = []
        for gi, win in enumerate(POOL_WINDOWS):
            cols = slice(gi * POOL_GROUP, (gi + 1) * POOL_GROUP)
            a = u[:, cols]
            for jrow in range(POOL_BUF - (win - 1), POOL_BUF):
                a = a + buf_ref[jrow, :, cols]
            p = a / float(win) - u[:, cols]
            pooled.append(_dot(p.astype(BF16), wpool_ref[gi]) * pscale_ref[:, cols])
        mix_ref[...] = jnp.concatenate(heads + pooled, axis=-1)
        for jrow in range(POOL_BUF - 1):
            bufo_ref[jrow] = buf_ref[jrow + 1]
        bufo_ref[POOL_BUF - 1] = u


def _sample_step(proj, bslab, ghead, wpool, pscale, C, n, m, buf_t, *, bb):
    nb = proj.shape[0]
    D = MLSTM_WIDTH + POOL_WIDTH
    cblk = pl.BlockSpec((bb, N_HEADS, HEAD_DIM, HEAD_DIM), lambda i: (i, 0, 0, 0))
    small = (proj, bslab, ghead, wpool, pscale)
    in_specs = [_const_spec(a.shape) for a in small] + [
        cblk, _const_spec(n.shape), _const_spec(m.shape), _const_spec(buf_t.shape)]
    out_shapes = (
        jax.ShapeDtypeStruct((nb, D), F32),
        jax.ShapeDtypeStruct(C.shape, F32),
        jax.ShapeDtypeStruct(n.shape, F32),
        jax.ShapeDtypeStruct((nb, LANES), F32),
        jax.ShapeDtypeStruct(buf_t.shape, F32),
    )
    full = lambda s: pl.BlockSpec(s, lambda i: (0,) * len(s))
    out_specs = (full((nb, D)), cblk, full(n.shape), full((nb, LANES)), full(buf_t.shape))
    scratch = [
        pltpu.VMEM((N_HEADS, nb, LANES), F32),
        pltpu.VMEM((N_HEADS, HEAD_DIM, nb), F32),
        pltpu.VMEM((N_HEADS, nb, HEAD_DIM), BF16),
        pltpu.VMEM((N_HEADS, nb, HEAD_DIM), BF16),
        pltpu.VMEM((N_HEADS, HEAD_DIM, nb), F32),
    ]
    est = (4 * bb * N_HEADS * HEAD_DIM * HEAD_DIM * 4 + 4 * buf_t.size * 4
           + 8 * proj.size * 4)
    return pl.pallas_call(
        functools.partial(_sample_step_kernel, bb=bb),
        grid=(nb // bb,), in_specs=in_specs, out_specs=out_specs, out_shape=out_shapes,
        scratch_shapes=scratch,
        compiler_params=pltpu.CompilerParams(
            dimension_semantics=("arbitrary",), vmem_limit_bytes=_vmem_limit(est)),
        name="sample_step",
    )(proj, bslab, ghead, wpool, pscale, C, n, m, buf_t)


def _pack_w_in(w_in):
    depth, d, _ = w_in.shape
    n_gate = 2 * N_HEADS
    pad = jnp.zeros((depth, d, GATE_SLAB - n_gate), w_in.dtype)
    packed = jnp.concatenate(
        [w_in[:, :, :U0], w_in[:, :, U0 + n_gate:], w_in[:, :, U0:U0 + n_gate], pad], axis=-1)
    return packed.astype(BF16)


def kernel(x_prompt, x_sample, state_mlstm_C, state_mlstm_n, state_mlstm_m, state_pool_buf,
           g_mix, w_in, b_gate, g_head, w_pool, pool_scale, w_out, g_ffn, w_gate, w_up,
           w_down, g_final):
    depth = w_in.shape[0]
    B, T, D = x_prompt.shape
    nb = x_sample.shape[0]
    tm_mixer, tm_ffn, bb = 256, 256, 8

    win = _pack_w_in(w_in)
    bslab = jnp.pad(b_gate, ((0, 0), (0, GATE_SLAB - b_gate.shape[1])))[:, None, :]
    wpool = w_pool.astype(BF16)
    wout = w_out.astype(BF16)
    wg, wu, wd = w_gate.astype(BF16), w_up.astype(BF16), w_down.astype(BF16)
    row = lambda a: a[:, None, :]
    gmix, ghead, pscale, gffn = row(g_mix), row(g_head), row(pool_scale), row(g_ffn)
    gfin = g_final[None, :]

    xp = x_prompt
    Cp, npr, mp, bp = [], [], [], []
    for l in range(depth):
        xm, C1, n1, m1, pb1 = _prompt_mixer(xp, gmix[l], win[l], bslab[l], ghead[l], wpool[l],
                                            pscale[l], wout[l], tm=tm_mixer)
        last = l == depth - 1
        xp = _ffn(xm.reshape(B * T, D), gffn[l], wg[l], wu[l], wd[l], tm=tm_ffn,
                  gfin=gfin if last else None).reshape(B, T, D)
        Cp.append(C1)
        npr.append(n1[:, :N_HEADS, :])
        mp.append(m1[:, :N_HEADS, 0])
        bp.append(pb1[:, POOL_CARRY - POOL_BUF:, :])
    y_prompt = xp

    xs = x_sample.reshape(nb, D)
    Cs, ns, ms, bs = [], [], [], []
    for l in range(depth):
        proj = _sample_proj(xs, gmix[l], win[l])
        m_pad = jnp.pad(state_mlstm_m[l], ((0, 0), (0, LANES - N_HEADS)))
        buf_t = jnp.transpose(state_pool_buf[l], (1, 0, 2))
        mix, C1, n1, m1, buf1 = _sample_step(
            proj, bslab[l], ghead[l], wpool[l], pscale[l], state_mlstm_C[l],
            state_mlstm_n[l].reshape(nb, MLSTM_WIDTH), m_pad, buf_t, bb=bb)
        last = l == depth - 1
        xs = _ffn(xs, gffn[l], wg[l], wu[l], wd[l], tm=nb, mix=mix, wout=wout[l],
                  gfin=gfin if last else None)
        Cs.append(C1)
        ns.append(n1.reshape(nb, N_HEADS, HEAD_DIM))
        ms.append(m1[:, :N_HEADS])
        bs.append(jnp.transpose(buf1, (1, 0, 2)))
    y_sample = xs.reshape(nb, 1, D)

    st = lambda xs_: jnp.stack(xs_, 0)
    return (y_prompt, y_sample, st(Cp), st(npr), st(mp), st(bp),
            st(Cs), st(ns), st(ms), st(bs))
```

```python
import functools

import jax
import jax.numpy as jnp
from jax import lax
from jax.experimental import pallas as pl
from jax.experimental.pallas import tpu as pltpu

F32 = jnp.float32
BF16 = jnp.bfloat16

EPS = 1e-6
N_HEADS = 4
HEAD_DIM = 128
MLSTM_WIDTH = N_HEADS * HEAD_DIM
POOL_WINDOWS = (2, 4, 8, 16)
POOL_GROUP = 128
POOL_WIDTH = POOL_GROUP * len(POOL_WINDOWS)
POOL_BUF = max(POOL_WINDOWS) - 1
POOL_CARRY = POOL_BUF + 1
CHUNK = 256
LANES = 128
SUBLANES = 8
N_ROWS = 2 * SUBLANES
GATE_SLAB = LANES

Q0, K0, U0 = 0, MLSTM_WIDTH, 2 * MLSTM_WIDTH
GI0 = U0 + POOL_WIDTH
GF0 = GI0 + GATE_SLAB
TOK_COLS = GF0 + GATE_SLAB
VT0, OT0 = 0, MLSTM_WIDTH
TR_ROWS = 2 * MLSTM_WIDTH
SV0 = TOK_COLS
SO0 = TOK_COLS + MLSTM_WIDTH
SAMPLE_COLS = TOK_COLS + TR_ROWS

V7X_VMEM_BYTES = 64 * 1024 * 1024

_NT = (((1,), (1,)), ((), ()))
_TN = (((0,), (0,)), ((), ()))


def _dot(a, b):
    return jnp.dot(a, b, preferred_element_type=F32)


def _dot_nt(a, b):
    return lax.dot_general(a, b, _NT, preferred_element_type=F32)


def _dot_tn(a, b):
    return lax.dot_general(a, b, _TN, preferred_element_type=F32)


def _rms(x, g):
    return x * lax.rsqrt(jnp.mean(x * x, axis=-1, keepdims=True) + EPS) * g


def _log_sigmoid(x):
    return jnp.minimum(x, 0.0) - jnp.log1p(jnp.exp(-jnp.abs(x)))


def _lane_col(slab, j):
    lane = lax.broadcasted_iota(jnp.int32, slab.shape, 1)
    return jnp.sum(jnp.where(lane == j, slab, 0.0), axis=-1, keepdims=True)


def _split3(x):
    hi = x.astype(BF16)
    r1 = x - hi.astype(F32)
    mid = r1.astype(BF16)
    lo = (r1 - mid.astype(F32)).astype(BF16)
    return hi, mid, lo


def _cumsum_rows(x):
    r = lax.broadcasted_iota(jnp.int32, (CHUNK, CHUNK), 0)
    c = lax.broadcasted_iota(jnp.int32, (CHUNK, CHUNK), 1)
    tril = (r >= c).astype(BF16)
    hi, mid, lo = _split3(x)
    return _dot(tril, hi) + _dot(tril, mid) + _dot(tril, lo)


def _mlstm_chunk_t(q, k, vT, a_col, a_row, b_row, C, n_rows, m, mask_st):
    qb, kb = q.astype(BF16), k.astype(BF16)
    by_q = _dot_nt(jnp.concatenate([kb, C.astype(BF16), n_rows.astype(BF16)], axis=0), qb)
    sT = by_q[:CHUNK]
    cqT = by_q[CHUNK:CHUNK + HEAD_DIM]
    qn = by_q[CHUNK + HEAD_DIM:CHUNK + HEAD_DIM + 1]
    aT = jnp.where(mask_st, a_col, -jnp.inf)
    M = jnp.maximum(jnp.max(aT, axis=0, keepdims=True), m)
    swT = sT * jnp.exp(aT - M)
    g = jnp.exp(m - M)
    den = jnp.sum(swT, axis=0, keepdims=True) + g * qn
    inv = 1.0 / jnp.maximum(jnp.abs(den), jnp.exp(-(b_row + M)))
    numT = _dot(vT.astype(BF16), swT.astype(BF16)) + g * cqT
    hT = numT * inv
    M_last = M[:, CHUNK - 1:CHUNK]
    m_new = b_row[:, CHUNK - 1:CHUNK] + M_last
    g_end = jnp.exp(m - M_last)
    w_end = jnp.exp(a_row - M_last)
    w_rows = jnp.broadcast_to(w_end, (N_ROWS, CHUNK))
    by_k = _dot(jnp.concatenate([(vT * w_end).astype(BF16), w_rows.astype(BF16)], axis=0), kb)
    C_new = g_end * C + by_k[:HEAD_DIM]
    n_new = g_end * n_rows + by_k[HEAD_DIM:]
    return hT, C_new, n_new, m_new


def _in_proj_items(x, gmix_ref, wtok_ref, wtr_ref, bslab_ref, tok_ref, tr_ref):
    hn = _rms(x, gmix_ref[...]).astype(BF16)
    yield
    for c0, width in ((GI0, 2 * GATE_SLAB), (Q0, MLSTM_WIDTH), (K0, MLSTM_WIDTH),
                      (U0, POOL_WIDTH)):
        blk = _dot(hn, wtok_ref[:, c0:c0 + width])
        if c0 == K0:
            blk = blk * (HEAD_DIM ** -0.5)
        if c0 == GI0:
            blk = blk + bslab_ref[...]
        tok_ref[:, c0:c0 + width] = blk
        yield
    for r0 in (VT0, OT0):
        tr_ref[r0:r0 + MLSTM_WIDTH, :] = _dot_nt(wtr_ref[r0:r0 + MLSTM_WIDTH, :], hn)
        yield


def _mix_items(x_ref, tok_ref, tr_ref, t, ghead_ref, wpool_ref, pscale_ref, wout_ref,
               C_ref, n_ref, m_ref, pb_ref, out_ref, tm):
    r = lax.broadcasted_iota(jnp.int32, (CHUNK, CHUNK), 0)
    c = lax.broadcasted_iota(jnp.int32, (CHUNK, CHUNK), 1)
    mask_st = r <= c

    hm_cols = []
    for ci in range(tm // CHUNK):
        rows = slice(ci * CHUNK, (ci + 1) * CHUNK)
        cs = _cumsum_rows(_log_sigmoid(tok_ref[rows, GF0:GF0 + GATE_SLAB]))
        a_slab = tok_ref[rows, GI0:GI0 + GATE_SLAB] - cs
        cs_t = cs.T
        a_t = a_slab.T
        yield
        heads = []
        for h in range(N_HEADS):
            cols = slice(h * HEAD_DIM, (h + 1) * HEAD_DIM)
            hT, C_new, n_new, m_new = _mlstm_chunk_t(
                tok_ref[rows, Q0 + h * HEAD_DIM:Q0 + (h + 1) * HEAD_DIM],
                tok_ref[rows, K0 + h * HEAD_DIM:K0 + (h + 1) * HEAD_DIM],
                tr_ref[VT0 + h * HEAD_DIM:VT0 + (h + 1) * HEAD_DIM, rows],
                a_slab[:, h:h + 1], a_t[h:h + 1, :], cs_t[h:h + 1, :],
                C_ref[h], n_ref[h], m_ref[h:h + 1, 0:1], mask_st)
            C_ref[h] = C_new
            n_ref[h] = n_new
            m_ref[h:h + 1, :] = jnp.broadcast_to(m_new, (1, LANES))
            hnT = hT * lax.rsqrt(jnp.mean(hT * hT, axis=0, keepdims=True) + EPS)
            oT = tr_ref[OT0 + h * HEAD_DIM:OT0 + (h + 1) * HEAD_DIM, rows]
            heads.append((hnT * ghead_ref[cols, :] * jax.nn.sigmoid(oT)).astype(BF16))
            yield
        hm_cols.append(jnp.concatenate(heads, axis=0))
    hmT = jnp.concatenate(hm_cols, axis=1)

    u = tok_ref[:, U0:U0 + POOL_WIDTH]
    ext = jnp.concatenate([pb_ref[...], u], axis=0)
    pb_ref[...] = u[tm - POOL_CARRY:, :]
    pos = t * tm + lax.broadcasted_iota(jnp.int32, (tm, POOL_GROUP), 0)
    pooled = []
    for gi, win in enumerate(POOL_WINDOWS):
        cols = slice(gi * POOL_GROUP, (gi + 1) * POOL_GROUP)
        a = ext[:, cols]
        step = 1
        while step < win:
            a = a[step:, :] + a[:a.shape[0] - step, :]
            step *= 2
        a = a[a.shape[0] - tm:, :]
        cnt = jnp.minimum(win, pos + 1).astype(F32)
        p = a / cnt - u[:, cols]
        pooled.append((_dot(p.astype(BF16), wpool_ref[gi]) * pscale_ref[:, cols]).astype(BF16))
    yield
    pm = jnp.concatenate(pooled, axis=-1)
    out_ref[...] = (x_ref[...] + _dot_tn(hmT, wout_ref[:MLSTM_WIDTH, :])
                    + _dot(pm, wout_ref[MLSTM_WIDTH:, :]))
    yield


def _interleave(*streams):
    streams = list(streams)
    while streams:
        for s in list(streams):
            try:
                next(s)
            except StopIteration:
                streams.remove(s)


def _prompt_mixer_kernel(xcur_ref, xnext_ref, gmix_ref, wtok_ref, wtr_ref, bslab_ref, ghead_ref,
                         wpool_ref, pscale_ref, wout_ref,
                         xo_ref, C_ref, n_ref, m_ref, pb_ref,
                         tok_a, tr_a, tok_b, tr_b, *, tm, nt):
    s = pl.program_id(0)
    t0 = lax.rem(2 * s, nt)
    proj_args = (gmix_ref, wtok_ref, wtr_ref, bslab_ref)
    mix_args = (ghead_ref, wpool_ref, pscale_ref, wout_ref, C_ref, n_ref, m_ref, pb_ref)

    @pl.when(s == 0)
    def _():
        _interleave(_in_proj_items(xcur_ref[0], *proj_args, tok_a, tr_a))

    @pl.when(t0 == 0)
    def _():
        C_ref[...] = jnp.zeros_like(C_ref)
        n_ref[...] = jnp.zeros_like(n_ref)
        m_ref[...] = jnp.zeros_like(m_ref)
        pb_ref[...] = jnp.zeros_like(pb_ref)

    _interleave(
        _mix_items(xcur_ref.at[0], tok_a, tr_a, t0, *mix_args, xo_ref.at[0], tm),
        _in_proj_items(xcur_ref[1], *proj_args, tok_b, tr_b))
    _interleave(
        _mix_items(xcur_ref.at[1], tok_b, tr_b, t0 + 1, *mix_args, xo_ref.at[1], tm),
        _in_proj_items(xnext_ref[0], *proj_args, tok_a, tr_a))


def _const_spec(shape):
    nd = len(shape)
    return pl.BlockSpec(shape, lambda *_: (0,) * nd, pipeline_mode=pl.Buffered(1))


def _vmem_limit(nbytes):
    return int(min(V7X_VMEM_BYTES - (4 << 20), max(nbytes, 16 << 20)))


def _prompt_mixer(x, gmix, wtok, wtr, bslab, ghead_rep, wpool, pscale, wout, *, tm):
    B, T, D = x.shape
    nt = T // tm
    assert nt % 2 == 0, "a grid step covers two tiles of one sequence"
    ntiles = B * nt
    xt = x.reshape(ntiles, tm, D)
    pair = pl.BlockSpec((2, tm, D), lambda s: (s, 0, 0))
    nxt = pl.BlockSpec((1, tm, D), lambda s: (jnp.minimum(2 * s + 2, ntiles - 1), 0, 0))
    seq = lambda s: (2 * s) // nt
    out_shapes = (
        jax.ShapeDtypeStruct((ntiles, tm, D), F32),
        jax.ShapeDtypeStruct((B, N_HEADS, HEAD_DIM, HEAD_DIM), F32),
        jax.ShapeDtypeStruct((B, N_HEADS, N_ROWS, HEAD_DIM), F32),
        jax.ShapeDtypeStruct((B, SUBLANES, LANES), F32),
        jax.ShapeDtypeStruct((B, POOL_CARRY, POOL_WIDTH), F32),
    )
    out_specs = (
        pair,
        pl.BlockSpec((None, N_HEADS, HEAD_DIM, HEAD_DIM), lambda s: (seq(s), 0, 0, 0)),
        pl.BlockSpec((None, N_HEADS, N_ROWS, HEAD_DIM), lambda s: (seq(s), 0, 0, 0)),
        pl.BlockSpec((None, SUBLANES, LANES), lambda s: (seq(s), 0, 0)),
        pl.BlockSpec((None, POOL_CARRY, POOL_WIDTH), lambda s: (seq(s), 0, 0)),
    )
    consts = (gmix, wtok, wtr, bslab, ghead_rep, wpool, pscale, wout)
    in_specs = [pair, nxt] + [_const_spec(a.shape) for a in consts]
    weights = sum(a.size * a.dtype.itemsize for a in (wtok, wtr, wpool, wout))
    est = weights + 12 * tm * D * 4 + 12 * tm * SAMPLE_COLS * 4
    scratch = [pltpu.VMEM((tm, TOK_COLS), F32), pltpu.VMEM((TR_ROWS, tm), F32)] * 2
    outs = pl.pallas_call(
        functools.partial(_prompt_mixer_kernel, tm=tm, nt=nt),
        grid=(ntiles // 2,), in_specs=in_specs, out_specs=out_specs, out_shape=out_shapes,
        scratch_shapes=scratch,
        compiler_params=pltpu.CompilerParams(
            dimension_semantics=("arbitrary",),
            vmem_limit_bytes=_vmem_limit(est)),
        name="prompt_mixer",
    )(xt, xt, *consts)
    return (outs[0].reshape(B, T, D),) + tuple(outs[1:])


def _ffn_kernel(*refs, has_mix, final):
    refs = list(refs)
    x_ref = refs.pop(0)
    x = x_ref[...]
    if has_mix:
        mix_ref, wout_ref = refs.pop(0), refs.pop(0)
        x = x + _dot(mix_ref[...].astype(BF16), wout_ref[...])
    gffn_ref, wg_ref, wu_ref, wd_ref = refs[:4]
    refs = refs[4:]
    h2 = _rms(x, gffn_ref[...]).astype(BF16)
    a = _dot(h2, wg_ref[...])
    bu = _dot(h2, wu_ref[...])
    act = (a * jax.nn.sigmoid(a) * bu).astype(BF16)
    y = x + _dot(act, wd_ref[...])
    if final:
        gfin_ref, out_ref = refs
        out_ref[...] = _rms(y, gfin_ref[...])
    else:
        (out_ref,) = refs
        out_ref[...] = y


def _ffn(x, gffn, wg, wu, wd, *, tm, mix=None, wout=None, gfin=None):
    M, D = x.shape
    dff = wg.shape[1]
    tile = pl.BlockSpec((tm, D), lambda i: (i, 0))
    args, in_specs = [x], [tile]
    if mix is not None:
        args += [mix, wout]
        in_specs += [tile, _const_spec(wout.shape)]
    args += [gffn, wg, wu, wd]
    in_specs += [_const_spec(a.shape) for a in (gffn, wg, wu, wd)]
    if gfin is not None:
        args.append(gfin)
        in_specs.append(_const_spec(gfin.shape))
    weights = sum(a.size * a.dtype.itemsize for a in args[1:] if a.dtype == BF16)
    est = weights + 6 * tm * D * 4 + 4 * tm * dff * 4
    return pl.pallas_call(
        functools.partial(_ffn_kernel, has_mix=mix is not None, final=gfin is not None),
        grid=(M // tm,), in_specs=in_specs, out_specs=tile,
        out_shape=jax.ShapeDtypeStruct((M, D), F32),
        compiler_params=pltpu.CompilerParams(
            dimension_semantics=("arbitrary",), vmem_limit_bytes=_vmem_limit(est)),
        name="ffn_final" if gfin is not None else "ffn",
    )(*args)


def _sample_proj_kernel(x_ref, gmix_ref, wtok_ref, wtr_ref, bslab_ref, proj_ref):
    hn = _rms(x_ref[...], gmix_ref[...]).astype(BF16)
    proj_ref[:, :TOK_COLS] = _dot(hn, wtok_ref[...])
    proj_ref[:, GI0:GI0 + 2 * GATE_SLAB] += bslab_ref[...]
    proj_ref[:, TOK_COLS:] = _dot_nt(hn, wtr_ref[...])


def _sample_proj(x, gmix, wtok, wtr, bslab):
    M = x.shape[0]
    est = 2 * (wtok.size + wtr.size) * 2 + 8 * M * SAMPLE_COLS * 4
    return pl.pallas_call(
        _sample_proj_kernel,
        out_shape=jax.ShapeDtypeStruct((M, SAMPLE_COLS), F32),
        compiler_params=pltpu.CompilerParams(vmem_limit_bytes=_vmem_limit(est)),
        name="sample_proj",
    )(x, gmix, wtok, wtr, bslab)


def _head_out(h, g_head, o):
    hn = h * lax.rsqrt(jnp.mean(h * h, axis=-1, keepdims=True) + EPS)
    return hn * g_head * jax.nn.sigmoid(o)


def _sample_step_kernel(proj_ref, ghead_ref, wpool_ref, pscale_ref,
                        C_ref, n_ref, m_ref, buf_ref,
                        mix_ref, Co_ref, no_ref, mo_ref, bufo_ref,
                        g_s, wvt_s, kb_s, qb_s, cq_s, *, bb):
    i = pl.program_id(0)
    nb = proj_ref.shape[0]
    scale = HEAD_DIM ** -0.5
    head = lambda c0, h: proj_ref[:, c0 + h * HEAD_DIM:c0 + (h + 1) * HEAD_DIM]

    def gate_terms(h):
        i_c = _lane_col(proj_ref[:, GI0:GI0 + GATE_SLAB], h)
        lf = _lane_col(_log_sigmoid(proj_ref[:, GF0:GF0 + GATE_SLAB]), h)
        m0 = _lane_col(m_ref[...], h)
        inter = lf + m0
        m_t = jnp.maximum(inter, i_c)
        return jnp.exp(i_c - m_t), jnp.exp(inter - m_t), m_t

    @pl.when(i == 0)
    def _():
        for h in range(N_HEADS):
            w, g, _ = gate_terms(h)
            g_s[h] = jnp.broadcast_to(g, (nb, LANES))
            wvt_s[h] = (w * head(SV0, h)).T
            kb_s[h] = (head(K0, h) * scale).astype(BF16)
            qb_s[h] = head(Q0, h).astype(BF16)
            cq_s[h] = jnp.zeros((HEAD_DIM, nb), F32)

    lane = lax.broadcasted_iota(jnp.int32, (HEAD_DIM, nb), 1)

    def pair_body(j, carry):
        b = i * bb + j
        sel = lane == b
        for h in range(N_HEADS):
            Cb = C_ref[j, h]
            r = _dot_nt(Cb.astype(BF16), qb_s[h])
            cq_s[h] = jnp.where(sel, r, cq_s[h])
            lhs = jnp.where(sel, wvt_s[h], 0.0).astype(BF16)
            outer = _dot(lhs, kb_s[h])
            Co_ref[j, h] = g_s[h, pl.ds(b, 1), :] * Cb + outer
        return carry

    lax.fori_loop(0, bb, pair_body, 0)

    @pl.when(i == pl.num_programs(0) - 1)
    def _():
        heads, n_heads = [], []
        lane_m = lax.broadcasted_iota(jnp.int32, (nb, LANES), 1)
        m_out = jnp.zeros((nb, LANES), F32)
        for h in range(N_HEADS):
            cols = slice(h * HEAD_DIM, (h + 1) * HEAD_DIM)
            w, g, m_t = gate_terms(h)
            qf = head(Q0, h)
            kf = head(K0, h) * scale
            vf = head(SV0, h)
            n0 = n_ref[:, cols]
            qk = jnp.sum(qf.astype(BF16).astype(F32) * kf.astype(BF16).astype(F32),
                         axis=-1, keepdims=True)
            s = qk * w
            num = s * vf + g * cq_s[h].T
            den = s + g * jnp.sum(n0 * qf, axis=-1, keepdims=True)
            hc = num * (1.0 / jnp.maximum(jnp.abs(den), jnp.exp(-m_t)))
            heads.append(_head_out(hc, ghead_ref[:, cols], head(SO0, h)))
            n_heads.append(g * n0 + w * kf)
            m_out = jnp.where(lane_m == h, m_t, m_out)
        no_ref[...] = jnp.concatenate(n_heads, axis=-1)
        mo_ref[...] = m_out

        u = proj_ref[:, U0:U0 + POOL_WIDTH]
        pooled = []
        for gi, win in enumerate(POOL_WINDOWS):
            cols = slice(gi * POOL_GROUP, (gi + 1) * POOL_GROUP)
            a = u[:, cols]
            for jrow in range(POOL_BUF - (win - 1), POOL_BUF):
                a = a + buf_ref[jrow, :, cols]
            p = a / float(win) - u[:, cols]
            pooled.append(_dot(p.astype(BF16), wpool_ref[gi]) * pscale_ref[:, cols])
        mix_ref[...] = jnp.concatenate(heads + pooled, axis=-1)
        for jrow in range(POOL_BUF - 1):
            bufo_ref[jrow] = buf_ref[jrow + 1]
        bufo_ref[POOL_BUF - 1] = u


def _sample_step(proj, ghead, wpool, pscale, C, n, m, buf_t, *, bb):
    nb = proj.shape[0]
    D = MLSTM_WIDTH + POOL_WIDTH
    cblk = pl.BlockSpec((bb, N_HEADS, HEAD_DIM, HEAD_DIM), lambda i: (i, 0, 0, 0))
    small = (proj, ghead, wpool, pscale)
    in_specs = [_const_spec(a.shape) for a in small] + [
        cblk, _const_spec(n.shape), _const_spec(m.shape), _const_spec(buf_t.shape)]
    out_shapes = (
        jax.ShapeDtypeStruct((nb, D), F32),
        jax.ShapeDtypeStruct(C.shape, F32),
        jax.ShapeDtypeStruct(n.shape, F32),
        jax.ShapeDtypeStruct((nb, LANES), F32),
        jax.ShapeDtypeStruct(buf_t.shape, F32),
    )
    full = lambda s: pl.BlockSpec(s, lambda i: (0,) * len(s))
    out_specs = (full((nb, D)), cblk, full(n.shape), full((nb, LANES)), full(buf_t.shape))
    scratch = [
        pltpu.VMEM((N_HEADS, nb, LANES), F32),
        pltpu.VMEM((N_HEADS, HEAD_DIM, nb), F32),
        pltpu.VMEM((N_HEADS, nb, HEAD_DIM), BF16),
        pltpu.VMEM((N_HEADS, nb, HEAD_DIM), BF16),
        pltpu.VMEM((N_HEADS, HEAD_DIM, nb), F32),
    ]
    est = (4 * bb * N_HEADS * HEAD_DIM * HEAD_DIM * 4 + 4 * buf_t.size * 4
           + 8 * proj.size * 4)
    return pl.pallas_call(
        functools.partial(_sample_step_kernel, bb=bb),
        grid=(nb // bb,), in_specs=in_specs, out_specs=out_specs, out_shape=out_shapes,
        scratch_shapes=scratch,
        compiler_params=pltpu.CompilerParams(
            dimension_semantics=("arbitrary",), vmem_limit_bytes=_vmem_limit(est)),
        name="sample_step",
    )(proj, ghead, wpool, pscale, C, n, m, buf_t)


def _pack_w_in(w_in, b_gate):
    depth, d, _ = w_in.shape
    w = MLSTM_WIDTH
    g0 = 4 * w
    pad = jnp.zeros((depth, d, GATE_SLAB - N_HEADS), w_in.dtype)
    wtok = jnp.concatenate(
        [w_in[:, :, 0:2 * w], w_in[:, :, g0 + 2 * N_HEADS:],
         w_in[:, :, g0:g0 + N_HEADS], pad, w_in[:, :, g0 + N_HEADS:g0 + 2 * N_HEADS], pad],
        axis=-1).astype(BF16)
    wtr = jnp.swapaxes(w_in[:, :, 2 * w:4 * w], 1, 2).astype(BF16)
    bpad = jnp.zeros((depth, GATE_SLAB - N_HEADS), b_gate.dtype)
    bslab = jnp.concatenate([b_gate[:, :N_HEADS], bpad, b_gate[:, N_HEADS:], bpad], axis=-1)
    return wtok, wtr, bslab[:, None, :]


def kernel(x_prompt, x_sample, state_mlstm_C, state_mlstm_n, state_mlstm_m, state_pool_buf,
           g_mix, w_in, b_gate, g_head, w_pool, pool_scale, w_out, g_ffn, w_gate, w_up,
           w_down, g_final):
    depth = w_in.shape[0]
    B, T, D = x_prompt.shape
    nb = x_sample.shape[0]
    tm_mixer, tm_ffn, bb = 256, 256, 8

    wtok, wtr, bslab = _pack_w_in(w_in, b_gate)
    wpool = w_pool.astype(BF16)
    wout = w_out.astype(BF16)
    wg, wu, wd = w_gate.astype(BF16), w_up.astype(BF16), w_down.astype(BF16)
    row = lambda a: a[:, None, :]
    gmix, ghead, pscale, gffn = row(g_mix), row(g_head), row(pool_scale), row(g_ffn)
    ghead_rep = jnp.broadcast_to(g_head[:, :, None], g_head.shape + (CHUNK,))
    gfin = g_final[None, :]

    xp = x_prompt
    Cp, npr, mp, bp = [], [], [], []
    for l in range(depth):
        xm, C1, n1, m1, pb1 = _prompt_mixer(xp, gmix[l], wtok[l], wtr[l], bslab[l], ghead_rep[l],
                                            wpool[l], pscale[l], wout[l], tm=tm_mixer)
        last = l == depth - 1
        xp = _ffn(xm.reshape(B * T, D), gffn[l], wg[l], wu[l], wd[l], tm=tm_ffn,
                  gfin=gfin if last else None).reshape(B, T, D)
        Cp.append(C1)
        npr.append(n1[:, :, 0, :])
        mp.append(m1[:, :N_HEADS, 0])
        bp.append(pb1[:, POOL_CARRY - POOL_BUF:, :])
    y_prompt = xp

    xs = x_sample.reshape(nb, D)
    Cs, ns, ms, bs = [], [], [], []
    for l in range(depth):
        proj = _sample_proj(xs, gmix[l], wtok[l], wtr[l], bslab[l])
        m_pad = jnp.pad(state_mlstm_m[l], ((0, 0), (0, LANES - N_HEADS)))
        buf_t = jnp.transpose(state_pool_buf[l], (1, 0, 2))
        mix, C1, n1, m1, buf1 = _sample_step(
            proj, ghead[l], wpool[l], pscale[l], state_mlstm_C[l],
            state_mlstm_n[l].reshape(nb, MLSTM_WIDTH), m_pad, buf_t, bb=bb)
        last = l == depth - 1
        xs = _ffn(xs, gffn[l], wg[l], wu[l], wd[l], tm=nb, mix=mix, wout=wout[l],
                  gfin=gfin if last else None)
        Cs.append(C1)
        ns.append(n1.reshape(nb, N_HEADS, HEAD_DIM))
        ms.append(m1[:, :N_HEADS])
        bs.append(jnp.transpose(buf1, (1, 0, 2)))
    y_sample = xs.reshape(nb, 1, D)

    st = lambda xs_: jnp.stack(xs_, 0)
    return (y_prompt, y_sample, st(Cp), st(npr), st(mp), st(bp),
            st(Cs), st(ns), st(ms), st(bs))
```

```python
import functools

import jax
import jax.numpy as jnp
from jax import lax
from jax.experimental import pallas as pl
from jax.experimental.pallas import tpu as pltpu

F32 = jnp.float32
BF16 = jnp.bfloat16

EPS = 1e-6
N_HEADS = 4
HEAD_DIM = 128
MLSTM_WIDTH = N_HEADS * HEAD_DIM
POOL_WINDOWS = (2, 4, 8, 16)
POOL_GROUP = 128
POOL_WIDTH = POOL_GROUP * len(POOL_WINDOWS)
POOL_BUF = max(POOL_WINDOWS) - 1
POOL_CARRY = POOL_BUF + 1
CHUNK = 256
LANES = 128
SUBLANES = 8
N_ROWS = 2 * SUBLANES
GATE_SLAB = LANES

Q0, K0, U0 = 0, MLSTM_WIDTH, 2 * MLSTM_WIDTH
GI0 = U0 + POOL_WIDTH
GF0 = GI0 + GATE_SLAB
TOK_COLS = GF0 + GATE_SLAB
VT0, OT0 = 0, MLSTM_WIDTH
TR_ROWS = 2 * MLSTM_WIDTH
SV0 = TOK_COLS
SO0 = TOK_COLS + MLSTM_WIDTH
SAMPLE_COLS = TOK_COLS + TR_ROWS

V7X_VMEM_BYTES = 64 * 1024 * 1024

_NT = (((1,), (1,)), ((), ()))
_TN = (((0,), (0,)), ((), ()))


def _dot(a, b):
    return jnp.dot(a, b, preferred_element_type=F32)


def _dot_nt(a, b):
    return lax.dot_general(a, b, _NT, preferred_element_type=F32)


def _dot_tn(a, b):
    return lax.dot_general(a, b, _TN, preferred_element_type=F32)


def _rms(x, g):
    return x * lax.rsqrt(jnp.mean(x * x, axis=-1, keepdims=True) + EPS) * g


def _log_sigmoid(x):
    return jnp.minimum(x, 0.0) - jnp.log1p(jnp.exp(-jnp.abs(x)))


def _lane_col(slab, j):
    lane = lax.broadcasted_iota(jnp.int32, slab.shape, 1)
    return jnp.sum(jnp.where(lane == j, slab, 0.0), axis=-1, keepdims=True)


def _split3(x):
    hi = x.astype(BF16)
    r1 = x - hi.astype(F32)
    mid = r1.astype(BF16)
    lo = (r1 - mid.astype(F32)).astype(BF16)
    return hi, mid, lo


def _cumsum_rows(x):
    r = lax.broadcasted_iota(jnp.int32, (CHUNK, CHUNK), 0)
    c = lax.broadcasted_iota(jnp.int32, (CHUNK, CHUNK), 1)
    tril = (r >= c).astype(BF16)
    hi, mid, lo = _split3(x)
    return _dot(tril, hi) + _dot(tril, mid) + _dot(tril, lo)


def _mlstm_chunk_t(q, k, vT, a_col, a_row, b_row, C, n_rows, m, mask_st):
    qb, kb = q.astype(BF16), k.astype(BF16)
    by_q = _dot_nt(jnp.concatenate([kb, C.astype(BF16), n_rows.astype(BF16)], axis=0), qb)
    sT = by_q[:CHUNK]
    cqT = by_q[CHUNK:CHUNK + HEAD_DIM]
    qn = by_q[CHUNK + HEAD_DIM:CHUNK + HEAD_DIM + 1]
    aT = jnp.where(mask_st, a_col, -jnp.inf)
    M = jnp.maximum(jnp.max(aT, axis=0, keepdims=True), m)
    swT = sT * jnp.exp(aT - M)
    g = jnp.exp(m - M)
    den = jnp.sum(swT, axis=0, keepdims=True) + g * qn
    inv = 1.0 / jnp.maximum(jnp.abs(den), jnp.exp(-(b_row + M)))
    numT = _dot(vT.astype(BF16), swT.astype(BF16)) + g * cqT
    hT = numT * inv
    M_last = M[:, CHUNK - 1:CHUNK]
    m_new = b_row[:, CHUNK - 1:CHUNK] + M_last
    g_end = jnp.exp(m - M_last)
    w_end = jnp.exp(a_row - M_last)
    w_rows = jnp.broadcast_to(w_end, (N_ROWS, CHUNK))
    by_k = _dot(jnp.concatenate([(vT * w_end).astype(BF16), w_rows.astype(BF16)], axis=0), kb)
    C_new = g_end * C + by_k[:HEAD_DIM]
    n_new = g_end * n_rows + by_k[HEAD_DIM:]
    return hT, C_new, n_new, m_new


def _in_proj_items(x, gmix_ref, wqk_ref, wu_ref, wgate_ref, wtr_ref, bslab_ref, tok_ref, tr_ref):
    hn = _rms(x, gmix_ref[...]).astype(BF16)
    yield
    tok_ref[:, GI0:GI0 + 2 * GATE_SLAB] = _dot(hn, wgate_ref[...]) + bslab_ref[...]
    yield
    tok_ref[:, Q0:Q0 + MLSTM_WIDTH] = _dot(hn, wqk_ref[:, :MLSTM_WIDTH])
    yield
    tok_ref[:, K0:K0 + MLSTM_WIDTH] = _dot(hn, wqk_ref[:, MLSTM_WIDTH:]) * (HEAD_DIM ** -0.5)
    yield
    tok_ref[:, U0:U0 + POOL_WIDTH] = _dot(hn, wu_ref[...])
    yield
    for r0 in (VT0, OT0):
        tr_ref[r0:r0 + MLSTM_WIDTH, :] = _dot_nt(wtr_ref[r0:r0 + MLSTM_WIDTH, :], hn)
        yield


def _mix_items(x_ref, tok_ref, tr_ref, t, ghead_ref, wpool_ref, pscale_ref, wout_ref,
               C_ref, n_ref, m_ref, pb_ref, out_ref, tm):
    r = lax.broadcasted_iota(jnp.int32, (CHUNK, CHUNK), 0)
    c = lax.broadcasted_iota(jnp.int32, (CHUNK, CHUNK), 1)
    mask_st = r <= c

    hm_cols = []
    for ci in range(tm // CHUNK):
        rows = slice(ci * CHUNK, (ci + 1) * CHUNK)
        cs = _cumsum_rows(_log_sigmoid(tok_ref[rows, GF0:GF0 + GATE_SLAB]))
        a_slab = tok_ref[rows, GI0:GI0 + GATE_SLAB] - cs
        cs_t = cs.T
        a_t = a_slab.T
        yield
        heads = []
        for h in range(N_HEADS):
            cols = slice(h * HEAD_DIM, (h + 1) * HEAD_DIM)
            hT, C_new, n_new, m_new = _mlstm_chunk_t(
                tok_ref[rows, Q0 + h * HEAD_DIM:Q0 + (h + 1) * HEAD_DIM],
                tok_ref[rows, K0 + h * HEAD_DIM:K0 + (h + 1) * HEAD_DIM],
                tr_ref[VT0 + h * HEAD_DIM:VT0 + (h + 1) * HEAD_DIM, rows],
                a_slab[:, h:h + 1], a_t[h:h + 1, :], cs_t[h:h + 1, :],
                C_ref[h], n_ref[h], m_ref[h:h + 1, 0:1], mask_st)
            C_ref[h] = C_new
            n_ref[h] = n_new
            m_ref[h:h + 1, :] = jnp.broadcast_to(m_new, (1, LANES))
            hnT = hT * lax.rsqrt(jnp.mean(hT * hT, axis=0, keepdims=True) + EPS)
            oT = tr_ref[OT0 + h * HEAD_DIM:OT0 + (h + 1) * HEAD_DIM, rows]
            heads.append((hnT * ghead_ref[cols, :] * jax.nn.sigmoid(oT)).astype(BF16))
            yield
        hm_cols.append(jnp.concatenate(heads, axis=0))
    hmT = jnp.concatenate(hm_cols, axis=1)

    u = tok_ref[:, U0:U0 + POOL_WIDTH]
    ext = jnp.concatenate([pb_ref[...], u], axis=0)
    pb_ref[...] = u[tm - POOL_CARRY:, :]
    pos = t * tm + lax.broadcasted_iota(jnp.int32, (tm, POOL_GROUP), 0)
    pooled = []
    for gi, win in enumerate(POOL_WINDOWS):
        cols = slice(gi * POOL_GROUP, (gi + 1) * POOL_GROUP)
        a = ext[:, cols]
        step = 1
        while step < win:
            a = a[step:, :] + a[:a.shape[0] - step, :]
            step *= 2
        a = a[a.shape[0] - tm:, :]
        cnt = jnp.minimum(win, pos + 1).astype(F32)
        p = a / cnt - u[:, cols]
        pooled.append((_dot(p.astype(BF16), wpool_ref[gi]) * pscale_ref[:, cols]).astype(BF16))
    yield
    pm = jnp.concatenate(pooled, axis=-1)
    out_ref[...] = (x_ref[...] + _dot_tn(hmT, wout_ref[:MLSTM_WIDTH, :])
                    + _dot(pm, wout_ref[MLSTM_WIDTH:, :]))
    yield


def _interleave(*streams):
    streams = list(streams)
    while streams:
        for s in list(streams):
            try:
                next(s)
            except StopIteration:
                streams.remove(s)


def _prompt_mixer_kernel(xcur_ref, xnext_ref, gmix_ref, wqk_ref, wu_ref, wgate_ref, wtr_ref,
                         bslab_ref, ghead_ref, wpool_ref, pscale_ref, wout_ref,
                         xo_ref, C_ref, n_ref, m_ref, pb_ref,
                         tok_a, tr_a, tok_b, tr_b, *, tm, nt):
    s = pl.program_id(0)
    t0 = lax.rem(2 * s, nt)
    proj_args = (gmix_ref, wqk_ref, wu_ref, wgate_ref, wtr_ref, bslab_ref)
    mix_args = (ghead_ref, wpool_ref, pscale_ref, wout_ref, C_ref, n_ref, m_ref, pb_ref)

    @pl.when(s == 0)
    def _():
        _interleave(_in_proj_items(xcur_ref[0], *proj_args, tok_a, tr_a))

    @pl.when(t0 == 0)
    def _():
        C_ref[...] = jnp.zeros_like(C_ref)
        n_ref[...] = jnp.zeros_like(n_ref)
        m_ref[...] = jnp.zeros_like(m_ref)
        pb_ref[...] = jnp.zeros_like(pb_ref)

    _interleave(
        _mix_items(xcur_ref.at[0], tok_a, tr_a, t0, *mix_args, xo_ref.at[0], tm),
        _in_proj_items(xcur_ref[1], *proj_args, tok_b, tr_b))
    _interleave(
        _mix_items(xcur_ref.at[1], tok_b, tr_b, t0 + 1, *mix_args, xo_ref.at[1], tm),
        _in_proj_items(xnext_ref[0], *proj_args, tok_a, tr_a))


def _const_spec(shape):
    nd = len(shape)
    return pl.BlockSpec(shape, lambda *_: (0,) * nd, pipeline_mode=pl.Buffered(1))


def _layer_spec(a, l):
    nd = a.ndim - 1
    return pl.BlockSpec((None,) + a.shape[1:], lambda *_: (l,) + (0,) * nd,
                        pipeline_mode=pl.Buffered(1))


def _layer_bytes(*arrays):
    return sum(a.size // a.shape[0] * a.dtype.itemsize for a in arrays)


def _vmem_limit(nbytes):
    return int(min(V7X_VMEM_BYTES - (4 << 20), max(nbytes, 16 << 20)))


def _prompt_mixer(x, l, gmix, wqk, wu, wgate, wtr, bslab, ghead_rep, wpool, pscale, wout, *, tm):
    B, T, D = x.shape
    nt = T // tm
    assert nt % 2 == 0, "a grid step covers two tiles of one sequence"
    ntiles = B * nt
    xt = x.reshape(ntiles, tm, D)
    pair = pl.BlockSpec((2, tm, D), lambda s: (s, 0, 0))
    nxt = pl.BlockSpec((1, tm, D), lambda s: (jnp.minimum(2 * s + 2, ntiles - 1), 0, 0))
    seq = lambda s: (2 * s) // nt
    out_shapes = (
        jax.ShapeDtypeStruct((ntiles, tm, D), F32),
        jax.ShapeDtypeStruct((B, N_HEADS, HEAD_DIM, HEAD_DIM), F32),
        jax.ShapeDtypeStruct((B, N_HEADS, N_ROWS, HEAD_DIM), F32),
        jax.ShapeDtypeStruct((B, SUBLANES, LANES), F32),
        jax.ShapeDtypeStruct((B, POOL_CARRY, POOL_WIDTH), F32),
    )
    out_specs = (
        pair,
        pl.BlockSpec((None, N_HEADS, HEAD_DIM, HEAD_DIM), lambda s: (seq(s), 0, 0, 0)),
        pl.BlockSpec((None, N_HEADS, N_ROWS, HEAD_DIM), lambda s: (seq(s), 0, 0, 0)),
        pl.BlockSpec((None, SUBLANES, LANES), lambda s: (seq(s), 0, 0)),
        pl.BlockSpec((None, POOL_CARRY, POOL_WIDTH), lambda s: (seq(s), 0, 0)),
    )
    consts = (gmix, wqk, wu, wgate, wtr, bslab, ghead_rep, wpool, pscale, wout)
    in_specs = [pair, nxt] + [_layer_spec(a, l) for a in consts]
    est = _layer_bytes(*consts) + 12 * tm * D * 4 + 12 * tm * SAMPLE_COLS * 4
    scratch = [pltpu.VMEM((tm, TOK_COLS), F32), pltpu.VMEM((TR_ROWS, tm), F32)] * 2
    outs = pl.pallas_call(
        functools.partial(_prompt_mixer_kernel, tm=tm, nt=nt),
        grid=(ntiles // 2,), in_specs=in_specs, out_specs=out_specs, out_shape=out_shapes,
        scratch_shapes=scratch,
        compiler_params=pltpu.CompilerParams(
            dimension_semantics=("arbitrary",),
            vmem_limit_bytes=_vmem_limit(est)),
        name="prompt_mixer",
    )(xt, xt, *consts)
    return (outs[0].reshape(B, T, D),) + tuple(outs[1:])


def _ffn_kernel(*refs, has_mix, final):
    refs = list(refs)
    x_ref = refs.pop(0)
    x = x_ref[...]
    if has_mix:
        mix_ref, wout_ref = refs.pop(0), refs.pop(0)
        x = x + _dot(mix_ref[...].astype(BF16), wout_ref[...])
    gffn_ref, wg_ref, wu_ref, wd_ref = refs[:4]
    refs = refs[4:]
    h2 = _rms(x, gffn_ref[...]).astype(BF16)
    a = _dot(h2, wg_ref[...])
    bu = _dot(h2, wu_ref[...])
    act = (a * jax.nn.sigmoid(a) * bu).astype(BF16)
    y = x + _dot(act, wd_ref[...])
    if final:
        gfin_ref, out_ref = refs
        out_ref[...] = _rms(y, gfin_ref[...])
    else:
        (out_ref,) = refs
        out_ref[...] = y


def _ffn(x, l, gffn, wg, wu, wd, *, tm, mix=None, wout=None, gfin=None):
    M, D = x.shape
    dff = wg.shape[-1]
    tile = pl.BlockSpec((tm, D), lambda i: (i, 0))
    args, in_specs = [x], [tile]
    if mix is not None:
        args += [mix, wout]
        in_specs += [tile, _layer_spec(wout, l)]
    args += [gffn, wg, wu, wd]
    in_specs += [_layer_spec(a, l) for a in (gffn, wg, wu, wd)]
    if gfin is not None:
        args.append(gfin)
        in_specs.append(_const_spec(gfin.shape))
    weights = _layer_bytes(wg, wu, wd) + (_layer_bytes(wout) if mix is not None else 0)
    est = weights + 6 * tm * D * 4 + 4 * tm * dff * 4
    return pl.pallas_call(
        functools.partial(_ffn_kernel, has_mix=mix is not None, final=gfin is not None),
        grid=(M // tm,), in_specs=in_specs, out_specs=tile,
        out_shape=jax.ShapeDtypeStruct((M, D), F32),
        compiler_params=pltpu.CompilerParams(
            dimension_semantics=("arbitrary",), vmem_limit_bytes=_vmem_limit(est)),
        name="ffn_final" if gfin is not None else "ffn",
    )(*args)


def _sample_proj_kernel(x_ref, gmix_ref, wqk_ref, wu_ref, wgate_ref, wtr_ref, bslab_ref, proj_ref):
    hn = _rms(x_ref[...], gmix_ref[...]).astype(BF16)
    proj_ref[:, Q0:U0] = _dot(hn, wqk_ref[...])
    proj_ref[:, U0:GI0] = _dot(hn, wu_ref[...])
    proj_ref[:, GI0:TOK_COLS] = _dot(hn, wgate_ref[...]) + bslab_ref[...]
    proj_ref[:, TOK_COLS:] = _dot_nt(hn, wtr_ref[...])


def _sample_proj(x, l, gmix, wqk, wu, wgate, wtr, bslab):
    M = x.shape[0]
    consts = (gmix, wqk, wu, wgate, wtr, bslab)
    est = 2 * _layer_bytes(*consts) + 8 * M * SAMPLE_COLS * 4
    return pl.pallas_call(
        _sample_proj_kernel,
        grid=(1,),
        in_specs=[_const_spec(x.shape)] + [_layer_spec(a, l) for a in consts],
        out_specs=pl.BlockSpec((M, SAMPLE_COLS), lambda i: (0, 0)),
        out_shape=jax.ShapeDtypeStruct((M, SAMPLE_COLS), F32),
        compiler_params=pltpu.CompilerParams(
            dimension_semantics=("arbitrary",), vmem_limit_bytes=_vmem_limit(est)),
        name="sample_proj",
    )(x, *consts)


def _head_out(h, g_head, o):
    hn = h * lax.rsqrt(jnp.mean(h * h, axis=-1, keepdims=True) + EPS)
    return hn * g_head * jax.nn.sigmoid(o)


def _sample_step_kernel(*refs, bb, has_acc):
    proj_ref, ghead_ref, wpool_ref, pscale_ref, C_ref, n_ref, m_ref, buf_ref = refs[:8]
    refs = refs[8 + int(has_acc):]
    mix_ref, Co_ref, no_ref, mo_ref, bufo_ref, g_s, wvt_s, kb_s, qb_s, cq_s = refs
    i = pl.program_id(0)
    nb = proj_ref.shape[0]
    scale = HEAD_DIM ** -0.5
    head = lambda c0, h: proj_ref[:, c0 + h * HEAD_DIM:c0 + (h + 1) * HEAD_DIM]

    def gate_terms(h):
        i_c = _lane_col(proj_ref[:, GI0:GI0 + GATE_SLAB], h)
        lf = _lane_col(_log_sigmoid(proj_ref[:, GF0:GF0 + GATE_SLAB]), h)
        m0 = _lane_col(m_ref[...], h)
        inter = lf + m0
        m_t = jnp.maximum(inter, i_c)
        return jnp.exp(i_c - m_t), jnp.exp(inter - m_t), m_t

    @pl.when(i == 0)
    def _():
        for h in range(N_HEADS):
            w, g, _ = gate_terms(h)
            g_s[h] = jnp.broadcast_to(g, (nb, LANES))
            wvt_s[h] = (w * head(SV0, h)).T
            kb_s[h] = (head(K0, h) * scale).astype(BF16)
            qb_s[h] = head(Q0, h).astype(BF16)
            cq_s[h] = jnp.zeros((HEAD_DIM, nb), F32)

    lane = lax.broadcasted_iota(jnp.int32, (HEAD_DIM, nb), 1)

    def pair_body(j, carry):
        b = i * bb + j
        sel = lane == b
        for h in range(N_HEADS):
            Cb = C_ref[j, h]
            r = _dot_nt(Cb.astype(BF16), qb_s[h])
            cq_s[h] = jnp.where(sel, r, cq_s[h])
            lhs = jnp.where(sel, wvt_s[h], 0.0).astype(BF16)
            outer = _dot(lhs, kb_s[h])
            Co_ref[j, h] = g_s[h, pl.ds(b, 1), :] * Cb + outer
        return carry

    lax.fori_loop(0, bb, pair_body, 0)

    @pl.when(i == pl.num_programs(0) - 1)
    def _():
        heads, n_heads = [], []
        lane_m = lax.broadcasted_iota(jnp.int32, (nb, LANES), 1)
        m_out = jnp.zeros((nb, LANES), F32)
        for h in range(N_HEADS):
            cols = slice(h * HEAD_DIM, (h + 1) * HEAD_DIM)
            w, g, m_t = gate_terms(h)
            qf = head(Q0, h)
            kf = head(K0, h) * scale
            vf = head(SV0, h)
            n0 = n_ref[:, cols]
            qk = jnp.sum(qf.astype(BF16).astype(F32) * kf.astype(BF16).astype(F32),
                         axis=-1, keepdims=True)
            s = qk * w
            num = s * vf + g * cq_s[h].T
            den = s + g * jnp.sum(n0 * qf, axis=-1, keepdims=True)
            hc = num * (1.0 / jnp.maximum(jnp.abs(den), jnp.exp(-m_t)))
            heads.append(_head_out(hc, ghead_ref[:, cols], head(SO0, h)))
            n_heads.append(g * n0 + w * kf)
            m_out = jnp.where(lane_m == h, m_t, m_out)
        no_ref[...] = jnp.concatenate(n_heads, axis=-1)
        mo_ref[...] = m_out

        u = proj_ref[:, U0:U0 + POOL_WIDTH]
        pooled = []
        for gi, win in enumerate(POOL_WINDOWS):
            cols = slice(gi * POOL_GROUP, (gi + 1) * POOL_GROUP)
            a = u[:, cols]
            for jrow in range(POOL_BUF - (win - 1), POOL_BUF):
                a = a + buf_ref[jrow, :, cols]
            p = a / float(win) - u[:, cols]
            pooled.append(_dot(p.astype(BF16), wpool_ref[gi]) * pscale_ref[:, cols])
        mix_ref[...] = jnp.concatenate(heads + pooled, axis=-1)
        for jrow in range(POOL_BUF - 1):
            bufo_ref[jrow] = buf_ref[jrow + 1]
        bufo_ref[POOL_BUF - 1] = u


def _sample_step(proj, l, ghead, wpool, pscale, C, n, m, buf_t, C_acc, *, bb):
    nb = proj.shape[0]
    D = MLSTM_WIDTH + POOL_WIDTH
    cblk = pl.BlockSpec((None, bb, N_HEADS, HEAD_DIM, HEAD_DIM), lambda i: (l, i, 0, 0, 0))
    in_specs = [_const_spec(proj.shape)] + [_layer_spec(a, l) for a in (ghead, wpool, pscale)] + [
        cblk, _layer_spec(n, l), _layer_spec(m, l), _layer_spec(buf_t, l)]
    args = [proj, ghead, wpool, pscale, C, n, m, buf_t]
    aliases = {}
    if C_acc is not None:
        in_specs.append(pl.BlockSpec(memory_space=pl.ANY))
        args.append(C_acc)
        aliases = {len(args) - 1: 1}
    out_shapes = (
        jax.ShapeDtypeStruct((nb, D), F32),
        jax.ShapeDtypeStruct(C.shape, F32),
        jax.ShapeDtypeStruct(n.shape[1:], F32),
        jax.ShapeDtypeStruct((nb, LANES), F32),
        jax.ShapeDtypeStruct(buf_t.shape[1:], F32),
    )
    full = lambda s: pl.BlockSpec(s, lambda i: (0,) * len(s))
    out_specs = (full((nb, D)), cblk, full(n.shape[1:]), full((nb, LANES)), full(buf_t.shape[1:]))
    scratch = [
        pltpu.VMEM((N_HEADS, nb, LANES), F32),
        pltpu.VMEM((N_HEADS, HEAD_DIM, nb), F32),
        pltpu.VMEM((N_HEADS, nb, HEAD_DIM), BF16),
        pltpu.VMEM((N_HEADS, nb, HEAD_DIM), BF16),
        pltpu.VMEM((N_HEADS, HEAD_DIM, nb), F32),
    ]
    est = (4 * bb * N_HEADS * HEAD_DIM * HEAD_DIM * 4 + 4 * _layer_bytes(buf_t)
           + 8 * proj.size * 4)
    return pl.pallas_call(
        functools.partial(_sample_step_kernel, bb=bb, has_acc=C_acc is not None),
        grid=(nb // bb,), in_specs=in_specs, out_specs=out_specs, out_shape=out_shapes,
        scratch_shapes=scratch, input_output_aliases=aliases,
        compiler_params=pltpu.CompilerParams(
            dimension_semantics=("arbitrary",), vmem_limit_bytes=_vmem_limit(est)),
        name="sample_step",
    )(*args)


def _pack_w_in(w_in, b_gate):
    depth, d, _ = w_in.shape
    w = MLSTM_WIDTH
    g0 = 4 * w
    wqk = w_in[:, :, 0:2 * w].astype(BF16)
    wu = w_in[:, :, g0 + 2 * N_HEADS:].astype(BF16)
    pad = jnp.zeros((depth, d, GATE_SLAB - N_HEADS), BF16)
    gates = w_in[:, :, g0:g0 + 2 * N_HEADS].astype(BF16)
    wgate = jnp.concatenate([gates[:, :, :N_HEADS], pad, gates[:, :, N_HEADS:], pad], axis=-1)
    wtr = jnp.swapaxes(w_in[:, :, 2 * w:4 * w], 1, 2).astype(BF16)
    bpad = jnp.zeros((depth, GATE_SLAB - N_HEADS), b_gate.dtype)
    bslab = jnp.concatenate([b_gate[:, :N_HEADS], bpad, b_gate[:, N_HEADS:], bpad], axis=-1)
    return wqk, wu, wgate, wtr, bslab[:, None, :]


def kernel(x_prompt, x_sample, state_mlstm_C, state_mlstm_n, state_mlstm_m, state_pool_buf,
           g_mix, w_in, b_gate, g_head, w_pool, pool_scale, w_out, g_ffn, w_gate, w_up,
           w_down, g_final):
    depth = w_in.shape[0]
    B, T, D = x_prompt.shape
    nb = x_sample.shape[0]
    tm_mixer, tm_ffn, bb = 256, 256, 8

    wqk, wu_in, wgate, wtr, bslab = _pack_w_in(w_in, b_gate)
    win = (wqk, wu_in, wgate, wtr, bslab)
    wpool = w_pool.astype(BF16)
    wout = w_out.astype(BF16)
    wg, wu, wd = w_gate.astype(BF16), w_up.astype(BF16), w_down.astype(BF16)
    row = lambda a: a[:, None, :]
    gmix, ghead, pscale, gffn = row(g_mix), row(g_head), row(pool_scale), row(g_ffn)
    ghead_rep = jnp.broadcast_to(g_head[:, :, None], g_head.shape + (CHUNK,))
    gfin = g_final[None, :]

    xp = x_prompt
    Cp, npr, mp, bp = [], [], [], []
    for l in range(depth):
        xm, C1, n1, m1, pb1 = _prompt_mixer(xp, l, gmix, *win, ghead_rep, wpool, pscale, wout,
                                            tm=tm_mixer)
        last = l == depth - 1
        xp = _ffn(xm.reshape(B * T, D), l, gffn, wg, wu, wd, tm=tm_ffn,
                  gfin=gfin if last else None).reshape(B, T, D)
        Cp.append(C1)
        npr.append(n1[:, :, 0, :])
        mp.append(m1[:, :N_HEADS, 0])
        bp.append(pb1[:, POOL_CARRY - POOL_BUF:, :])
    y_prompt = xp

    xs = x_sample.reshape(nb, D)
    ns, ms, bs = [], [], []
    n_in = state_mlstm_n.reshape(depth, nb, MLSTM_WIDTH)
    m_pad = jnp.pad(state_mlstm_m, ((0, 0), (0, 0), (0, LANES - N_HEADS)))
    buf_t = jnp.transpose(state_pool_buf, (0, 2, 1, 3))
    C_sample = None
    for l in range(depth):
        proj = _sample_proj(xs, l, gmix, *win)
        mix, C_sample, n1, m1, buf1 = _sample_step(
            proj, l, ghead, wpool, pscale, state_mlstm_C, n_in, m_pad, buf_t, C_sample, bb=bb)
        last = l == depth - 1
        xs = _ffn(xs, l, gffn, wg, wu, wd, tm=nb, mix=mix, wout=wout,
                  gfin=gfin if last else None)
        ns.append(n1.reshape(nb, N_HEADS, HEAD_DIM))
        ms.append(m1[:, :N_HEADS])
        bs.append(jnp.transpose(buf1, (1, 0, 2)))
    y_sample = xs.reshape(nb, 1, D)

    st = lambda xs_: jnp.stack(xs_, 0)
    return (y_prompt, y_sample, st(Cp), st(npr), st(mp), st(bp),
            C_sample, st(ns), st(ms), st(bs))
```

```python
import functools

import jax
import jax.numpy as jnp
from jax import lax
from jax.experimental import pallas as pl
from jax.experimental.pallas import tpu as pltpu

F32 = jnp.float32
BF16 = jnp.bfloat16

EPS = 1e-6
N_HEADS = 4
HEAD_DIM = 128
MLSTM_WIDTH = N_HEADS * HEAD_DIM
POOL_WINDOWS = (2, 4, 8, 16)
POOL_GROUP = 128
POOL_WIDTH = POOL_GROUP * len(POOL_WINDOWS)
POOL_BUF = max(POOL_WINDOWS) - 1
POOL_CARRY = POOL_BUF + 1
CHUNK = 256
LANES = 128
SUBLANES = 8
N_ROWS = 2 * SUBLANES
GATE_SLAB = LANES

Q0, K0, U0 = 0, MLSTM_WIDTH, 2 * MLSTM_WIDTH
GI0 = U0 + POOL_WIDTH
GF0 = GI0 + GATE_SLAB
TOK_COLS = GF0 + GATE_SLAB
VT0, OT0 = 0, MLSTM_WIDTH
TR_ROWS = 2 * MLSTM_WIDTH
SV0 = TOK_COLS
SO0 = TOK_COLS + MLSTM_WIDTH
SAMPLE_COLS = TOK_COLS + TR_ROWS

V7X_VMEM_BYTES = 64 * 1024 * 1024

_NT = (((1,), (1,)), ((), ()))
_TN = (((0,), (0,)), ((), ()))


def _dot(a, b):
    return jnp.dot(a, b, preferred_element_type=F32)


def _dot_nt(a, b):
    return lax.dot_general(a, b, _NT, preferred_element_type=F32)


def _dot_tn(a, b):
    return lax.dot_general(a, b, _TN, preferred_element_type=F32)


def _rms(x, g):
    return x * lax.rsqrt(jnp.mean(x * x, axis=-1, keepdims=True) + EPS) * g


def _log_sigmoid(x):
    return jnp.minimum(x, 0.0) - jnp.log1p(jnp.exp(-jnp.abs(x)))


def _lane_col(slab, j):
    lane = lax.broadcasted_iota(jnp.int32, slab.shape, 1)
    return jnp.sum(jnp.where(lane == j, slab, 0.0), axis=-1, keepdims=True)


def _split3(x):
    hi = x.astype(BF16)
    r1 = x - hi.astype(F32)
    mid = r1.astype(BF16)
    lo = (r1 - mid.astype(F32)).astype(BF16)
    return hi, mid, lo


def _cumsum_rows(x):
    r = lax.broadcasted_iota(jnp.int32, (CHUNK, CHUNK), 0)
    c = lax.broadcasted_iota(jnp.int32, (CHUNK, CHUNK), 1)
    tril = (r >= c).astype(BF16)
    hi, mid, lo = _split3(x)
    return _dot(tril, hi) + _dot(tril, mid) + _dot(tril, lo)


def _head_stages(h, tok_ref, tr_ref, a_slab, a_t, cs_t, ghead_ref, C_ref, n_ref, m_ref,
                 mask_st, out):
    cols = slice(h * HEAD_DIM, (h + 1) * HEAD_DIM)
    head = lambda c0: slice(c0 + h * HEAD_DIM, c0 + (h + 1) * HEAD_DIM)
    C, n_rows, m = C_ref[h], n_ref[h], m_ref[h:h + 1, 0:1]
    qb = tok_ref[:, head(Q0)].astype(BF16)
    kb = tok_ref[:, head(K0)].astype(BF16)
    by_q = _dot_nt(jnp.concatenate([kb, C.astype(BF16), n_rows.astype(BF16)], axis=0), qb)
    yield
    sT = by_q[:CHUNK]
    cqT = by_q[CHUNK:CHUNK + HEAD_DIM]
    qn = by_q[CHUNK + HEAD_DIM:CHUNK + HEAD_DIM + 1]
    a_row, b_row = a_t[h:h + 1, :], cs_t[h:h + 1, :]
    aT = jnp.where(mask_st, a_slab[:, h:h + 1], -jnp.inf)
    M = jnp.maximum(jnp.max(aT, axis=0, keepdims=True), m)
    swT = sT * jnp.exp(aT - M)
    g = jnp.exp(m - M)
    den = jnp.sum(swT, axis=0, keepdims=True) + g * qn
    inv = 1.0 / jnp.maximum(jnp.abs(den), jnp.exp(-(b_row + M)))
    vT = tr_ref[head(VT0), :]
    hT = (_dot(vT.astype(BF16), swT.astype(BF16)) + g * cqT) * inv
    hnT = hT * lax.rsqrt(jnp.mean(hT * hT, axis=0, keepdims=True) + EPS)
    out[h] = (hnT * ghead_ref[cols, :] * jax.nn.sigmoid(tr_ref[head(OT0), :])).astype(BF16)
    yield
    M_last = M[:, CHUNK - 1:CHUNK]
    g_end = jnp.exp(m - M_last)
    w_end = jnp.exp(a_row - M_last)
    w_rows = jnp.broadcast_to(w_end, (N_ROWS, CHUNK))
    by_k = _dot(jnp.concatenate([(vT * w_end).astype(BF16), w_rows.astype(BF16)], axis=0), kb)
    C_ref[h] = g_end * C + by_k[:HEAD_DIM]
    n_ref[h] = g_end * n_rows + by_k[HEAD_DIM:]
    m_ref[h:h + 1, :] = jnp.broadcast_to(b_row[:, CHUNK - 1:CHUNK] + M_last, (1, LANES))
    yield


def _skewed(streams):
    streams = list(streams)
    live = []
    while streams or live:
        if streams:
            live.append(streams.pop(0))
        for g in list(live):
            try:
                next(g)
            except StopIteration:
                live.remove(g)
        if streams or live:
            yield


def _norm_tile(x_ref, gmix_ref, hn_ref):
    hn_ref[...] = _rms(x_ref[...], gmix_ref[...]).astype(BF16)


def _in_proj_items(hn_ref, wqk_ref, wu_ref, wgate_ref, wtr_ref, bslab_ref, tok_ref, tr_ref,
                   then=None):
    tok_ref[:, GI0:GI0 + 2 * GATE_SLAB] = _dot(hn_ref[...], wgate_ref[...]) + bslab_ref[...]
    yield
    tok_ref[:, Q0:Q0 + MLSTM_WIDTH] = _dot(hn_ref[...], wqk_ref[:, :MLSTM_WIDTH])
    yield
    tok_ref[:, K0:K0 + MLSTM_WIDTH] = (_dot(hn_ref[...], wqk_ref[:, MLSTM_WIDTH:])
                                       * (HEAD_DIM ** -0.5))
    yield
    tok_ref[:, U0:U0 + POOL_WIDTH] = _dot(hn_ref[...], wu_ref[...])
    yield
    for r0 in (VT0, OT0):
        tr_ref[r0:r0 + MLSTM_WIDTH, :] = _dot_nt(wtr_ref[r0:r0 + MLSTM_WIDTH, :], hn_ref[...])
        yield
    if then is not None:
        then()
        yield


def _mix_items(x_ref, tok_ref, tr_ref, t, ghead_ref, wpool_ref, pscale_ref, wout_ref,
               C_ref, n_ref, m_ref, pb_ref, out_ref, tm):
    r = lax.broadcasted_iota(jnp.int32, (CHUNK, CHUNK), 0)
    c = lax.broadcasted_iota(jnp.int32, (CHUNK, CHUNK), 1)
    mask_st = r <= c

    assert tm == CHUNK, "one mLSTM chunk per tile"
    cs = _cumsum_rows(_log_sigmoid(tok_ref[:, GF0:GF0 + GATE_SLAB]))
    a_slab = tok_ref[:, GI0:GI0 + GATE_SLAB] - cs
    cs_t = cs.T
    a_t = a_slab.T
    yield
    heads = [None] * N_HEADS
    yield from _skewed(
        _head_stages(h, tok_ref, tr_ref, a_slab, a_t, cs_t, ghead_ref, C_ref, n_ref, m_ref,
                     mask_st, heads)
        for h in range(N_HEADS))
    yield
    hmT = jnp.concatenate(heads, axis=0)

    u = tok_ref[:, U0:U0 + POOL_WIDTH]
    ext = jnp.concatenate([pb_ref[...], u], axis=0)
    pb_ref[...] = u[tm - POOL_CARRY:, :]
    pos = t * tm + lax.broadcasted_iota(jnp.int32, (tm, POOL_GROUP), 0)
    pooled = []
    for gi, win in enumerate(POOL_WINDOWS):
        cols = slice(gi * POOL_GROUP, (gi + 1) * POOL_GROUP)
        a = ext[:, cols]
        step = 1
        while step < win:
            a = a[step:, :] + a[:a.shape[0] - step, :]
            step *= 2
        a = a[a.shape[0] - tm:, :]
        cnt = jnp.minimum(win, pos + 1).astype(F32)
        p = a / cnt - u[:, cols]
        pooled.append((_dot(p.astype(BF16), wpool_ref[gi]) * pscale_ref[:, cols]).astype(BF16))
    yield
    pm = jnp.concatenate(pooled, axis=-1)
    out_ref[...] = (x_ref[...] + _dot_tn(hmT, wout_ref[:MLSTM_WIDTH, :])
                    + _dot(pm, wout_ref[MLSTM_WIDTH:, :]))
    yield


def _interleave(*streams):
    streams = list(streams)
    while streams:
        for s in list(streams):
            try:
                next(s)
            except StopIteration:
                streams.remove(s)


def _prompt_mixer_kernel(xcur_ref, xnext_ref, gmix_ref, wqk_ref, wu_ref, wgate_ref, wtr_ref,
                         bslab_ref, ghead_ref, wpool_ref, pscale_ref, wout_ref,
                         xo_ref, C_ref, n_ref, m_ref, pb_ref,
                         tok_a, tr_a, hn_a, tok_b, tr_b, hn_b, *, tm, nt):
    s = pl.program_id(0)
    t0 = lax.rem(2 * s, nt)
    proj_args = (wqk_ref, wu_ref, wgate_ref, wtr_ref, bslab_ref)
    mix_args = (ghead_ref, wpool_ref, pscale_ref, wout_ref, C_ref, n_ref, m_ref, pb_ref)

    @pl.when(s == 0)
    def _():
        _norm_tile(xcur_ref.at[0], gmix_ref, hn_a)
        _interleave(_in_proj_items(hn_a, *proj_args, tok_a, tr_a))
        _norm_tile(xcur_ref.at[1], gmix_ref, hn_b)

    @pl.when(t0 == 0)
    def _():
        C_ref[...] = jnp.zeros_like(C_ref)
        n_ref[...] = jnp.zeros_like(n_ref)
        m_ref[...] = jnp.zeros_like(m_ref)
        pb_ref[...] = jnp.zeros_like(pb_ref)

    _interleave(
        _in_proj_items(hn_b, *proj_args, tok_b, tr_b,
                       then=lambda: _norm_tile(xnext_ref.at[0], gmix_ref, hn_a)),
        _mix_items(xcur_ref.at[0], tok_a, tr_a, t0, *mix_args, xo_ref.at[0], tm))
    _interleave(
        _in_proj_items(hn_a, *proj_args, tok_a, tr_a,
                       then=lambda: _norm_tile(xnext_ref.at[1], gmix_ref, hn_b)),
        _mix_items(xcur_ref.at[1], tok_b, tr_b, t0 + 1, *mix_args, xo_ref.at[1], tm))


def _const_spec(shape):
    nd = len(shape)
    return pl.BlockSpec(shape, lambda *_: (0,) * nd, pipeline_mode=pl.Buffered(1))


def _layer_spec(a, l):
    nd = a.ndim - 1
    return pl.BlockSpec((None,) + a.shape[1:], lambda *_: (l,) + (0,) * nd,
                        pipeline_mode=pl.Buffered(1))


def _layer_bytes(*arrays):
    return sum(a.size // a.shape[0] * a.dtype.itemsize for a in arrays)


def _vmem_limit(nbytes):
    return int(min(V7X_VMEM_BYTES - (4 << 20), max(nbytes, 16 << 20)))


def _prompt_mixer(x, l, gmix, wqk, wu, wgate, wtr, bslab, ghead_rep, wpool, pscale, wout, *, tm):
    B, T, D = x.shape
    nt = T // tm
    assert nt % 2 == 0, "a grid step covers two tiles of one sequence"
    ntiles = B * nt
    xt = x.reshape(ntiles, tm, D)
    pair = pl.BlockSpec((2, tm, D), lambda s: (s, 0, 0))
    nxt = pl.BlockSpec((2, tm, D), lambda s: (jnp.minimum(s + 1, ntiles // 2 - 1), 0, 0))
    seq = lambda s: (2 * s) // nt
    out_shapes = (
        jax.ShapeDtypeStruct((ntiles, tm, D), F32),
        jax.ShapeDtypeStruct((B, N_HEADS, HEAD_DIM, HEAD_DIM), F32),
        jax.ShapeDtypeStruct((B, N_HEADS, N_ROWS, HEAD_DIM), F32),
        jax.ShapeDtypeStruct((B, SUBLANES, LANES), F32),
        jax.ShapeDtypeStruct((B, POOL_CARRY, POOL_WIDTH), F32),
    )
    out_specs = (
        pair,
        pl.BlockSpec((None, N_HEADS, HEAD_DIM, HEAD_DIM), lambda s: (seq(s), 0, 0, 0)),
        pl.BlockSpec((None, N_HEADS, N_ROWS, HEAD_DIM), lambda s: (seq(s), 0, 0, 0)),
        pl.BlockSpec((None, SUBLANES, LANES), lambda s: (seq(s), 0, 0)),
        pl.BlockSpec((None, POOL_CARRY, POOL_WIDTH), lambda s: (seq(s), 0, 0)),
    )
    consts = (gmix, wqk, wu, wgate, wtr, bslab, ghead_rep, wpool, pscale, wout)
    in_specs = [pair, nxt] + [_layer_spec(a, l) for a in consts]
    est = _layer_bytes(*consts) + 12 * tm * D * 4 + 12 * tm * SAMPLE_COLS * 4
    scratch = [pltpu.VMEM((tm, TOK_COLS), F32), pltpu.VMEM((TR_ROWS, tm), F32),
               pltpu.VMEM((tm, D), BF16)] * 2
    outs = pl.pallas_call(
        functools.partial(_prompt_mixer_kernel, tm=tm, nt=nt),
        grid=(ntiles // 2,), in_specs=in_specs, out_specs=out_specs, out_shape=out_shapes,
        scratch_shapes=scratch,
        compiler_params=pltpu.CompilerParams(
            dimension_semantics=("arbitrary",),
            vmem_limit_bytes=_vmem_limit(est)),
        name="prompt_mixer",
    )(xt, xt, *consts)
    return (outs[0].reshape(B, T, D),) + tuple(outs[1:])


def _ffn_kernel(*refs, has_mix, final):
    refs = list(refs)
    x_ref = refs.pop(0)
    x = x_ref[...]
    if has_mix:
        mix_ref, wout_ref = refs.pop(0), refs.pop(0)
        x = x + _dot(mix_ref[...].astype(BF16), wout_ref[...])
    gffn_ref, wg_ref, wu_ref, wd_ref = refs[:4]
    refs = refs[4:]
    h2 = _rms(x, gffn_ref[...]).astype(BF16)
    a = _dot(h2, wg_ref[...])
    bu = _dot(h2, wu_ref[...])
    act = (a * jax.nn.sigmoid(a) * bu).astype(BF16)
    y = x + _dot(act, wd_ref[...])
    if final:
        gfin_ref, out_ref = refs
        out_ref[...] = _rms(y, gfin_ref[...])
    else:
        (out_ref,) = refs
        out_ref[...] = y


def _ffn(x, l, gffn, wg, wu, wd, *, tm, mix=None, wout=None, gfin=None):
    M, D = x.shape
    dff = wg.shape[-1]
    tile = pl.BlockSpec((tm, D), lambda i: (i, 0))
    args, in_specs = [x], [tile]
    if mix is not None:
        args += [mix, wout]
        in_specs += [tile, _layer_spec(wout, l)]
    args += [gffn, wg, wu, wd]
    in_specs += [_layer_spec(a, l) for a in (gffn, wg, wu, wd)]
    if gfin is not None:
        args.append(gfin)
        in_specs.append(_const_spec(gfin.shape))
    weights = _layer_bytes(wg, wu, wd) + (_layer_bytes(wout) if mix is not None else 0)
    est = weights + 6 * tm * D * 4 + 4 * tm * dff * 4
    return pl.pallas_call(
        functools.partial(_ffn_kernel, has_mix=mix is not None, final=gfin is not None),
        grid=(M // tm,), in_specs=in_specs, out_specs=tile,
        out_shape=jax.ShapeDtypeStruct((M, D), F32),
        compiler_params=pltpu.CompilerParams(
            dimension_semantics=("arbitrary",), vmem_limit_bytes=_vmem_limit(est)),
        name="ffn_final" if gfin is not None else "ffn",
    )(*args)


def _sample_proj_kernel(x_ref, gmix_ref, wqk_ref, wu_ref, wgate_ref, wtr_ref, bslab_ref, proj_ref):
    hn = _rms(x_ref[...], gmix_ref[...]).astype(BF16)
    proj_ref[:, Q0:U0] = _dot(hn, wqk_ref[...])
    proj_ref[:, U0:GI0] = _dot(hn, wu_ref[...])
    proj_ref[:, GI0:TOK_COLS] = _dot(hn, wgate_ref[...]) + bslab_ref[...]
    proj_ref[:, TOK_COLS:] = _dot_nt(hn, wtr_ref[...])


def _sample_proj(x, l, gmix, wqk, wu, wgate, wtr, bslab):
    M = x.shape[0]
    consts = (gmix, wqk, wu, wgate, wtr, bslab)
    est = 2 * _layer_bytes(*consts) + 8 * M * SAMPLE_COLS * 4
    return pl.pallas_call(
        _sample_proj_kernel,
        grid=(1,),
        in_specs=[_const_spec(x.shape)] + [_layer_spec(a, l) for a in consts],
        out_specs=pl.BlockSpec((M, SAMPLE_COLS), lambda i: (0, 0)),
        out_shape=jax.ShapeDtypeStruct((M, SAMPLE_COLS), F32),
        compiler_params=pltpu.CompilerParams(
            dimension_semantics=("arbitrary",), vmem_limit_bytes=_vmem_limit(est)),
        name="sample_proj",
    )(x, *consts)


def _head_out(h, g_head, o):
    hn = h * lax.rsqrt(jnp.mean(h * h, axis=-1, keepdims=True) + EPS)
    return hn * g_head * jax.nn.sigmoid(o)


def _sample_step_kernel(*refs, bb, has_acc):
    proj_ref, ghead_ref, wpool_ref, pscale_ref, C_ref, n_ref, m_ref, buf_ref = refs[:8]
    refs = refs[8 + int(has_acc):]
    mix_ref, Co_ref, no_ref, mo_ref, bufo_ref, g_s, wvt_s, kb_s, qb_s, cq_s = refs
    i = pl.program_id(0)
    nb = proj_ref.shape[0]
    scale = HEAD_DIM ** -0.5
    head = lambda c0, h: proj_ref[:, c0 + h * HEAD_DIM:c0 + (h + 1) * HEAD_DIM]

    def gate_terms(h):
        i_c = _lane_col(proj_ref[:, GI0:GI0 + GATE_SLAB], h)
        lf = _lane_col(_log_sigmoid(proj_ref[:, GF0:GF0 + GATE_SLAB]), h)
        m0 = _lane_col(m_ref[...], h)
        inter = lf + m0
        m_t = jnp.maximum(inter, i_c)
        return jnp.exp(i_c - m_t), jnp.exp(inter - m_t), m_t

    @pl.when(i == 0)
    def _():
        for h in range(N_HEADS):
            w, g, _ = gate_terms(h)
            g_s[h] = jnp.broadcast_to(g, (nb, LANES))
            wvt_s[h] = (w * head(SV0, h)).T
            kb_s[h] = (head(K0, h) * scale).astype(BF16)
            qb_s[h] = head(Q0, h).astype(BF16)
            cq_s[h] = jnp.zeros((HEAD_DIM, nb), F32)

    lane = lax.broadcasted_iota(jnp.int32, (HEAD_DIM, nb), 1)

    def pair_body(j, carry):
        b = i * bb + j
        sel = lane == b
        for h in range(N_HEADS):
            Cb = C_ref[j, h]
            r = _dot_nt(Cb.astype(BF16), qb_s[h])
            cq_s[h] = jnp.where(sel, r, cq_s[h])
            lhs = jnp.where(sel, wvt_s[h], 0.0).astype(BF16)
            outer = _dot(lhs, kb_s[h])
            Co_ref[j, h] = g_s[h, pl.ds(b, 1), :] * Cb + outer
        return carry

    lax.fori_loop(0, bb, pair_body, 0)

    @pl.when(i == pl.num_programs(0) - 1)
    def _():
        heads, n_heads = [], []
        lane_m = lax.broadcasted_iota(jnp.int32, (nb, LANES), 1)
        m_out = jnp.zeros((nb, LANES), F32)
        for h in range(N_HEADS):
            cols = slice(h * HEAD_DIM, (h + 1) * HEAD_DIM)
            w, g, m_t = gate_terms(h)
            qf = head(Q0, h)
            kf = head(K0, h) * scale
            vf = head(SV0, h)
            n0 = n_ref[:, cols]
            qk = jnp.sum(qf.astype(BF16).astype(F32) * kf.astype(BF16).astype(F32),
                         axis=-1, keepdims=True)
            s = qk * w
            num = s * vf + g * cq_s[h].T
            den = s + g * jnp.sum(n0 * qf, axis=-1, keepdims=True)
            hc = num * (1.0 / jnp.maximum(jnp.abs(den), jnp.exp(-m_t)))
            heads.append(_head_out(hc, ghead_ref[:, cols], head(SO0, h)))
            n_heads.append(g * n0 + w * kf)
            m_out = jnp.where(lane_m == h, m_t, m_out)
        no_ref[...] = jnp.concatenate(n_heads, axis=-1)
        mo_ref[...] = m_out

        u = proj_ref[:, U0:U0 + POOL_WIDTH]
        pooled = []
        for gi, win in enumerate(POOL_WINDOWS):
            cols = slice(gi * POOL_GROUP, (gi + 1) * POOL_GROUP)
            a = u[:, cols]
            for jrow in range(POOL_BUF - (win - 1), POOL_BUF):
                a = a + buf_ref[jrow, :, cols]
            p = a / float(win) - u[:, cols]
            pooled.append(_dot(p.astype(BF16), wpool_ref[gi]) * pscale_ref[:, cols])
        mix_ref[...] = jnp.concatenate(heads + pooled, axis=-1)
        for jrow in range(POOL_BUF - 1):
            bufo_ref[jrow] = buf_ref[jrow + 1]
        bufo_ref[POOL_BUF - 1] = u


def _sample_step(proj, l, ghead, wpool, pscale, C, n, m, buf_t, C_acc, *, bb):
    nb = proj.shape[0]
    D = MLSTM_WIDTH + POOL_WIDTH
    cblk = pl.BlockSpec((None, bb, N_HEADS, HEAD_DIM, HEAD_DIM), lambda i: (l, i, 0, 0, 0))
    in_specs = [_const_spec(proj.shape)] + [_layer_spec(a, l) for a in (ghead, wpool, pscale)] + [
        cblk, _layer_spec(n, l), _layer_spec(m, l), _layer_spec(buf_t, l)]
    args = [proj, ghead, wpool, pscale, C, n, m, buf_t]
    aliases = {}
    if C_acc is not None:
        in_specs.append(pl.BlockSpec(memory_space=pl.ANY))
        args.append(C_acc)
        aliases = {len(args) - 1: 1}
    out_shapes = (
        jax.ShapeDtypeStruct((nb, D), F32),
        jax.ShapeDtypeStruct(C.shape, F32),
        jax.ShapeDtypeStruct(n.shape[1:], F32),
        jax.ShapeDtypeStruct((nb, LANES), F32),
        jax.ShapeDtypeStruct(buf_t.shape[1:], F32),
    )
    full = lambda s: pl.BlockSpec(s, lambda i: (0,) * len(s))
    out_specs = (full((nb, D)), cblk, full(n.shape[1:]), full((nb, LANES)), full(buf_t.shape[1:]))
    scratch = [
        pltpu.VMEM((N_HEADS, nb, LANES), F32),
        pltpu.VMEM((N_HEADS, HEAD_DIM, nb), F32),
        pltpu.VMEM((N_HEADS, nb, HEAD_DIM), BF16),
        pltpu.VMEM((N_HEADS, nb, HEAD_DIM), BF16),
        pltpu.VMEM((N_HEADS, HEAD_DIM, nb), F32),
    ]
    est = (4 * bb * N_HEADS * HEAD_DIM * HEAD_DIM * 4 + 4 * _layer_bytes(buf_t)
           + 8 * proj.size * 4)
    return pl.pallas_call(
        functools.partial(_sample_step_kernel, bb=bb, has_acc=C_acc is not None),
        grid=(nb // bb,), in_specs=in_specs, out_specs=out_specs, out_shape=out_shapes,
        scratch_shapes=scratch, input_output_aliases=aliases,
        compiler_params=pltpu.CompilerParams(
            dimension_semantics=("arbitrary",), vmem_limit_bytes=_vmem_limit(est)),
        name="sample_step",
    )(*args)


def _pack_w_in(w_in, b_gate):
    depth, d, _ = w_in.shape
    w = MLSTM_WIDTH
    g0 = 4 * w
    wqk = w_in[:, :, 0:2 * w].astype(BF16)
    wu = w_in[:, :, g0 + 2 * N_HEADS:].astype(BF16)
    pad = jnp.zeros((depth, d, GATE_SLAB - N_HEADS), BF16)
    gates = w_in[:, :, g0:g0 + 2 * N_HEADS].astype(BF16)
    wgate = jnp.concatenate([gates[:, :, :N_HEADS], pad, gates[:, :, N_HEADS:], pad], axis=-1)
    wtr = jnp.swapaxes(w_in[:, :, 2 * w:4 * w], 1, 2).astype(BF16)
    bpad = jnp.zeros((depth, GATE_SLAB - N_HEADS), b_gate.dtype)
    bslab = jnp.concatenate([b_gate[:, :N_HEADS], bpad, b_gate[:, N_HEADS:], bpad], axis=-1)
    return wqk, wu, wgate, wtr, bslab[:, None, :]


def kernel(x_prompt, x_sample, state_mlstm_C, state_mlstm_n, state_mlstm_m, state_pool_buf,
           g_mix, w_in, b_gate, g_head, w_pool, pool_scale, w_out, g_ffn, w_gate, w_up,
           w_down, g_final):
    depth = w_in.shape[0]
    B, T, D = x_prompt.shape
    nb = x_sample.shape[0]
    tm_mixer, tm_ffn, bb = 256, 512, 8

    wqk, wu_in, wgate, wtr, bslab = _pack_w_in(w_in, b_gate)
    win = (wqk, wu_in, wgate, wtr, bslab)
    wpool = w_pool.astype(BF16)
    wout = w_out.astype(BF16)
    wg, wu, wd = w_gate.astype(BF16), w_up.astype(BF16), w_down.astype(BF16)
    row = lambda a: a[:, None, :]
    gmix, ghead, pscale, gffn = row(g_mix), row(g_head), row(pool_scale), row(g_ffn)
    ghead_rep = jnp.broadcast_to(g_head[:, :, None], g_head.shape + (CHUNK,))
    gfin = g_final[None, :]

    xp = x_prompt
    Cp, npr, mp, bp = [], [], [], []
    for l in range(depth):
        xm, C1, n1, m1, pb1 = _prompt_mixer(xp, l, gmix, *win, ghead_rep, wpool, pscale, wout,
                                            tm=tm_mixer)
        last = l == depth - 1
        xp = _ffn(xm.reshape(B * T, D), l, gffn, wg, wu, wd, tm=tm_ffn,
                  gfin=gfin if last else None).reshape(B, T, D)
        Cp.append(C1)
        npr.append(n1[:, :, 0, :])
        mp.append(m1[:, :N_HEADS, 0])
        bp.append(pb1[:, POOL_CARRY - POOL_BUF:, :])
    y_prompt = xp

    xs = x_sample.reshape(nb, D)
    ns, ms, bs = [], [], []
    n_in = state_mlstm_n.reshape(depth, nb, MLSTM_WIDTH)
    m_pad = jnp.pad(state_mlstm_m, ((0, 0), (0, 0), (0, LANES - N_HEADS)))
    buf_t = jnp.transpose(state_pool_buf, (0, 2, 1, 3))
    C_sample = None
    for l in range(depth):
        proj = _sample_proj(xs, l, gmix, *win)
        mix, C_sample, n1, m1, buf1 = _sample_step(
            proj, l, ghead, wpool, pscale, state_mlstm_C, n_in, m_pad, buf_t, C_sample, bb=bb)
        last = l == depth - 1
        xs = _ffn(xs, l, gffn, wg, wu, wd, tm=nb, mix=mix, wout=wout,
                  gfin=gfin if last else None)
        ns.append(n1.reshape(nb, N_HEADS, HEAD_DIM))
        ms.append(m1[:, :N_HEADS])
        bs.append(jnp.transpose(buf1, (1, 0, 2)))
    y_sample = xs.reshape(nb, 1, D)

    st = lambda xs_: jnp.stack(xs_, 0)
    return (y_prompt, y_sample, st(Cp), st(npr), st(mp), st(bp),
            C_sample, st(ns), st(ms), st(bs))
```

```python
import functools

import jax
import jax.numpy as jnp
from jax import lax
from jax.experimental import pallas as pl
from jax.experimental.pallas import tpu as pltpu

F32 = jnp.float32
BF16 = jnp.bfloat16

EPS = 1e-6
N_HEADS = 4
HEAD_DIM = 128
MLSTM_WIDTH = N_HEADS * HEAD_DIM
POOL_WINDOWS = (2, 4, 8, 16)
POOL_GROUP = 128
POOL_WIDTH = POOL_GROUP * len(POOL_WINDOWS)
POOL_BUF = max(POOL_WINDOWS) - 1
POOL_CARRY = POOL_BUF + 1
CHUNK = 256
LANES = 128
FF_CHUNK = 256
SUBLANES = 8
N_ROWS = 2 * SUBLANES
GATE_SLAB = LANES

Q0, K0, U0 = 0, MLSTM_WIDTH, 2 * MLSTM_WIDTH
GI0 = U0 + POOL_WIDTH
GF0 = GI0 + GATE_SLAB
TOK_COLS = GF0 + GATE_SLAB
VT0, OT0 = 0, MLSTM_WIDTH
TR_ROWS = 2 * MLSTM_WIDTH
SV0 = TOK_COLS
SO0 = TOK_COLS + MLSTM_WIDTH
SAMPLE_COLS = TOK_COLS + TR_ROWS

V7X_VMEM_BYTES = 64 * 1024 * 1024

_NT = (((1,), (1,)), ((), ()))
_TN = (((0,), (0,)), ((), ()))


def _dot(a, b):
    return jnp.dot(a, b, preferred_element_type=F32)


def _dot_nt(a, b):
    return lax.dot_general(a, b, _NT, preferred_element_type=F32)


def _dot_tn(a, b):
    return lax.dot_general(a, b, _TN, preferred_element_type=F32)


def _rms(x, g):
    return x * lax.rsqrt(jnp.mean(x * x, axis=-1, keepdims=True) + EPS) * g


def _log_sigmoid(x):
    return jnp.minimum(x, 0.0) - jnp.log1p(jnp.exp(-jnp.abs(x)))


def _lane_col(slab, j):
    lane = lax.broadcasted_iota(jnp.int32, slab.shape, 1)
    return jnp.sum(jnp.where(lane == j, slab, 0.0), axis=-1, keepdims=True)


def _split3(x):
    hi = x.astype(BF16)
    r1 = x - hi.astype(F32)
    mid = r1.astype(BF16)
    lo = (r1 - mid.astype(F32)).astype(BF16)
    return hi, mid, lo


def _cumsum_rows(x):
    r = lax.broadcasted_iota(jnp.int32, (CHUNK, CHUNK), 0)
    c = lax.broadcasted_iota(jnp.int32, (CHUNK, CHUNK), 1)
    tril = (r >= c).astype(BF16)
    hi, mid, lo = _split3(x)
    return _dot(tril, hi) + _dot(tril, mid) + _dot(tril, lo)


def _head_stages(h, tok_ref, tr_ref, a_slab, a_t, cs_t, ghead_ref, C_ref, n_ref, m_ref,
                 mask_st, out):
    cols = slice(h * HEAD_DIM, (h + 1) * HEAD_DIM)
    head = lambda c0: slice(c0 + h * HEAD_DIM, c0 + (h + 1) * HEAD_DIM)
    C, n_rows, m = C_ref[h], n_ref[h], m_ref[h:h + 1, 0:1]
    qb = tok_ref[:, head(Q0)].astype(BF16)
    kb = tok_ref[:, head(K0)].astype(BF16)
    by_q = _dot_nt(jnp.concatenate([kb, C.astype(BF16), n_rows.astype(BF16)], axis=0), qb)
    yield
    sT = by_q[:CHUNK]
    cqT = by_q[CHUNK:CHUNK + HEAD_DIM]
    qn = by_q[CHUNK + HEAD_DIM:CHUNK + HEAD_DIM + 1]
    a_row, b_row = a_t[h:h + 1, :], cs_t[h:h + 1, :]
    aT = jnp.where(mask_st, a_slab[:, h:h + 1], -jnp.inf)
    M = jnp.maximum(jnp.max(aT, axis=0, keepdims=True), m)
    swT = sT * jnp.exp(aT - M)
    g = jnp.exp(m - M)
    den = jnp.sum(swT, axis=0, keepdims=True) + g * qn
    inv = 1.0 / jnp.maximum(jnp.abs(den), jnp.exp(-(b_row + M)))
    vT = tr_ref[head(VT0), :]
    hT = (_dot(vT.astype(BF16), swT.astype(BF16)) + g * cqT) * inv
    hnT = hT * lax.rsqrt(jnp.mean(hT * hT, axis=0, keepdims=True) + EPS)
    out[h] = (hnT * ghead_ref[cols, :] * jax.nn.sigmoid(tr_ref[head(OT0), :])).astype(BF16)
    yield
    M_last = M[:, CHUNK - 1:CHUNK]
    g_end = jnp.exp(m - M_last)
    w_end = jnp.exp(a_row - M_last)
    w_rows = jnp.broadcast_to(w_end, (N_ROWS, CHUNK))
    by_k = _dot(jnp.concatenate([(vT * w_end).astype(BF16), w_rows.astype(BF16)], axis=0), kb)
    C_ref[h] = g_end * C + by_k[:HEAD_DIM]
    n_ref[h] = g_end * n_rows + by_k[HEAD_DIM:]
    m_ref[h:h + 1, :] = jnp.broadcast_to(b_row[:, CHUNK - 1:CHUNK] + M_last, (1, LANES))
    yield


def _skewed(streams):
    streams = list(streams)
    live = []
    while streams or live:
        if streams:
            live.append(streams.pop(0))
        for g in list(live):
            try:
                next(g)
            except StopIteration:
                live.remove(g)
        if streams or live:
            yield


def _norm_tile(x_ref, gmix_ref, hn_ref):
    hn_ref[...] = _rms(x_ref[...], gmix_ref[...]).astype(BF16)


def _in_proj_items(hn_ref, wqk_ref, wu_ref, wgate_ref, wtr_ref, bslab_ref, tok_ref, tr_ref,
                   then=None):
    tok_ref[:, GI0:GI0 + 2 * GATE_SLAB] = _dot(hn_ref[...], wgate_ref[...]) + bslab_ref[...]
    yield
    tok_ref[:, Q0:Q0 + MLSTM_WIDTH] = _dot(hn_ref[...], wqk_ref[:, :MLSTM_WIDTH])
    yield
    tok_ref[:, K0:K0 + MLSTM_WIDTH] = (_dot(hn_ref[...], wqk_ref[:, MLSTM_WIDTH:])
                                       * (HEAD_DIM ** -0.5))
    yield
    tok_ref[:, U0:U0 + POOL_WIDTH] = _dot(hn_ref[...], wu_ref[...])
    yield
    for r0 in (VT0, OT0):
        tr_ref[r0:r0 + MLSTM_WIDTH, :] = _dot_nt(wtr_ref[r0:r0 + MLSTM_WIDTH, :], hn_ref[...])
        yield
    if then is not None:
        then()
        yield


def _mix_items(x_ref, tok_ref, tr_ref, t, ghead_ref, wpool_ref, pscale_ref, wout_ref,
               C_ref, n_ref, m_ref, pb_ref, out_ref, tm):
    r = lax.broadcasted_iota(jnp.int32, (CHUNK, CHUNK), 0)
    c = lax.broadcasted_iota(jnp.int32, (CHUNK, CHUNK), 1)
    mask_st = r <= c

    assert tm == CHUNK, "one mLSTM chunk per tile"
    cs = _cumsum_rows(_log_sigmoid(tok_ref[:, GF0:GF0 + GATE_SLAB]))
    a_slab = tok_ref[:, GI0:GI0 + GATE_SLAB] - cs
    cs_t = cs.T
    a_t = a_slab.T
    yield
    heads = [None] * N_HEADS
    yield from _skewed(
        _head_stages(h, tok_ref, tr_ref, a_slab, a_t, cs_t, ghead_ref, C_ref, n_ref, m_ref,
                     mask_st, heads)
        for h in range(N_HEADS))
    yield
    hmT = jnp.concatenate(heads, axis=0)

    u = tok_ref[:, U0:U0 + POOL_WIDTH]
    ext = jnp.concatenate([pb_ref[...], u], axis=0)
    pb_ref[...] = u[tm - POOL_CARRY:, :]
    pos = t * tm + lax.broadcasted_iota(jnp.int32, (tm, POOL_GROUP), 0)
    pooled = []
    for gi, win in enumerate(POOL_WINDOWS):
        cols = slice(gi * POOL_GROUP, (gi + 1) * POOL_GROUP)
        a = ext[:, cols]
        step = 1
        while step < win:
            a = a[step:, :] + a[:a.shape[0] - step, :]
            step *= 2
        a = a[a.shape[0] - tm:, :]
        cnt = jnp.minimum(win, pos + 1).astype(F32)
        p = a / cnt - u[:, cols]
        pooled.append((_dot(p.astype(BF16), wpool_ref[gi]) * pscale_ref[:, cols]).astype(BF16))
    yield
    pm = jnp.concatenate(pooled, axis=-1)
    out_ref[...] = (x_ref[...] + _dot_tn(hmT, wout_ref[:MLSTM_WIDTH, :])
                    + _dot(pm, wout_ref[MLSTM_WIDTH:, :]))
    yield


def _interleave(*streams):
    streams = list(streams)
    while streams:
        for s in list(streams):
            try:
                next(s)
            except StopIteration:
                streams.remove(s)


def _prompt_mixer_kernel(xcur_ref, xnext_ref, gmix_ref, wqk_ref, wu_ref, wgate_ref, wtr_ref,
                         bslab_ref, ghead_ref, wpool_ref, pscale_ref, wout_ref,
                         xo_ref, C_ref, n_ref, m_ref, pb_ref,
                         tok_a, tr_a, hn_a, tok_b, tr_b, hn_b, *, tm, nt):
    s = pl.program_id(0)
    t0 = lax.rem(2 * s, nt)
    proj_args = (wqk_ref, wu_ref, wgate_ref, wtr_ref, bslab_ref)
    mix_args = (ghead_ref, wpool_ref, pscale_ref, wout_ref, C_ref, n_ref, m_ref, pb_ref)

    @pl.when(s == 0)
    def _():
        _norm_tile(xcur_ref.at[0], gmix_ref, hn_a)
        _interleave(_in_proj_items(hn_a, *proj_args, tok_a, tr_a))
        _norm_tile(xcur_ref.at[1], gmix_ref, hn_b)

    @pl.when(t0 == 0)
    def _():
        C_ref[...] = jnp.zeros_like(C_ref)
        n_ref[...] = jnp.zeros_like(n_ref)
        m_ref[...] = jnp.zeros_like(m_ref)
        pb_ref[...] = jnp.zeros_like(pb_ref)

    _interleave(
        _in_proj_items(hn_b, *proj_args, tok_b, tr_b,
                       then=lambda: _norm_tile(xnext_ref.at[0], gmix_ref, hn_a)),
        _mix_items(xcur_ref.at[0], tok_a, tr_a, t0, *mix_args, xo_ref.at[0], tm))
    _interleave(
        _in_proj_items(hn_a, *proj_args, tok_a, tr_a,
                       then=lambda: _norm_tile(xnext_ref.at[1], gmix_ref, hn_b)),
        _mix_items(xcur_ref.at[1], tok_b, tr_b, t0 + 1, *mix_args, xo_ref.at[1], tm))


def _const_spec(shape):
    nd = len(shape)
    return pl.BlockSpec(shape, lambda *_: (0,) * nd, pipeline_mode=pl.Buffered(1))


def _layer_spec(a, l):
    nd = a.ndim - 1
    return pl.BlockSpec((None,) + a.shape[1:], lambda *_: (l,) + (0,) * nd,
                        pipeline_mode=pl.Buffered(1))


def _layer_bytes(*arrays):
    return sum(a.size // a.shape[0] * a.dtype.itemsize for a in arrays)


def _vmem_limit(nbytes):
    return int(min(V7X_VMEM_BYTES - (4 << 20), max(nbytes, 16 << 20)))


def _prompt_mixer(x, l, gmix, wqk, wu, wgate, wtr, bslab, ghead_rep, wpool, pscale, wout, *, tm):
    B, T, D = x.shape
    nt = T // tm
    assert nt % 2 == 0, "a grid step covers two tiles of one sequence"
    ntiles = B * nt
    xt = x.reshape(ntiles, tm, D)
    pair = pl.BlockSpec((2, tm, D), lambda s: (s, 0, 0))
    nxt = pl.BlockSpec((2, tm, D), lambda s: (jnp.minimum(s + 1, ntiles // 2 - 1), 0, 0))
    seq = lambda s: (2 * s) // nt
    out_shapes = (
        jax.ShapeDtypeStruct((ntiles, tm, D), F32),
        jax.ShapeDtypeStruct((B, N_HEADS, HEAD_DIM, HEAD_DIM), F32),
        jax.ShapeDtypeStruct((B, N_HEADS, N_ROWS, HEAD_DIM), F32),
        jax.ShapeDtypeStruct((B, SUBLANES, LANES), F32),
        jax.ShapeDtypeStruct((B, POOL_CARRY, POOL_WIDTH), F32),
    )
    out_specs = (
        pair,
        pl.BlockSpec((None, N_HEADS, HEAD_DIM, HEAD_DIM), lambda s: (seq(s), 0, 0, 0)),
        pl.BlockSpec((None, N_HEADS, N_ROWS, HEAD_DIM), lambda s: (seq(s), 0, 0, 0)),
        pl.BlockSpec((None, SUBLANES, LANES), lambda s: (seq(s), 0, 0)),
        pl.BlockSpec((None, POOL_CARRY, POOL_WIDTH), lambda s: (seq(s), 0, 0)),
    )
    consts = (gmix, wqk, wu, wgate, wtr, bslab, ghead_rep, wpool, pscale, wout)
    in_specs = [pair, nxt] + [_layer_spec(a, l) for a in consts]
    est = _layer_bytes(*consts) + 12 * tm * D * 4 + 12 * tm * SAMPLE_COLS * 4
    scratch = [pltpu.VMEM((tm, TOK_COLS), F32), pltpu.VMEM((TR_ROWS, tm), F32),
               pltpu.VMEM((tm, D), BF16)] * 2
    outs = pl.pallas_call(
        functools.partial(_prompt_mixer_kernel, tm=tm, nt=nt),
        grid=(ntiles // 2,), in_specs=in_specs, out_specs=out_specs, out_shape=out_shapes,
        scratch_shapes=scratch,
        compiler_params=pltpu.CompilerParams(
            dimension_semantics=("arbitrary",),
            vmem_limit_bytes=_vmem_limit(est)),
        name="prompt_mixer",
    )(xt, xt, *consts)
    return (outs[0].reshape(B, T, D),) + tuple(outs[1:])


def _ffn_kernel(*refs, has_mix, final, n_chunks):
    refs = list(refs)
    x_ref = refs.pop(0)
    if has_mix:
        mix_ref, wout_ref = refs.pop(0), refs.pop(0)
    gffn_ref, wg_ref, wu_ref, wd_ref = refs[:4]
    refs = refs[4:]
    if final:
        gfin_ref = refs.pop(0)
    out_ref, wg_s, wu_s, wd_s = refs
    i = pl.program_id(0)
    fc = wg_ref.shape[-1]

    for c in range(n_chunks):
        @pl.when(i == c)
        def _(c=c):
            wg_s[:, c * fc:(c + 1) * fc] = wg_ref[...].astype(BF16)
            wu_s[:, c * fc:(c + 1) * fc] = wu_ref[...].astype(BF16)
            wd_s[c * fc:(c + 1) * fc, :] = wd_ref[...].astype(BF16)

    @pl.when(i >= n_chunks)
    def _():
        x = x_ref[...]
        if has_mix:
            x = x + _dot(mix_ref[...].astype(BF16), wout_ref[...])
        h2 = _rms(x, gffn_ref[...]).astype(BF16)
        a = _dot(h2, wg_s[...])
        bu = _dot(h2, wu_s[...])
        act = (a * jax.nn.sigmoid(a) * bu).astype(BF16)
        y = x + _dot(act, wd_s[...])
        out_ref[...] = _rms(y, gfin_ref[...]) if final else y


def _ffn(x, l, gffn, wg, wu, wd, *, tm, mix=None, wout=None, gfin=None):
    M, D = x.shape
    dff = wg.shape[-1]
    n_chunks = dff // FF_CHUNK
    assert n_chunks * FF_CHUNK == dff
    chunk = lambda i: jnp.minimum(i, n_chunks - 1)
    tile = pl.BlockSpec((tm, D), lambda i: (jnp.maximum(i - n_chunks, 0), 0))
    args, in_specs = [x], [tile]
    if mix is not None:
        args += [mix, wout]
        in_specs += [tile, _layer_spec(wout, l)]
    args += [gffn, wg, wu, wd]
    in_specs += [
        _layer_spec(gffn, l),
        pl.BlockSpec((None, D, FF_CHUNK), lambda i: (l, 0, chunk(i))),
        pl.BlockSpec((None, D, FF_CHUNK), lambda i: (l, 0, chunk(i))),
        pl.BlockSpec((None, FF_CHUNK, D), lambda i: (l, chunk(i), 0)),
    ]
    if gfin is not None:
        args.append(gfin)
        in_specs.append(_const_spec(gfin.shape))
    scratch = [pltpu.VMEM((D, dff), BF16), pltpu.VMEM((D, dff), BF16), pltpu.VMEM((dff, D), BF16)]
    est = (3 * D * dff * 2 + 6 * D * FF_CHUNK * 4 + (_layer_bytes(wout) if mix is not None else 0)
           + 6 * tm * D * 4 + 4 * tm * dff * 4)
    return pl.pallas_call(
        functools.partial(_ffn_kernel, has_mix=mix is not None, final=gfin is not None,
                          n_chunks=n_chunks),
        grid=(n_chunks + M // tm,), in_specs=in_specs, out_specs=tile,
        out_shape=jax.ShapeDtypeStruct((M, D), F32),
        scratch_shapes=scratch,
        compiler_params=pltpu.CompilerParams(
            dimension_semantics=("arbitrary",), vmem_limit_bytes=_vmem_limit(est)),
        name="ffn_final" if gfin is not None else "ffn",
    )(*args)


def _sample_proj_kernel(x_ref, gmix_ref, wqk_ref, wu_ref, wgate_ref, wtr_ref, bslab_ref, proj_ref):
    hn = _rms(x_ref[...], gmix_ref[...]).astype(BF16)
    proj_ref[:, Q0:U0] = _dot(hn, wqk_ref[...])
    proj_ref[:, U0:GI0] = _dot(hn, wu_ref[...])
    proj_ref[:, GI0:TOK_COLS] = _dot(hn, wgate_ref[...]) + bslab_ref[...]
    proj_ref[:, TOK_COLS:] = _dot_nt(hn, wtr_ref[...])


def _sample_proj(x, l, gmix, wqk, wu, wgate, wtr, bslab):
    M = x.shape[0]
    consts = (gmix, wqk, wu, wgate, wtr, bslab)
    est = 2 * _layer_bytes(*consts) + 8 * M * SAMPLE_COLS * 4
    return pl.pallas_call(
        _sample_proj_kernel,
        grid=(1,),
        in_specs=[_const_spec(x.shape)] + [_layer_spec(a, l) for a in consts],
        out_specs=pl.BlockSpec((M, SAMPLE_COLS), lambda i: (0, 0)),
        out_shape=jax.ShapeDtypeStruct((M, SAMPLE_COLS), F32),
        compiler_params=pltpu.CompilerParams(
            dimension_semantics=("arbitrary",), vmem_limit_bytes=_vmem_limit(est)),
        name="sample_proj",
    )(x, *consts)


def _head_out(h, g_head, o):
    hn = h * lax.rsqrt(jnp.mean(h * h, axis=-1, keepdims=True) + EPS)
    return hn * g_head * jax.nn.sigmoid(o)


def _sample_step_kernel(proj_ref, ghead_ref, wpool_ref, pscale_ref, C_ref, n_ref, m_ref, buf_ref,
                        acc_ref, mix_ref, Co_ref, no_ref, mo_ref, bufo_ref,
                        g_s, wvt_s, kb_s, qb_s, cq_s, *, bb):
    del acc_ref
    i = pl.program_id(0)
    nb = proj_ref.shape[0]
    scale = HEAD_DIM ** -0.5
    head = lambda c0, h: proj_ref[:, c0 + h * HEAD_DIM:c0 + (h + 1) * HEAD_DIM]

    def gate_terms(h):
        i_c = _lane_col(proj_ref[:, GI0:GI0 + GATE_SLAB], h)
        lf = _lane_col(_log_sigmoid(proj_ref[:, GF0:GF0 + GATE_SLAB]), h)
        m0 = _lane_col(m_ref[...], h)
        inter = lf + m0
        m_t = jnp.maximum(inter, i_c)
        return jnp.exp(i_c - m_t), jnp.exp(inter - m_t), m_t

    @pl.when(i == 0)
    def _():
        for h in range(N_HEADS):
            w, g, _ = gate_terms(h)
            g_s[h] = jnp.broadcast_to(g, (nb, LANES))
            wvt_s[h] = (w * head(SV0, h)).T
            kb_s[h] = (head(K0, h) * scale).astype(BF16)
            qb_s[h] = head(Q0, h).astype(BF16)
            cq_s[h] = jnp.zeros((HEAD_DIM, nb), F32)

    lane = lax.broadcasted_iota(jnp.int32, (HEAD_DIM, nb), 1)

    for h in range(N_HEADS):
        Cblk = C_ref[:, h]
        r = _dot_nt(Cblk.reshape(bb * HEAD_DIM, HEAD_DIM).astype(BF16), qb_s[h])
        cq = cq_s[h]
        lhs = []
        for j in range(bb):
            sel = lane == i * bb + j
            cq = jnp.where(sel, r[j * HEAD_DIM:(j + 1) * HEAD_DIM], cq)
            lhs.append(jnp.where(sel, wvt_s[h], 0.0).astype(BF16))
        cq_s[h] = cq
        outer = _dot(jnp.concatenate(lhs, axis=0), kb_s[h])
        for j in range(bb):
            Co_ref[j, h] = (g_s[h, pl.ds(i * bb + j, 1), :] * Cblk[j]
                            + outer[j * HEAD_DIM:(j + 1) * HEAD_DIM])

    @pl.when(i == pl.num_programs(0) - 1)
    def _():
        heads, n_heads = [], []
        lane_m = lax.broadcasted_iota(jnp.int32, (nb, LANES), 1)
        m_out = jnp.zeros((nb, LANES), F32)
        for h in range(N_HEADS):
            cols = slice(h * HEAD_DIM, (h + 1) * HEAD_DIM)
            w, g, m_t = gate_terms(h)
            qf = head(Q0, h)
            kf = head(K0, h) * scale
            vf = head(SV0, h)
            n0 = n_ref[:, cols]
            qk = jnp.sum(qf.astype(BF16).astype(F32) * kf.astype(BF16).astype(F32),
                         axis=-1, keepdims=True)
            s = qk * w
            num = s * vf + g * cq_s[h].T
            den = s + g * jnp.sum(n0 * qf, axis=-1, keepdims=True)
            hc = num * (1.0 / jnp.maximum(jnp.abs(den), jnp.exp(-m_t)))
            heads.append(_head_out(hc, ghead_ref[:, cols], head(SO0, h)))
            n_heads.append(g * n0 + w * kf)
            m_out = jnp.where(lane_m == h, m_t, m_out)
        no_ref[...] = jnp.concatenate(n_heads, axis=-1)
        mo_ref[...] = m_out

        u = proj_ref[:, U0:U0 + POOL_WIDTH]
        pooled = []
        for gi, win in enumerate(POOL_WINDOWS):
            cols = slice(gi * POOL_GROUP, (gi + 1) * POOL_GROUP)
            a = u[:, cols]
            for jrow in range(POOL_BUF - (win - 1), POOL_BUF):
                a = a + buf_ref[jrow, :, cols]
            p = a / float(win) - u[:, cols]
            pooled.append(_dot(p.astype(BF16), wpool_ref[gi]) * pscale_ref[:, cols])
        mix_ref[...] = jnp.concatenate(heads + pooled, axis=-1)
        for jrow in range(POOL_BUF - 1):
            bufo_ref[jrow] = buf_ref[jrow + 1]
        bufo_ref[POOL_BUF - 1] = u


def _sample_step(proj, l, ghead, wpool, pscale, C, n, m, buf_t, C_acc, *, bb):
    nb = proj.shape[0]
    D = MLSTM_WIDTH + POOL_WIDTH
    cblk = pl.BlockSpec((None, bb, N_HEADS, HEAD_DIM, HEAD_DIM), lambda i: (l, i, 0, 0, 0))
    in_specs = [_const_spec(proj.shape)] + [_layer_spec(a, l) for a in (ghead, wpool, pscale)] + [
        cblk, _layer_spec(n, l), _layer_spec(m, l), _layer_spec(buf_t, l),
        pl.BlockSpec(memory_space=pl.ANY)]
    args = [proj, ghead, wpool, pscale, C, n, m, buf_t, C_acc]
    aliases = {len(args) - 1: 1}
    out_shapes = (
        jax.ShapeDtypeStruct((nb, D), F32),
        jax.ShapeDtypeStruct(C.shape, F32),
        jax.ShapeDtypeStruct(n.shape[1:], F32),
        jax.ShapeDtypeStruct((nb, LANES), F32),
        jax.ShapeDtypeStruct(buf_t.shape[1:], F32),
    )
    full = lambda s: pl.BlockSpec(s, lambda i: (0,) * len(s))
    out_specs = (full((nb, D)), cblk, full(n.shape[1:]), full((nb, LANES)), full(buf_t.shape[1:]))
    scratch = [
        pltpu.VMEM((N_HEADS, nb, LANES), F32),
        pltpu.VMEM((N_HEADS, HEAD_DIM, nb), F32),
        pltpu.VMEM((N_HEADS, nb, HEAD_DIM), BF16),
        pltpu.VMEM((N_HEADS, nb, HEAD_DIM), BF16),
        pltpu.VMEM((N_HEADS, HEAD_DIM, nb), F32),
    ]
    est = (4 * bb * N_HEADS * HEAD_DIM * HEAD_DIM * 4 + 4 * _layer_bytes(buf_t)
           + 8 * proj.size * 4)
    return pl.pallas_call(
        functools.partial(_sample_step_kernel, bb=bb),
        grid=(nb // bb,), in_specs=in_specs, out_specs=out_specs, out_shape=out_shapes,
        scratch_shapes=scratch, input_output_aliases=aliases,
        compiler_params=pltpu.CompilerParams(
            dimension_semantics=("arbitrary",), vmem_limit_bytes=_vmem_limit(est)),
        name="sample_step",
    )(*args)


def _pack_w_in(w_in, b_gate):
    depth, d, _ = w_in.shape
    w = MLSTM_WIDTH
    g0 = 4 * w
    wqk = w_in[:, :, 0:2 * w].astype(BF16)
    wu = w_in[:, :, g0 + 2 * N_HEADS:].astype(BF16)
    pad = jnp.zeros((depth, d, GATE_SLAB - N_HEADS), BF16)
    gates = w_in[:, :, g0:g0 + 2 * N_HEADS].astype(BF16)
    wgate = jnp.concatenate([gates[:, :, :N_HEADS], pad, gates[:, :, N_HEADS:], pad], axis=-1)
    wtr = jnp.swapaxes(w_in[:, :, 2 * w:4 * w], 1, 2).astype(BF16)
    bpad = jnp.zeros((depth, GATE_SLAB - N_HEADS), b_gate.dtype)
    bslab = jnp.concatenate([b_gate[:, :N_HEADS], bpad, b_gate[:, N_HEADS:], bpad], axis=-1)
    return wqk, wu, wgate, wtr, bslab[:, None, :]


def kernel(x_prompt, x_sample, state_mlstm_C, state_mlstm_n, state_mlstm_m, state_pool_buf,
           g_mix, w_in, b_gate, g_head, w_pool, pool_scale, w_out, g_ffn, w_gate, w_up,
           w_down, g_final):
    depth = w_in.shape[0]
    B, T, D = x_prompt.shape
    nb = x_sample.shape[0]
    tm_mixer, tm_ffn, bb = 256, 512, 16

    wqk, wu_in, wgate, wtr, bslab = _pack_w_in(w_in, b_gate)
    win = (wqk, wu_in, wgate, wtr, bslab)
    wpool = w_pool.astype(BF16)
    wout = w_out.astype(BF16)
    wg, wu, wd = w_gate, w_up, w_down
    row = lambda a: a[:, None, :]
    gmix, ghead, pscale, gffn = row(g_mix), row(g_head), row(pool_scale), row(g_ffn)
    ghead_rep = jnp.broadcast_to(g_head[:, :, None], g_head.shape + (CHUNK,))
    gfin = g_final[None, :]

    xp = x_prompt
    Cp, npr, mp, bp = [], [], [], []
    for l in range(depth):
        xm, C1, n1, m1, pb1 = _prompt_mixer(xp, l, gmix, *win, ghead_rep, wpool, pscale, wout,
                                            tm=tm_mixer)
        last = l == depth - 1
        xp = _ffn(xm.reshape(B * T, D), l, gffn, wg, wu, wd, tm=tm_ffn,
                  gfin=gfin if last else None).reshape(B, T, D)
        Cp.append(C1)
        npr.append(n1[:, :, 0, :])
        mp.append(m1[:, :N_HEADS, 0])
        bp.append(pb1[:, POOL_CARRY - POOL_BUF:, :])
    y_prompt = xp

    xs = x_sample.reshape(nb, D)
    ns, ms, bs = [], [], []
    n_in = state_mlstm_n.reshape(depth, nb, MLSTM_WIDTH)
    m_pad = jnp.pad(state_mlstm_m, ((0, 0), (0, 0), (0, LANES - N_HEADS)))
    buf_t = jnp.transpose(state_pool_buf, (0, 2, 1, 3))
    C_sample = jnp.zeros(state_mlstm_C.shape, F32)
    for l in range(depth):
        proj = _sample_proj(xs, l, gmix, *win)
        mix, C_sample, n1, m1, buf1 = _sample_step(
            proj, l, ghead, wpool, pscale, state_mlstm_C, n_in, m_pad, buf_t, C_sample, bb=bb)
        last = l == depth - 1
        xs = _ffn(xs, l, gffn, wg, wu, wd, tm=nb, mix=mix, wout=wout,
                  gfin=gfin if last else None)
        ns.append(n1.reshape(nb, N_HEADS, HEAD_DIM))
        ms.append(m1[:, :N_HEADS])
        bs.append(jnp.transpose(buf1, (1, 0, 2)))
    y_sample = xs.reshape(nb, 1, D)

    st = lambda xs_: jnp.stack(xs_, 0)
    return (y_prompt, y_sample, st(Cp), st(npr), st(mp), st(bp),
            C_sample, st(ns), st(ms), st(bs))
```

```python
import functools

import jax
import jax.numpy as jnp
from jax import lax
from jax.experimental import pallas as pl
from jax.experimental.pallas import tpu as pltpu

F32 = jnp.float32
BF16 = jnp.bfloat16

EPS = 1e-6
N_HEADS = 4
HEAD_DIM = 128
MLSTM_WIDTH = N_HEADS * HEAD_DIM
POOL_WINDOWS = (2, 4, 8, 16)
POOL_GROUP = 128
POOL_WIDTH = POOL_GROUP * len(POOL_WINDOWS)
POOL_BUF = max(POOL_WINDOWS) - 1
POOL_CARRY = POOL_BUF + 1
CHUNK = 256
LANES = 128
FF_CHUNK = 256
SUBLANES = 8
N_ROWS = 2 * SUBLANES
GATE_SLAB = LANES

Q0, K0, U0 = 0, MLSTM_WIDTH, 2 * MLSTM_WIDTH
GI0 = U0 + POOL_WIDTH
GF0 = GI0 + GATE_SLAB
TOK_COLS = GF0 + GATE_SLAB
VT0, OT0 = 0, MLSTM_WIDTH
TR_ROWS = 2 * MLSTM_WIDTH
SV0 = TOK_COLS
SO0 = TOK_COLS + MLSTM_WIDTH
SAMPLE_COLS = TOK_COLS + TR_ROWS

V7X_VMEM_BYTES = 64 * 1024 * 1024

_NT = (((1,), (1,)), ((), ()))
_TN = (((0,), (0,)), ((), ()))


def _dot(a, b):
    return jnp.dot(a, b, preferred_element_type=F32)


def _dot_nt(a, b):
    return lax.dot_general(a, b, _NT, preferred_element_type=F32)


def _dot_tn(a, b):
    return lax.dot_general(a, b, _TN, preferred_element_type=F32)


def _rms(x, g):
    return x * lax.rsqrt(jnp.mean(x * x, axis=-1, keepdims=True) + EPS) * g


def _log_sigmoid(x):
    return jnp.minimum(x, 0.0) - jnp.log1p(jnp.exp(-jnp.abs(x)))


def _lane_col(slab, j):
    lane = lax.broadcasted_iota(jnp.int32, slab.shape, 1)
    return jnp.sum(jnp.where(lane == j, slab, 0.0), axis=-1, keepdims=True)


def _split3(x):
    hi = x.astype(BF16)
    r1 = x - hi.astype(F32)
    mid = r1.astype(BF16)
    lo = (r1 - mid.astype(F32)).astype(BF16)
    return hi, mid, lo


def _cumsum_rows(x):
    r = lax.broadcasted_iota(jnp.int32, (CHUNK, CHUNK), 0)
    c = lax.broadcasted_iota(jnp.int32, (CHUNK, CHUNK), 1)
    tril = (r >= c).astype(BF16)
    hi, mid, lo = _split3(x)
    return _dot(tril, hi) + _dot(tril, mid) + _dot(tril, lo)


def _head_stages(h, tok_ref, tr_ref, a_slab, a_t, cs_t, ghead_ref, C_ref, n_ref, m_ref,
                 mask_st, out):
    cols = slice(h * HEAD_DIM, (h + 1) * HEAD_DIM)
    head = lambda c0: slice(c0 + h * HEAD_DIM, c0 + (h + 1) * HEAD_DIM)
    C, n_rows, m = C_ref[h], n_ref[h], m_ref[h:h + 1, 0:1]
    qb = tok_ref[:, head(Q0)].astype(BF16)
    kb = tok_ref[:, head(K0)].astype(BF16)
    by_q = _dot_nt(jnp.concatenate([kb, C.astype(BF16), n_rows.astype(BF16)], axis=0), qb)
    yield
    sT = by_q[:CHUNK]
    cqT = by_q[CHUNK:CHUNK + HEAD_DIM]
    qn = by_q[CHUNK + HEAD_DIM:CHUNK + HEAD_DIM + 1]
    a_row, b_row = a_t[h:h + 1, :], cs_t[h:h + 1, :]
    aT = jnp.where(mask_st, a_slab[:, h:h + 1], -jnp.inf)
    M = jnp.maximum(jnp.max(aT, axis=0, keepdims=True), m)
    swT = sT * jnp.exp(aT - M)
    g = jnp.exp(m - M)
    den = jnp.sum(swT, axis=0, keepdims=True) + g * qn
    inv = 1.0 / jnp.maximum(jnp.abs(den), jnp.exp(-(b_row + M)))
    vT = tr_ref[head(VT0), :]
    hT = (_dot(vT.astype(BF16), swT.astype(BF16)) + g * cqT) * inv
    hnT = hT * lax.rsqrt(jnp.mean(hT * hT, axis=0, keepdims=True) + EPS)
    out[h] = (hnT * ghead_ref[cols, :] * jax.nn.sigmoid(tr_ref[head(OT0), :])).astype(BF16)
    yield
    M_last = M[:, CHUNK - 1:CHUNK]
    g_end = jnp.exp(m - M_last)
    w_end = jnp.exp(a_row - M_last)
    w_rows = jnp.broadcast_to(w_end, (N_ROWS, CHUNK))
    by_k = _dot(jnp.concatenate([(vT * w_end).astype(BF16), w_rows.astype(BF16)], axis=0), kb)
    C_ref[h] = g_end * C + by_k[:HEAD_DIM]
    n_ref[h] = g_end * n_rows + by_k[HEAD_DIM:]
    m_ref[h:h + 1, :] = jnp.broadcast_to(b_row[:, CHUNK - 1:CHUNK] + M_last, (1, LANES))
    yield


def _skewed(streams):
    streams = list(streams)
    live = []
    while streams or live:
        if streams:
            live.append(streams.pop(0))
        for g in list(live):
            try:
                next(g)
            except StopIteration:
                live.remove(g)
        if streams or live:
            yield


def _norm_tile(x_ref, gmix_ref, hn_ref):
    hn_ref[...] = _rms(x_ref[...], gmix_ref[...]).astype(BF16)


def _in_proj_items(hn_ref, wqk_ref, wu_ref, wgate_ref, wtr_ref, bslab_ref, tok_ref, tr_ref,
                   then=None):
    tok_ref[:, GI0:GI0 + 2 * GATE_SLAB] = _dot(hn_ref[...], wgate_ref[...]) + bslab_ref[...]
    yield
    tok_ref[:, Q0:Q0 + MLSTM_WIDTH] = _dot(hn_ref[...], wqk_ref[:, :MLSTM_WIDTH])
    yield
    tok_ref[:, K0:K0 + MLSTM_WIDTH] = (_dot(hn_ref[...], wqk_ref[:, MLSTM_WIDTH:])
                                       * (HEAD_DIM ** -0.5))
    yield
    tok_ref[:, U0:U0 + POOL_WIDTH] = _dot(hn_ref[...], wu_ref[...])
    yield
    for r0 in (VT0, OT0):
        tr_ref[r0:r0 + MLSTM_WIDTH, :] = _dot_nt(wtr_ref[r0:r0 + MLSTM_WIDTH, :], hn_ref[...])
        yield
    if then is not None:
        then()
        yield


def _mix_items(x_ref, tok_ref, tr_ref, t, ghead_ref, wpool_ref, pscale_ref, wout_ref,
               C_ref, n_ref, m_ref, pb_ref, out_ref, tm):
    r = lax.broadcasted_iota(jnp.int32, (CHUNK, CHUNK), 0)
    c = lax.broadcasted_iota(jnp.int32, (CHUNK, CHUNK), 1)
    mask_st = r <= c

    assert tm == CHUNK, "one mLSTM chunk per tile"
    cs = _cumsum_rows(_log_sigmoid(tok_ref[:, GF0:GF0 + GATE_SLAB]))
    a_slab = tok_ref[:, GI0:GI0 + GATE_SLAB] - cs
    cs_t = cs.T
    a_t = a_slab.T
    yield
    heads = [None] * N_HEADS
    yield from _skewed(
        _head_stages(h, tok_ref, tr_ref, a_slab, a_t, cs_t, ghead_ref, C_ref, n_ref, m_ref,
                     mask_st, heads)
        for h in range(N_HEADS))
    yield
    hmT = jnp.concatenate(heads, axis=0)

    u = tok_ref[:, U0:U0 + POOL_WIDTH]
    ext = jnp.concatenate([pb_ref[...], u], axis=0)
    pb_ref[...] = u[tm - POOL_CARRY:, :]
    pos = t * tm + lax.broadcasted_iota(jnp.int32, (tm, POOL_GROUP), 0)
    pooled = []
    for gi, win in enumerate(POOL_WINDOWS):
        cols = slice(gi * POOL_GROUP, (gi + 1) * POOL_GROUP)
        a = ext[:, cols]
        step = 1
        while step < win:
            a = a[step:, :] + a[:a.shape[0] - step, :]
            step *= 2
        a = a[a.shape[0] - tm:, :]
        cnt = jnp.minimum(win, pos + 1).astype(F32)
        p = a / cnt - u[:, cols]
        pooled.append((_dot(p.astype(BF16), wpool_ref[gi]) * pscale_ref[:, cols]).astype(BF16))
    yield
    pm = jnp.concatenate(pooled, axis=-1)
    out_ref[...] = (x_ref[...] + _dot_tn(hmT, wout_ref[:MLSTM_WIDTH, :])
                    + _dot(pm, wout_ref[MLSTM_WIDTH:, :]))
    yield


def _interleave(*streams):
    streams = list(streams)
    while streams:
        for s in list(streams):
            try:
                next(s)
            except StopIteration:
                streams.remove(s)


def _prompt_mixer_kernel(xcur_ref, xnext_ref, gmix_ref, wqk_ref, wu_ref, wgate_ref, wvo_ref,
                         bslab_ref, ghead_ref, wpool_ref, pscale_ref, wout_ref,
                         xo_ref, C_ref, n_ref, m_ref, pb_ref,
                         tok_a, tr_a, hn_a, tok_b, tr_b, hn_b, wtr_s, *, tm, nt):
    s = pl.program_id(0)
    t0 = lax.rem(2 * s, nt)
    proj_args = (wqk_ref, wu_ref, wgate_ref, wtr_s, bslab_ref)
    mix_args = (ghead_ref, wpool_ref, pscale_ref, wout_ref, C_ref, n_ref, m_ref, pb_ref)

    @pl.when(s == 0)
    def _():
        wtr_s[...] = wvo_ref[...].T
        _norm_tile(xcur_ref.at[0], gmix_ref, hn_a)
        _interleave(_in_proj_items(hn_a, *proj_args, tok_a, tr_a))
        _norm_tile(xcur_ref.at[1], gmix_ref, hn_b)

    @pl.when(t0 == 0)
    def _():
        C_ref[...] = jnp.zeros_like(C_ref)
        n_ref[...] = jnp.zeros_like(n_ref)
        m_ref[...] = jnp.zeros_like(m_ref)
        pb_ref[...] = jnp.zeros_like(pb_ref)

    _interleave(
        _in_proj_items(hn_b, *proj_args, tok_b, tr_b,
                       then=lambda: _norm_tile(xnext_ref.at[0], gmix_ref, hn_a)),
        _mix_items(xcur_ref.at[0], tok_a, tr_a, t0, *mix_args, xo_ref.at[0], tm))
    _interleave(
        _in_proj_items(hn_a, *proj_args, tok_a, tr_a,
                       then=lambda: _norm_tile(xnext_ref.at[1], gmix_ref, hn_b)),
        _mix_items(xcur_ref.at[1], tok_b, tr_b, t0 + 1, *mix_args, xo_ref.at[1], tm))


def _const_spec(shape):
    nd = len(shape)
    return pl.BlockSpec(shape, lambda *_: (0,) * nd, pipeline_mode=pl.Buffered(1))


def _layer_spec(a, l):
    nd = a.ndim - 1
    return pl.BlockSpec((None,) + a.shape[1:], lambda *_: (l,) + (0,) * nd,
                        pipeline_mode=pl.Buffered(1))


def _in_hbm(*arrays):
    return [pltpu.with_memory_space_constraint(a, pltpu.MemorySpace.HBM) for a in arrays]


def _layer_bytes(*arrays):
    return sum(a.size // a.shape[0] * a.dtype.itemsize for a in arrays)


def _vmem_limit(nbytes):
    return int(min(V7X_VMEM_BYTES - (4 << 20), max(nbytes, 16 << 20)))


def _prompt_mixer(x, l, gmix, wqk, wu, wgate, wvo, bslab, ghead_rep, wpool, pscale, wout, *, tm):
    B, T, D = x.shape
    nt = T // tm
    assert nt % 2 == 0, "a grid step covers two tiles of one sequence"
    ntiles = B * nt
    xt = x.reshape(ntiles, tm, D)
    pair = pl.BlockSpec((2, tm, D), lambda s: (s, 0, 0))
    nxt = pl.BlockSpec((2, tm, D), lambda s: (jnp.minimum(s + 1, ntiles // 2 - 1), 0, 0))
    seq = lambda s: (2 * s) // nt
    out_shapes = (
        jax.ShapeDtypeStruct((ntiles, tm, D), F32),
        jax.ShapeDtypeStruct((B, N_HEADS, HEAD_DIM, HEAD_DIM), F32),
        jax.ShapeDtypeStruct((B, N_HEADS, N_ROWS, HEAD_DIM), F32),
        jax.ShapeDtypeStruct((B, SUBLANES, LANES), F32),
        jax.ShapeDtypeStruct((B, POOL_CARRY, POOL_WIDTH), F32),
    )
    out_specs = (
        pair,
        pl.BlockSpec((None, N_HEADS, HEAD_DIM, HEAD_DIM), lambda s: (seq(s), 0, 0, 0)),
        pl.BlockSpec((None, N_HEADS, N_ROWS, HEAD_DIM), lambda s: (seq(s), 0, 0, 0)),
        pl.BlockSpec((None, SUBLANES, LANES), lambda s: (seq(s), 0, 0)),
        pl.BlockSpec((None, POOL_CARRY, POOL_WIDTH), lambda s: (seq(s), 0, 0)),
    )
    consts = (gmix, wqk, wu, wgate, wvo, bslab, ghead_rep, wpool, pscale, wout)
    in_specs = [pair, nxt] + [_layer_spec(a, l) for a in consts]
    est = _layer_bytes(*consts) + 12 * tm * D * 4 + 12 * tm * SAMPLE_COLS * 4
    scratch = [pltpu.VMEM((tm, TOK_COLS), F32), pltpu.VMEM((TR_ROWS, tm), F32),
               pltpu.VMEM((tm, D), BF16)] * 2 + [pltpu.VMEM((TR_ROWS, D), BF16)]
    outs = pl.pallas_call(
        functools.partial(_prompt_mixer_kernel, tm=tm, nt=nt),
        grid=(ntiles // 2,), in_specs=in_specs, out_specs=out_specs, out_shape=out_shapes,
        scratch_shapes=scratch,
        compiler_params=pltpu.CompilerParams(
            dimension_semantics=("arbitrary",),
            vmem_limit_bytes=_vmem_limit(est)),
        name="prompt_mixer",
    )(*_in_hbm(xt, xt, *consts))
    return (outs[0].reshape(B, T, D),) + tuple(outs[1:])


def _ffn_tile(x, gffn_ref, wg_s, wu_s, wd_s, gfin_ref):
    h2 = _rms(x, gffn_ref[...]).astype(BF16)
    a = _dot(h2, wg_s[...])
    bu = _dot(h2, wu_s[...])
    act = (a * jax.nn.sigmoid(a) * bu).astype(BF16)
    y = x + _dot(act, wd_s[...])
    return y if gfin_ref is None else _rms(y, gfin_ref[...])


def _ffn_kernel(*refs, final, n_chunks, n_tiles):
    refs = list(refs)
    x_ref, xs_ref, mixs_ref, wout_ref, gffn_ref, wg_ref, wu_ref, wd_ref = refs[:8]
    refs = refs[8:]
    gfin_ref = refs.pop(0) if final else None
    out_ref, outs_ref, wg_s, wu_s, wd_s = refs
    i = pl.program_id(0)
    fc = wg_ref.shape[-1]
    weights = (gffn_ref, wg_s, wu_s, wd_s, gfin_ref)

    for c in range(n_chunks):
        @pl.when(i == c)
        def _(c=c):
            wg_s[:, c * fc:(c + 1) * fc] = wg_ref[...].astype(BF16)
            wu_s[:, c * fc:(c + 1) * fc] = wu_ref[...].astype(BF16)
            wd_s[c * fc:(c + 1) * fc, :] = wd_ref[...].astype(BF16)

    @pl.when((i >= n_chunks) & (i < n_chunks + n_tiles))
    def _():
        out_ref[...] = _ffn_tile(x_ref[...], *weights)

    @pl.when(i == n_chunks + n_tiles)
    def _():
        xs = xs_ref[...] + _dot(mixs_ref[...].astype(BF16), wout_ref[...])
        outs_ref[...] = _ffn_tile(xs, *weights)


def _ffn(x, xs, mixs, l, gffn, wg, wu, wd, wout, *, tm, gfin=None):
    M, D = x.shape
    dff = wg.shape[-1]
    n_chunks, n_tiles = dff // FF_CHUNK, M // tm
    assert n_chunks * FF_CHUNK == dff and n_tiles * tm == M
    chunk = lambda i: jnp.minimum(i, n_chunks - 1)
    tile = pl.BlockSpec((tm, D), lambda i: (jnp.clip(i - n_chunks, 0, n_tiles - 1), 0))
    args = [x, xs, mixs, wout, gffn, wg, wu, wd]
    in_specs = [
        tile, _const_spec(xs.shape), _const_spec(mixs.shape), _layer_spec(wout, l),
        _layer_spec(gffn, l),
        pl.BlockSpec((None, D, FF_CHUNK), lambda i: (l, 0, chunk(i))),
        pl.BlockSpec((None, D, FF_CHUNK), lambda i: (l, 0, chunk(i))),
        pl.BlockSpec((None, FF_CHUNK, D), lambda i: (l, chunk(i), 0)),
    ]
    if gfin is not None:
        args.append(gfin)
        in_specs.append(_const_spec(gfin.shape))
    scratch = [pltpu.VMEM((D, dff), BF16), pltpu.VMEM((D, dff), BF16), pltpu.VMEM((dff, D), BF16)]
    est = (3 * D * dff * 2 + 6 * D * FF_CHUNK * 4 + _layer_bytes(wout) + 6 * xs.size * 4
           + 6 * tm * D * 4 + 4 * tm * dff * 4)
    return pl.pallas_call(
        functools.partial(_ffn_kernel, final=gfin is not None, n_chunks=n_chunks, n_tiles=n_tiles),
        grid=(n_chunks + n_tiles + 1,), in_specs=in_specs,
        out_specs=(tile, pl.BlockSpec(xs.shape, lambda i: (0, 0))),
        out_shape=(jax.ShapeDtypeStruct((M, D), F32), jax.ShapeDtypeStruct(xs.shape, F32)),
        scratch_shapes=scratch,
        compiler_params=pltpu.CompilerParams(
            dimension_semantics=("arbitrary",), vmem_limit_bytes=_vmem_limit(est)),
        name="ffn_final" if gfin is not None else "ffn",
    )(*_in_hbm(*args))


def _sample_proj_kernel(x_ref, gmix_ref, wqk_ref, wu_ref, wgate_ref, wvo_ref, bslab_ref, proj_ref):
    hn = _rms(x_ref[...], gmix_ref[...]).astype(BF16)
    proj_ref[:, Q0:U0] = _dot(hn, wqk_ref[...])
    proj_ref[:, U0:GI0] = _dot(hn, wu_ref[...])
    proj_ref[:, GI0:TOK_COLS] = _dot(hn, wgate_ref[...]) + bslab_ref[...]
    proj_ref[:, TOK_COLS:] = _dot(hn, wvo_ref[...])


def _sample_proj(x, l, gmix, wqk, wu, wgate, wvo, bslab):
    M = x.shape[0]
    consts = (gmix, wqk, wu, wgate, wvo, bslab)
    est = 2 * _layer_bytes(*consts) + 8 * M * SAMPLE_COLS * 4
    return pl.pallas_call(
        _sample_proj_kernel,
        grid=(1,),
        in_specs=[_const_spec(x.shape)] + [_layer_spec(a, l) for a in consts],
        out_specs=pl.BlockSpec((M, SAMPLE_COLS), lambda i: (0, 0)),
        out_shape=jax.ShapeDtypeStruct((M, SAMPLE_COLS), F32),
        compiler_params=pltpu.CompilerParams(
            dimension_semantics=("arbitrary",), vmem_limit_bytes=_vmem_limit(est)),
        name="sample_proj",
    )(*_in_hbm(x, *consts))


def _head_out(h, g_head, o):
    hn = h * lax.rsqrt(jnp.mean(h * h, axis=-1, keepdims=True) + EPS)
    return hn * g_head * jax.nn.sigmoid(o)


def _sample_step_kernel(proj_ref, ghead_ref, wpool_ref, pscale_ref, C_ref, n_ref, m_ref, buf_ref,
                        acc_ref, mix_ref, Co_ref, no_ref, mo_ref, bufo_ref,
                        g_s, wvt_s, kb_s, qb_s, cq_s, *, bb):
    del acc_ref
    i = pl.program_id(0)
    nb = proj_ref.shape[0]
    scale = HEAD_DIM ** -0.5
    head = lambda c0, h: proj_ref[:, c0 + h * HEAD_DIM:c0 + (h + 1) * HEAD_DIM]

    def gate_terms(h):
        i_c = _lane_col(proj_ref[:, GI0:GI0 + GATE_SLAB], h)
        lf = _lane_col(_log_sigmoid(proj_ref[:, GF0:GF0 + GATE_SLAB]), h)
        m0 = _lane_col(m_ref[...], h)
        inter = lf + m0
        m_t = jnp.maximum(inter, i_c)
        return jnp.exp(i_c - m_t), jnp.exp(inter - m_t), m_t

    @pl.when(i == 0)
    def _():
        for h in range(N_HEADS):
            w, g, _ = gate_terms(h)
            g_s[h] = jnp.broadcast_to(g, (nb, LANES))
            wvt_s[h] = (w * head(SV0, h)).T
            kb_s[h] = (head(K0, h) * scale).astype(BF16)
            qb_s[h] = head(Q0, h).astype(BF16)
            cq_s[h] = jnp.zeros((HEAD_DIM, nb), F32)

    lane = lax.broadcasted_iota(jnp.int32, (HEAD_DIM, nb), 1)

    for h in range(N_HEADS):
        Cblk = C_ref[:, h]
        r = _dot_nt(Cblk.reshape(bb * HEAD_DIM, HEAD_DIM).astype(BF16), qb_s[h])
        cq = cq_s[h]
        lhs = []
        for j in range(bb):
            sel = lane == i * bb + j
            cq = jnp.where(sel, r[j * HEAD_DIM:(j + 1) * HEAD_DIM], cq)
            lhs.append(jnp.where(sel, wvt_s[h], 0.0).astype(BF16))
        cq_s[h] = cq
        outer = _dot(jnp.concatenate(lhs, axis=0), kb_s[h])
        for j in range(bb):
            Co_ref[j, h] = (g_s[h, pl.ds(i * bb + j, 1), :] * Cblk[j]
                            + outer[j * HEAD_DIM:(j + 1) * HEAD_DIM])

    @pl.when(i == pl.num_programs(0) - 1)
    def _():
        heads, n_heads = [], []
        lane_m = lax.broadcasted_iota(jnp.int32, (nb, LANES), 1)
        m_out = jnp.zeros((nb, LANES), F32)
        for h in range(N_HEADS):
            cols = slice(h * HEAD_DIM, (h + 1) * HEAD_DIM)
            w, g, m_t = gate_terms(h)
            qf = head(Q0, h)
            kf = head(K0, h) * scale
            vf = head(SV0, h)
            n0 = n_ref[:, cols]
            qk = jnp.sum(qf.astype(BF16).astype(F32) * kf.astype(BF16).astype(F32),
                         axis=-1, keepdims=True)
            s = qk * w
            num = s * vf + g * cq_s[h].T
            den = s + g * jnp.sum(n0 * qf, axis=-1, keepdims=True)
            hc = num * (1.0 / jnp.maximum(jnp.abs(den), jnp.exp(-m_t)))
            heads.append(_head_out(hc, ghead_ref[:, cols], head(SO0, h)))
            n_heads.append(g * n0 + w * kf)
            m_out = jnp.where(lane_m == h, m_t, m_out)
        no_ref[...] = jnp.concatenate(n_heads, axis=-1)
        mo_ref[...] = m_out

        u = proj_ref[:, U0:U0 + POOL_WIDTH]
        pooled = []
        for gi, win in enumerate(POOL_WINDOWS):
            cols = slice(gi * POOL_GROUP, (gi + 1) * POOL_GROUP)
            a = u[:, cols]
            for jrow in range(POOL_BUF - (win - 1), POOL_BUF):
                a = a + buf_ref[jrow, :, cols]
            p = a / float(win) - u[:, cols]
            pooled.append(_dot(p.astype(BF16), wpool_ref[gi]) * pscale_ref[:, cols])
        mix_ref[...] = jnp.concatenate(heads + pooled, axis=-1)
        for jrow in range(POOL_BUF - 1):
            bufo_ref[jrow] = buf_ref[jrow + 1]
        bufo_ref[POOL_BUF - 1] = u


def _sample_step(proj, l, ghead, wpool, pscale, C, n, m, buf_t, C_acc, *, bb):
    nb = proj.shape[0]
    D = MLSTM_WIDTH + POOL_WIDTH
    cblk = pl.BlockSpec((None, bb, N_HEADS, HEAD_DIM, HEAD_DIM), lambda i: (l, i, 0, 0, 0))
    in_specs = [_const_spec(proj.shape)] + [_layer_spec(a, l) for a in (ghead, wpool, pscale)] + [
        cblk, _layer_spec(n, l), _layer_spec(m, l), _layer_spec(buf_t, l),
        pl.BlockSpec(memory_space=pl.ANY)]
    args = [proj, ghead, wpool, pscale, C, n, m, buf_t, C_acc]
    aliases = {len(args) - 1: 1}
    out_shapes = (
        jax.ShapeDtypeStruct((nb, D), F32),
        jax.ShapeDtypeStruct(C.shape, F32),
        jax.ShapeDtypeStruct(n.shape[1:], F32),
        jax.ShapeDtypeStruct((nb, LANES), F32),
        jax.ShapeDtypeStruct(buf_t.shape[1:], F32),
    )
    full = lambda s: pl.BlockSpec(s, lambda i: (0,) * len(s))
    out_specs = (full((nb, D)), cblk, full(n.shape[1:]), full((nb, LANES)), full(buf_t.shape[1:]))
    scratch = [
        pltpu.VMEM((N_HEADS, nb, LANES), F32),
        pltpu.VMEM((N_HEADS, HEAD_DIM, nb), F32),
        pltpu.VMEM((N_HEADS, nb, HEAD_DIM), BF16),
        pltpu.VMEM((N_HEADS, nb, HEAD_DIM), BF16),
        pltpu.VMEM((N_HEADS, HEAD_DIM, nb), F32),
    ]
    est = (4 * bb * N_HEADS * HEAD_DIM * HEAD_DIM * 4 + 4 * _layer_bytes(buf_t)
           + 8 * proj.size * 4)
    return pl.pallas_call(
        functools.partial(_sample_step_kernel, bb=bb),
        grid=(nb // bb,), in_specs=in_specs, out_specs=out_specs, out_shape=out_shapes,
        scratch_shapes=scratch, input_output_aliases=aliases,
        compiler_params=pltpu.CompilerParams(
            dimension_semantics=("arbitrary",), vmem_limit_bytes=_vmem_limit(est)),
        name="sample_step",
    )(*_in_hbm(*args))


def _pack_w_in(w_in, b_gate):
    depth, d, _ = w_in.shape
    w = MLSTM_WIDTH
    g0 = 4 * w
    wqk = w_in[:, :, 0:2 * w].astype(BF16)
    wu = w_in[:, :, g0 + 2 * N_HEADS:].astype(BF16)
    pad = jnp.zeros((depth, d, GATE_SLAB - N_HEADS), BF16)
    gates = w_in[:, :, g0:g0 + 2 * N_HEADS].astype(BF16)
    wgate = jnp.concatenate([gates[:, :, :N_HEADS], pad, gates[:, :, N_HEADS:], pad], axis=-1)
    wvo = w_in[:, :, 2 * w:4 * w].astype(BF16)
    bpad = jnp.zeros((depth, GATE_SLAB - N_HEADS), b_gate.dtype)
    bslab = jnp.concatenate([b_gate[:, :N_HEADS], bpad, b_gate[:, N_HEADS:], bpad], axis=-1)
    return wqk, wu, wgate, wvo, bslab[:, None, :]


def kernel(x_prompt, x_sample, state_mlstm_C, state_mlstm_n, state_mlstm_m, state_pool_buf,
           g_mix, w_in, b_gate, g_head, w_pool, pool_scale, w_out, g_ffn, w_gate, w_up,
           w_down, g_final):
    depth = w_in.shape[0]
    B, T, D = x_prompt.shape
    nb = x_sample.shape[0]
    tm_mixer, tm_ffn, bb = 256, 512, 16

    win = _pack_w_in(w_in, b_gate)
    wpool = w_pool.astype(BF16)
    wout = w_out.astype(BF16)
    wg, wu, wd = w_gate, w_up, w_down
    row = lambda a: a[:, None, :]
    gmix, ghead, pscale, gffn = row(g_mix), row(g_head), row(pool_scale), row(g_ffn)
    ghead_rep = jnp.broadcast_to(g_head[:, :, None], g_head.shape + (CHUNK,))
    gfin = g_final[None, :]

    n_in = state_mlstm_n.reshape(depth, nb, MLSTM_WIDTH)
    m_pad = jnp.pad(state_mlstm_m, ((0, 0), (0, 0), (0, LANES - N_HEADS)))
    buf_t = jnp.transpose(state_pool_buf, (0, 2, 1, 3))
    C_sample = jnp.zeros(state_mlstm_C.shape, F32)

    xp = x_prompt
    xs = x_sample.reshape(nb, D)
    Cp, npr, mp, bp, ns, ms, bs = [], [], [], [], [], [], []
    for l in range(depth):
        xm, C1, n1, m1, pb1 = _prompt_mixer(xp, l, gmix, *win, ghead_rep, wpool, pscale, wout,
                                            tm=tm_mixer)
        Cp.append(C1)
        npr.append(n1[:, :, 0, :])
        mp.append(m1[:, :N_HEADS, 0])
        bp.append(pb1[:, POOL_CARRY - POOL_BUF:, :])

        proj = _sample_proj(xs, l, gmix, *win)
        mix, C_sample, n1, m1, buf1 = _sample_step(
            proj, l, ghead, wpool, pscale, state_mlstm_C, n_in, m_pad, buf_t, C_sample, bb=bb)
        ns.append(n1.reshape(nb, N_HEADS, HEAD_DIM))
        ms.append(m1[:, :N_HEADS])
        bs.append(jnp.transpose(buf1, (1, 0, 2)))

        xp, xs = _ffn(xm.reshape(B * T, D), xs, mix, l, gffn, wg, wu, wd, wout, tm=tm_ffn,
                      gfin=gfin if l == depth - 1 else None)
        xp = xp.reshape(B, T, D)
    y_prompt = xp
    y_sample = xs.reshape(nb, 1, D)

    st = lambda xs_: jnp.stack(xs_, 0)
    return (y_prompt, y_sample, st(Cp), st(npr), st(mp), st(bp),
            C_sample, st(ns), st(ms), st(bs))
```

```python
import functools

import jax
import jax.numpy as jnp
from jax import lax
from jax.experimental import pallas as pl
from jax.experimental.pallas import tpu as pltpu

F32 = jnp.float32
BF16 = jnp.bfloat16

EPS = 1e-6
N_HEADS = 4
HEAD_DIM = 128
MLSTM_WIDTH = N_HEADS * HEAD_DIM
POOL_WINDOWS = (2, 4, 8, 16)
POOL_GROUP = 128
POOL_WIDTH = POOL_GROUP * len(POOL_WINDOWS)
POOL_BUF = max(POOL_WINDOWS) - 1
POOL_CARRY = POOL_BUF + 1
CHUNK = 256
LANES = 128
FF_CHUNK = 256
FF_SUBTILE = 256
SUBLANES = 8
N_ROWS = 2 * SUBLANES
GATE_SLAB = LANES

Q0, K0, U0 = 0, MLSTM_WIDTH, 2 * MLSTM_WIDTH
GI0 = U0 + POOL_WIDTH
GF0 = GI0 + GATE_SLAB
TOK_COLS = GF0 + GATE_SLAB
VT0, OT0 = 0, MLSTM_WIDTH
TR_ROWS = 2 * MLSTM_WIDTH
SV0 = TOK_COLS
SO0 = TOK_COLS + MLSTM_WIDTH
SAMPLE_COLS = TOK_COLS + TR_ROWS

V7X_VMEM_BYTES = 64 * 1024 * 1024

_NT = (((1,), (1,)), ((), ()))
_TN = (((0,), (0,)), ((), ()))


def _dot(a, b):
    return jnp.dot(a, b, preferred_element_type=F32)


def _dot_nt(a, b):
    return lax.dot_general(a, b, _NT, preferred_element_type=F32)


def _dot_tn(a, b):
    return lax.dot_general(a, b, _TN, preferred_element_type=F32)


def _rms(x, g):
    return x * lax.rsqrt(jnp.mean(x * x, axis=-1, keepdims=True) + EPS) * g


def _log_sigmoid(x):
    return jnp.minimum(x, 0.0) - jnp.log1p(jnp.exp(-jnp.abs(x)))


def _lane_col(slab, j):
    lane = lax.broadcasted_iota(jnp.int32, slab.shape, 1)
    return jnp.sum(jnp.where(lane == j, slab, 0.0), axis=-1, keepdims=True)


def _split3(x):
    hi = x.astype(BF16)
    r1 = x - hi.astype(F32)
    mid = r1.astype(BF16)
    lo = (r1 - mid.astype(F32)).astype(BF16)
    return hi, mid, lo


def _cumsum_rows(x):
    r = lax.broadcasted_iota(jnp.int32, (CHUNK, CHUNK), 0)
    c = lax.broadcasted_iota(jnp.int32, (CHUNK, CHUNK), 1)
    tril = (r >= c).astype(BF16)
    hi, mid, lo = _split3(x)
    return _dot(tril, hi) + _dot(tril, mid) + _dot(tril, lo)


def _head_stages(h, tok_ref, tr_ref, a_slab, a_t, cs_t, ghead_ref, C_ref, n_ref, m_ref,
                 mask_st, out):
    cols = slice(h * HEAD_DIM, (h + 1) * HEAD_DIM)
    head = lambda c0: slice(c0 + h * HEAD_DIM, c0 + (h + 1) * HEAD_DIM)
    C, n_rows, m = C_ref[h], n_ref[h], m_ref[h:h + 1, 0:1]
    qb = tok_ref[:, head(Q0)].astype(BF16)
    kb = tok_ref[:, head(K0)].astype(BF16)
    by_q = _dot_nt(jnp.concatenate([kb, C.astype(BF16), n_rows.astype(BF16)], axis=0), qb)
    yield
    sT = by_q[:CHUNK]
    cqT = by_q[CHUNK:CHUNK + HEAD_DIM]
    qn = by_q[CHUNK + HEAD_DIM:CHUNK + HEAD_DIM + 1]
    a_row, b_row = a_t[h:h + 1, :], cs_t[h:h + 1, :]
    aT = jnp.where(mask_st, a_slab[:, h:h + 1], -jnp.inf)
    M = jnp.maximum(jnp.max(aT, axis=0, keepdims=True), m)
    swT = sT * jnp.exp(aT - M)
    g = jnp.exp(m - M)
    den = jnp.sum(swT, axis=0, keepdims=True) + g * qn
    inv = 1.0 / jnp.maximum(jnp.abs(den), jnp.exp(-(b_row + M)))
    vT = tr_ref[head(VT0), :]
    hT = (_dot(vT.astype(BF16), swT.astype(BF16)) + g * cqT) * inv
    hnT = hT * lax.rsqrt(jnp.mean(hT * hT, axis=0, keepdims=True) + EPS)
    out[h] = (hnT * ghead_ref[cols, :] * jax.nn.sigmoid(tr_ref[head(OT0), :])).astype(BF16)
    yield
    M_last = M[:, CHUNK - 1:CHUNK]
    g_end = jnp.exp(m - M_last)
    w_end = jnp.exp(a_row - M_last)
    w_rows = jnp.broadcast_to(w_end, (N_ROWS, CHUNK))
    by_k = _dot(jnp.concatenate([(vT * w_end).astype(BF16), w_rows.astype(BF16)], axis=0), kb)
    C_ref[h] = g_end * C + by_k[:HEAD_DIM]
    n_ref[h] = g_end * n_rows + by_k[HEAD_DIM:]
    m_ref[h:h + 1, :] = jnp.broadcast_to(b_row[:, CHUNK - 1:CHUNK] + M_last, (1, LANES))
    yield


def _skewed(streams):
    streams = list(streams)
    live = []
    while streams or live:
        if streams:
            live.append(streams.pop(0))
        for g in list(live):
            try:
                next(g)
            except StopIteration:
                live.remove(g)
        if streams or live:
            yield


def _norm_tile(x_ref, gmix_ref, hn_ref):
    hn_ref[...] = _rms(x_ref[...], gmix_ref[...]).astype(BF16)


def _in_proj_items(hn_ref, wqk_ref, wu_ref, wgate_ref, wtr_ref, bslab_ref, tok_ref, tr_ref,
                   then=None):
    tok_ref[:, GI0:GI0 + 2 * GATE_SLAB] = _dot(hn_ref[...], wgate_ref[...]) + bslab_ref[...]
    yield
    tok_ref[:, Q0:Q0 + MLSTM_WIDTH] = _dot(hn_ref[...], wqk_ref[:, :MLSTM_WIDTH])
    yield
    tok_ref[:, K0:K0 + MLSTM_WIDTH] = (_dot(hn_ref[...], wqk_ref[:, MLSTM_WIDTH:])
                                       * (HEAD_DIM ** -0.5))
    yield
    tok_ref[:, U0:U0 + POOL_WIDTH] = _dot(hn_ref[...], wu_ref[...])
    yield
    for r0 in (VT0, OT0):
        tr_ref[r0:r0 + MLSTM_WIDTH, :] = _dot_nt(wtr_ref[r0:r0 + MLSTM_WIDTH, :], hn_ref[...])
        yield
    if then is not None:
        then()
        yield


def _mix_items(x_ref, tok_ref, tr_ref, t, ghead_ref, wpool_ref, pscale_ref, wout_ref,
               C_ref, n_ref, m_ref, pb_ref, out_ref, tm):
    r = lax.broadcasted_iota(jnp.int32, (CHUNK, CHUNK), 0)
    c = lax.broadcasted_iota(jnp.int32, (CHUNK, CHUNK), 1)
    mask_st = r <= c

    assert tm == CHUNK, "one mLSTM chunk per tile"
    cs = _cumsum_rows(_log_sigmoid(tok_ref[:, GF0:GF0 + GATE_SLAB]))
    a_slab = tok_ref[:, GI0:GI0 + GATE_SLAB] - cs
    cs_t = cs.T
    a_t = a_slab.T
    yield
    heads = [None] * N_HEADS
    yield from _skewed(
        _head_stages(h, tok_ref, tr_ref, a_slab, a_t, cs_t, ghead_ref, C_ref, n_ref, m_ref,
                     mask_st, heads)
        for h in range(N_HEADS))
    yield
    hmT = jnp.concatenate(heads, axis=0)

    u = tok_ref[:, U0:U0 + POOL_WIDTH]
    ext = jnp.concatenate([pb_ref[...], u], axis=0)
    pb_ref[...] = u[tm - POOL_CARRY:, :]
    pos = t * tm + lax.broadcasted_iota(jnp.int32, (tm, POOL_GROUP), 0)
    pooled = []
    for gi, win in enumerate(POOL_WINDOWS):
        cols = slice(gi * POOL_GROUP, (gi + 1) * POOL_GROUP)
        a = ext[:, cols]
        step = 1
        while step < win:
            a = a[step:, :] + a[:a.shape[0] - step, :]
            step *= 2
        a = a[a.shape[0] - tm:, :]
        cnt = jnp.minimum(win, pos + 1).astype(F32)
        p = a / cnt - u[:, cols]
        pooled.append((_dot(p.astype(BF16), wpool_ref[gi]) * pscale_ref[:, cols]).astype(BF16))
    yield
    pm = jnp.concatenate(pooled, axis=-1)
    out_ref[...] = (x_ref[...] + _dot_tn(hmT, wout_ref[:MLSTM_WIDTH, :])
                    + _dot(pm, wout_ref[MLSTM_WIDTH:, :]))
    yield


def _interleave(*streams):
    streams = list(streams)
    while streams:
        for s in list(streams):
            try:
                next(s)
            except StopIteration:
                streams.remove(s)


def _prompt_mixer_kernel(xcur_ref, xnext_ref, gmix_ref, wqk_ref, wu_ref, wgate_ref, wvo_ref,
                         bslab_ref, ghead_ref, wpool_ref, pscale_ref, wout_ref,
                         xo_ref, C_ref, n_ref, m_ref, pb_ref,
                         tok_a, tr_a, hn_a, tok_b, tr_b, hn_b, wtr_s, *, tm, nt):
    s = pl.program_id(0)
    t0 = lax.rem(2 * s, nt)
    proj_args = (wqk_ref, wu_ref, wgate_ref, wtr_s, bslab_ref)
    mix_args = (ghead_ref, wpool_ref, pscale_ref, wout_ref, C_ref, n_ref, m_ref, pb_ref)

    @pl.when(s == 0)
    def _():
        wtr_s[...] = wvo_ref[...].T
        _norm_tile(xcur_ref.at[0], gmix_ref, hn_a)
        _interleave(_in_proj_items(hn_a, *proj_args, tok_a, tr_a))
        _norm_tile(xcur_ref.at[1], gmix_ref, hn_b)

    @pl.when(t0 == 0)
    def _():
        C_ref[...] = jnp.zeros_like(C_ref)
        n_ref[...] = jnp.zeros_like(n_ref)
        m_ref[...] = jnp.zeros_like(m_ref)
        pb_ref[...] = jnp.zeros_like(pb_ref)

    _interleave(
        _in_proj_items(hn_b, *proj_args, tok_b, tr_b,
                       then=lambda: _norm_tile(xnext_ref.at[0], gmix_ref, hn_a)),
        _mix_items(xcur_ref.at[0], tok_a, tr_a, t0, *mix_args, xo_ref.at[0], tm))
    _interleave(
        _in_proj_items(hn_a, *proj_args, tok_a, tr_a,
                       then=lambda: _norm_tile(xnext_ref.at[1], gmix_ref, hn_b)),
        _mix_items(xcur_ref.at[1], tok_b, tr_b, t0 + 1, *mix_args, xo_ref.at[1], tm))


def _const_spec(shape):
    nd = len(shape)
    return pl.BlockSpec(shape, lambda *_: (0,) * nd, pipeline_mode=pl.Buffered(1))


def _layer_spec(a, l):
    nd = a.ndim - 1
    return pl.BlockSpec((None,) + a.shape[1:], lambda *_: (l,) + (0,) * nd,
                        pipeline_mode=pl.Buffered(1))


def _in_hbm(*arrays):
    return [pltpu.with_memory_space_constraint(a, pltpu.MemorySpace.HBM) for a in arrays]


def _layer_bytes(*arrays):
    return sum(a.size // a.shape[0] * a.dtype.itemsize for a in arrays)


def _vmem_limit(nbytes):
    return int(min(V7X_VMEM_BYTES - (4 << 20), max(nbytes, 16 << 20)))


def _prompt_mixer(x, l, gmix, wqk, wu, wgate, wvo, bslab, ghead_rep, wpool, pscale, wout, *, tm):
    B, T, D = x.shape
    nt = T // tm
    assert nt % 2 == 0, "a grid step covers two tiles of one sequence"
    ntiles = B * nt
    xt = x.reshape(ntiles, tm, D)
    pair = pl.BlockSpec((2, tm, D), lambda s: (s, 0, 0))
    nxt = pl.BlockSpec((2, tm, D), lambda s: (jnp.minimum(s + 1, ntiles // 2 - 1), 0, 0))
    seq = lambda s: (2 * s) // nt
    out_shapes = (
        jax.ShapeDtypeStruct((ntiles, tm, D), F32),
        jax.ShapeDtypeStruct((B, N_HEADS, HEAD_DIM, HEAD_DIM), F32),
        jax.ShapeDtypeStruct((B, N_HEADS, N_ROWS, HEAD_DIM), F32),
        jax.ShapeDtypeStruct((B, SUBLANES, LANES), F32),
        jax.ShapeDtypeStruct((B, POOL_CARRY, POOL_WIDTH), F32),
    )
    out_specs = (
        pair,
        pl.BlockSpec((None, N_HEADS, HEAD_DIM, HEAD_DIM), lambda s: (seq(s), 0, 0, 0)),
        pl.BlockSpec((None, N_HEADS, N_ROWS, HEAD_DIM), lambda s: (seq(s), 0, 0, 0)),
        pl.BlockSpec((None, SUBLANES, LANES), lambda s: (seq(s), 0, 0)),
        pl.BlockSpec((None, POOL_CARRY, POOL_WIDTH), lambda s: (seq(s), 0, 0)),
    )
    consts = (gmix, wqk, wu, wgate, wvo, bslab, ghead_rep, wpool, pscale, wout)
    in_specs = [pair, nxt] + [_layer_spec(a, l) for a in consts]
    est = _layer_bytes(*consts) + 12 * tm * D * 4 + 12 * tm * SAMPLE_COLS * 4
    scratch = [pltpu.VMEM((tm, TOK_COLS), F32), pltpu.VMEM((TR_ROWS, tm), F32),
               pltpu.VMEM((tm, D), BF16)] * 2 + [pltpu.VMEM((TR_ROWS, D), BF16)]
    outs = pl.pallas_call(
        functools.partial(_prompt_mixer_kernel, tm=tm, nt=nt),
        grid=(ntiles // 2,), in_specs=in_specs, out_specs=out_specs, out_shape=out_shapes,
        scratch_shapes=scratch,
        compiler_params=pltpu.CompilerParams(
            dimension_semantics=("arbitrary",),
            vmem_limit_bytes=_vmem_limit(est)),
        name="prompt_mixer",
    )(*_in_hbm(xt, xt, *consts))
    return (outs[0].reshape(B, T, D),) + tuple(outs[1:])


def _ffn_tile(x, gffn_ref, wg_s, wu_s, wd_s, gfin_ref):
    h2 = _rms(x, gffn_ref[...]).astype(BF16)
    a = _dot(h2, wg_s[...])
    bu = _dot(h2, wu_s[...])
    act = (a * jax.nn.sigmoid(a) * bu).astype(BF16)
    y = x + _dot(act, wd_s[...])
    return y if gfin_ref is None else _rms(y, gfin_ref[...])


def _ffn_kernel(*refs, final, n_chunks, n_tiles):
    refs = list(refs)
    x_ref, xs_ref, mixs_ref, wout_ref, gffn_ref, wg_ref, wu_ref, wd_ref = refs[:8]
    refs = refs[8:]
    gfin_ref = refs.pop(0) if final else None
    out_ref, outs_ref, wg_s, wu_s, wd_s = refs
    i = pl.program_id(0)
    fc = wg_ref.shape[-1]
    weights = (gffn_ref, wg_s, wu_s, wd_s, gfin_ref)

    for c in range(n_chunks):
        @pl.when(i == c)
        def _(c=c):
            wg_s[:, c * fc:(c + 1) * fc] = wg_ref[...].astype(BF16)
            wu_s[:, c * fc:(c + 1) * fc] = wu_ref[...].astype(BF16)
            wd_s[c * fc:(c + 1) * fc, :] = wd_ref[...].astype(BF16)

    @pl.when((i >= n_chunks) & (i < n_chunks + n_tiles))
    def _():
        for r0 in range(0, x_ref.shape[0], FF_SUBTILE):
            rows = slice(r0, r0 + FF_SUBTILE)
            out_ref[rows, :] = _ffn_tile(x_ref[rows, :], *weights)

    @pl.when(i == n_chunks + n_tiles)
    def _():
        xs = xs_ref[...] + _dot(mixs_ref[...].astype(BF16), wout_ref[...])
        outs_ref[...] = _ffn_tile(xs, *weights)


def _ffn(x, xs, mixs, l, gffn, wg, wu, wd, wout, *, tm, gfin=None):
    M, D = x.shape
    dff = wg.shape[-1]
    n_chunks, n_tiles = dff // FF_CHUNK, M // tm
    assert n_chunks * FF_CHUNK == dff and n_tiles * tm == M
    chunk = lambda i: jnp.minimum(i, n_chunks - 1)
    tile = pl.BlockSpec((tm, D), lambda i: (jnp.clip(i - n_chunks, 0, n_tiles - 1), 0))
    args = [x, xs, mixs, wout, gffn, wg, wu, wd]
    in_specs = [
        tile, _const_spec(xs.shape), _const_spec(mixs.shape), _layer_spec(wout, l),
        _layer_spec(gffn, l),
        pl.BlockSpec((None, D, FF_CHUNK), lambda i: (l, 0, chunk(i))),
        pl.BlockSpec((None, D, FF_CHUNK), lambda i: (l, 0, chunk(i))),
        pl.BlockSpec((None, FF_CHUNK, D), lambda i: (l, chunk(i), 0)),
    ]
    if gfin is not None:
        args.append(gfin)
        in_specs.append(_const_spec(gfin.shape))
    scratch = [pltpu.VMEM((D, dff), BF16), pltpu.VMEM((D, dff), BF16), pltpu.VMEM((dff, D), BF16)]
    est = (3 * D * dff * 2 + 6 * D * FF_CHUNK * 4 + _layer_bytes(wout) + 6 * xs.size * 4
           + 6 * tm * D * 4 + 4 * tm * dff * 4)
    return pl.pallas_call(
        functools.partial(_ffn_kernel, final=gfin is not None, n_chunks=n_chunks, n_tiles=n_tiles),
        grid=(n_chunks + n_tiles + 1,), in_specs=in_specs,
        out_specs=(tile, pl.BlockSpec(xs.shape, lambda i: (0, 0))),
        out_shape=(jax.ShapeDtypeStruct((M, D), F32), jax.ShapeDtypeStruct(xs.shape, F32)),
        scratch_shapes=scratch,
        compiler_params=pltpu.CompilerParams(
            dimension_semantics=("arbitrary",), vmem_limit_bytes=_vmem_limit(est)),
        name="ffn_final" if gfin is not None else "ffn",
    )(*_in_hbm(*args))


def _sample_proj_kernel(x_ref, gmix_ref, wqk_ref, wu_ref, wgate_ref, wvo_ref, bslab_ref, proj_ref):
    hn = _rms(x_ref[...], gmix_ref[...]).astype(BF16)
    proj_ref[:, Q0:U0] = _dot(hn, wqk_ref[...])
    proj_ref[:, U0:GI0] = _dot(hn, wu_ref[...])
    proj_ref[:, GI0:TOK_COLS] = _dot(hn, wgate_ref[...]) + bslab_ref[...]
    proj_ref[:, TOK_COLS:] = _dot(hn, wvo_ref[...])


def _sample_proj(x, l, gmix, wqk, wu, wgate, wvo, bslab):
    M = x.shape[0]
    consts = (gmix, wqk, wu, wgate, wvo, bslab)
    est = 2 * _layer_bytes(*consts) + 8 * M * SAMPLE_COLS * 4
    return pl.pallas_call(
        _sample_proj_kernel,
        grid=(1,),
        in_specs=[_const_spec(x.shape)] + [_layer_spec(a, l) for a in consts],
        out_specs=pl.BlockSpec((M, SAMPLE_COLS), lambda i: (0, 0)),
        out_shape=jax.ShapeDtypeStruct((M, SAMPLE_COLS), F32),
        compiler_params=pltpu.CompilerParams(
            dimension_semantics=("arbitrary",), vmem_limit_bytes=_vmem_limit(est)),
        name="sample_proj",
    )(*_in_hbm(x, *consts))


def _head_out(h, g_head, o):
    hn = h * lax.rsqrt(jnp.mean(h * h, axis=-1, keepdims=True) + EPS)
    return hn * g_head * jax.nn.sigmoid(o)


def _sample_step_kernel(proj_ref, ghead_ref, wpool_ref, pscale_ref, C_ref, n_ref, m_ref, buf_ref,
                        acc_ref, mix_ref, Co_ref, no_ref, mo_ref, bufo_ref,
                        g_s, wvt_s, kb_s, qb_s, cq_s, *, bb):
    del acc_ref
    i = pl.program_id(0)
    nb = proj_ref.shape[0]
    scale = HEAD_DIM ** -0.5
    head = lambda c0, h: proj_ref[:, c0 + h * HEAD_DIM:c0 + (h + 1) * HEAD_DIM]

    def gate_terms(h):
        i_c = _lane_col(proj_ref[:, GI0:GI0 + GATE_SLAB], h)
        lf = _lane_col(_log_sigmoid(proj_ref[:, GF0:GF0 + GATE_SLAB]), h)
        m0 = _lane_col(m_ref[...], h)
        inter = lf + m0
        m_t = jnp.maximum(inter, i_c)
        return jnp.exp(i_c - m_t), jnp.exp(inter - m_t), m_t

    @pl.when(i == 0)
    def _():
        for h in range(N_HEADS):
            w, g, _ = gate_terms(h)
            g_s[h] = jnp.broadcast_to(g, (nb, LANES))
            wvt_s[h] = (w * head(SV0, h)).T
            kb_s[h] = (head(K0, h) * scale).astype(BF16)
            qb_s[h] = head(Q0, h).astype(BF16)
            cq_s[h] = jnp.zeros((HEAD_DIM, nb), F32)

    lane = lax.broadcasted_iota(jnp.int32, (HEAD_DIM, nb), 1)

    for h in range(N_HEADS):
        Cblk = C_ref[:, h]
        r = _dot_nt(Cblk.reshape(bb * HEAD_DIM, HEAD_DIM).astype(BF16), qb_s[h])
        cq = cq_s[h]
        lhs = []
        for j in range(bb):
            sel = lane == i * bb + j
            cq = jnp.where(sel, r[j * HEAD_DIM:(j + 1) * HEAD_DIM], cq)
            lhs.append(jnp.where(sel, wvt_s[h], 0.0).astype(BF16))
        cq_s[h] = cq
        outer = _dot(jnp.concatenate(lhs, axis=0), kb_s[h])
        for j in range(bb):
            Co_ref[j, h] = (g_s[h, pl.ds(i * bb + j, 1), :] * Cblk[j]
                            + outer[j * HEAD_DIM:(j + 1) * HEAD_DIM])

    @pl.when(i == pl.num_programs(0) - 1)
    def _():
        heads, n_heads = [], []
        lane_m = lax.broadcasted_iota(jnp.int32, (nb, LANES), 1)
        m_out = jnp.zeros((nb, LANES), F32)
        for h in range(N_HEADS):
            cols = slice(h * HEAD_DIM, (h + 1) * HEAD_DIM)
            w, g, m_t = gate_terms(h)
            qf = head(Q0, h)
            kf = head(K0, h) * scale
            vf = head(SV0, h)
            n0 = n_ref[:, cols]
            qk = jnp.sum(qf.astype(BF16).astype(F32) * kf.astype(BF16).astype(F32),
                         axis=-1, keepdims=True)
            s = qk * w
            num = s * vf + g * cq_s[h].T
            den = s + g * jnp.sum(n0 * qf, axis=-1, keepdims=True)
            hc = num * (1.0 / jnp.maximum(jnp.abs(den), jnp.exp(-m_t)))
            heads.append(_head_out(hc, ghead_ref[:, cols], head(SO0, h)))
            n_heads.append(g * n0 + w * kf)
            m_out = jnp.where(lane_m == h, m_t, m_out)
        no_ref[...] = jnp.concatenate(n_heads, axis=-1)
        mo_ref[...] = m_out

        u = proj_ref[:, U0:U0 + POOL_WIDTH]
        pooled = []
        for gi, win in enumerate(POOL_WINDOWS):
            cols = slice(gi * POOL_GROUP, (gi + 1) * POOL_GROUP)
            a = u[:, cols]
            for jrow in range(POOL_BUF - (win - 1), POOL_BUF):
                a = a + buf_ref[jrow, :, cols]
            p = a / float(win) - u[:, cols]
            pooled.append(_dot(p.astype(BF16), wpool_ref[gi]) * pscale_ref[:, cols])
        mix_ref[...] = jnp.concatenate(heads + pooled, axis=-1)
        for jrow in range(POOL_BUF - 1):
            bufo_ref[jrow] = buf_ref[jrow + 1]
        bufo_ref[POOL_BUF - 1] = u


def _sample_step(proj, l, ghead, wpool, pscale, C, n, m, buf_t, C_acc, *, bb):
    nb = proj.shape[0]
    D = MLSTM_WIDTH + POOL_WIDTH
    cblk = pl.BlockSpec((None, bb, N_HEADS, HEAD_DIM, HEAD_DIM), lambda i: (l, i, 0, 0, 0))
    in_specs = [_const_spec(proj.shape)] + [_layer_spec(a, l) for a in (ghead, wpool, pscale)] + [
        cblk, _layer_spec(n, l), _layer_spec(m, l), _layer_spec(buf_t, l),
        pl.BlockSpec(memory_space=pl.ANY)]
    args = [proj, ghead, wpool, pscale, C, n, m, buf_t, C_acc]
    aliases = {len(args) - 1: 1}
    out_shapes = (
        jax.ShapeDtypeStruct((nb, D), F32),
        jax.ShapeDtypeStruct(C.shape, F32),
        jax.ShapeDtypeStruct(n.shape[1:], F32),
        jax.ShapeDtypeStruct((nb, LANES), F32),
        jax.ShapeDtypeStruct(buf_t.shape[1:], F32),
    )
    full = lambda s: pl.BlockSpec(s, lambda i: (0,) * len(s))
    out_specs = (full((nb, D)), cblk, full(n.shape[1:]), full((nb, LANES)), full(buf_t.shape[1:]))
    scratch = [
        pltpu.VMEM((N_HEADS, nb, LANES), F32),
        pltpu.VMEM((N_HEADS, HEAD_DIM, nb), F32),
        pltpu.VMEM((N_HEADS, nb, HEAD_DIM), BF16),
        pltpu.VMEM((N_HEADS, nb, HEAD_DIM), BF16),
        pltpu.VMEM((N_HEADS, HEAD_DIM, nb), F32),
    ]
    est = (4 * bb * N_HEADS * HEAD_DIM * HEAD_DIM * 4 + 4 * _layer_bytes(buf_t)
           + 8 * proj.size * 4)
    return pl.pallas_call(
        functools.partial(_sample_step_kernel, bb=bb),
        grid=(nb // bb,), in_specs=in_specs, out_specs=out_specs, out_shape=out_shapes,
        scratch_shapes=scratch, input_output_aliases=aliases,
        compiler_params=pltpu.CompilerParams(
            dimension_semantics=("arbitrary",), vmem_limit_bytes=_vmem_limit(est)),
        name="sample_step",
    )(*_in_hbm(*args))


def _round_weights_kernel(win_ref, wout_ref, wqk_ref, wvo_ref, woutb_ref):
    wqk_ref[...] = win_ref[:, :2 * MLSTM_WIDTH].astype(BF16)
    wvo_ref[...] = win_ref[:, 2 * MLSTM_WIDTH:4 * MLSTM_WIDTH].astype(BF16)
    woutb_ref[...] = wout_ref[...].astype(BF16)


def _round_weights(w_in, w_out, *, rows):
    depth, d, cols = w_in.shape
    dout = w_out.shape[-1]
    blk = lambda n: pl.BlockSpec((None, rows, n), lambda l, r: (l, r, 0))
    out = jax.ShapeDtypeStruct((depth, d, 2 * MLSTM_WIDTH), BF16)
    return pl.pallas_call(
        _round_weights_kernel,
        grid=(depth, d // rows),
        in_specs=[blk(cols), blk(dout)],
        out_specs=(blk(2 * MLSTM_WIDTH), blk(2 * MLSTM_WIDTH), blk(dout)),
        out_shape=(out, out, jax.ShapeDtypeStruct(w_out.shape, BF16)),
        compiler_params=pltpu.CompilerParams(
            dimension_semantics=("arbitrary", "arbitrary"),
            vmem_limit_bytes=_vmem_limit(6 * rows * (cols + dout) * 4)),
        name="round_weights",
    )(*_in_hbm(w_in, w_out))


def _pack_w_in(w_in, b_gate, w_out):
    depth, d, _ = w_in.shape
    g0 = 4 * MLSTM_WIDTH
    wqk, wvo, wout = _round_weights(w_in, w_out, rows=256)
    tail = w_in[:, :, g0:].astype(BF16)
    wu = tail[:, :, 2 * N_HEADS:]
    pad = jnp.zeros((depth, d, GATE_SLAB - N_HEADS), BF16)
    wgate = jnp.concatenate([tail[:, :, :N_HEADS], pad, tail[:, :, N_HEADS:2 * N_HEADS], pad],
                            axis=-1)
    bpad = jnp.zeros((depth, GATE_SLAB - N_HEADS), b_gate.dtype)
    bslab = jnp.concatenate([b_gate[:, :N_HEADS], bpad, b_gate[:, N_HEADS:], bpad], axis=-1)
    return (wqk, wu, wgate, wvo, bslab[:, None, :]), wout


def kernel(x_prompt, x_sample, state_mlstm_C, state_mlstm_n, state_mlstm_m, state_pool_buf,
           g_mix, w_in, b_gate, g_head, w_pool, pool_scale, w_out, g_ffn, w_gate, w_up,
           w_down, g_final):
    depth = w_in.shape[0]
    B, T, D = x_prompt.shape
    nb = x_sample.shape[0]
    tm_mixer, tm_ffn, bb = 256, 512, 16

    win, wout = _pack_w_in(w_in, b_gate, w_out)
    wpool = w_pool.astype(BF16)
    wg, wu, wd = w_gate, w_up, w_down
    row = lambda a: a[:, None, :]
    gmix, ghead, pscale, gffn = row(g_mix), row(g_head), row(pool_scale), row(g_ffn)
    ghead_rep = jnp.broadcast_to(g_head[:, :, None], g_head.shape + (CHUNK,))
    gfin = g_final[None, :]

    n_in = state_mlstm_n.reshape(depth, nb, MLSTM_WIDTH)
    m_pad = jnp.pad(state_mlstm_m, ((0, 0), (0, 0), (0, LANES - N_HEADS)))
    buf_t = jnp.transpose(state_pool_buf, (0, 2, 1, 3))
    C_sample = jnp.zeros(state_mlstm_C.shape, F32)

    xp = x_prompt
    xs = x_sample.reshape(nb, D)
    Cp, npr, mp, bp, ns, ms, bs = [], [], [], [], [], [], []
    for l in range(depth):
        xm, C1, n1, m1, pb1 = _prompt_mixer(xp, l, gmix, *win, ghead_rep, wpool, pscale, wout,
                                            tm=tm_mixer)
        Cp.append(C1)
        npr.append(n1[:, :, 0, :])
        mp.append(m1[:, :N_HEADS, 0])
        bp.append(pb1[:, POOL_CARRY - POOL_BUF:, :])

        proj = _sample_proj(xs, l, gmix, *win)
        mix, C_sample, n1, m1, buf1 = _sample_step(
            proj, l, ghead, wpool, pscale, state_mlstm_C, n_in, m_pad, buf_t, C_sample, bb=bb)
        ns.append(n1.reshape(nb, N_HEADS, HEAD_DIM))
        ms.append(m1[:, :N_HEADS])
        bs.append(jnp.transpose(buf1, (1, 0, 2)))

        xp, xs = _ffn(xm.reshape(B * T, D), xs, mix, l, gffn, wg, wu, wd, wout, tm=tm_ffn,
                      gfin=gfin if l == depth - 1 else None)
        xp = xp.reshape(B, T, D)
    y_prompt = xp
    y_sample = xs.reshape(nb, 1, D)

    st = lambda xs_: jnp.stack(xs_, 0)
    return (y_prompt, y_sample, st(Cp), st(npr), st(mp), st(bp),
            C_sample, st(ns), st(ms), st(bs))
```

```python
import functools

import jax
import jax.numpy as jnp
from jax import lax
from jax.experimental import pallas as pl
from jax.experimental.pallas import tpu as pltpu

F32 = jnp.float32
BF16 = jnp.bfloat16

EPS = 1e-6
N_HEADS = 4
HEAD_DIM = 128
MLSTM_WIDTH = N_HEADS * HEAD_DIM
POOL_WINDOWS = (2, 4, 8, 16)
POOL_GROUP = 128
POOL_WIDTH = POOL_GROUP * len(POOL_WINDOWS)
POOL_BUF = max(POOL_WINDOWS) - 1
POOL_CARRY = POOL_BUF + 1
CHUNK = 256
LANES = 128
FF_SUBTILE = 256
SUBLANES = 8
N_ROWS = 2 * SUBLANES
GATE_SLAB = LANES

Q0, K0, U0 = 0, MLSTM_WIDTH, 2 * MLSTM_WIDTH
GI0 = U0 + POOL_WIDTH
GF0 = GI0 + GATE_SLAB
TOK_COLS = GF0 + GATE_SLAB
VT0, OT0 = 0, MLSTM_WIDTH
TR_ROWS = 2 * MLSTM_WIDTH
SV0 = TOK_COLS
SO0 = TOK_COLS + MLSTM_WIDTH
SAMPLE_COLS = TOK_COLS + TR_ROWS

V7X_VMEM_BYTES = 64 * 1024 * 1024

_NT = (((1,), (1,)), ((), ()))
_TN = (((0,), (0,)), ((), ()))


def _dot(a, b):
    return jnp.dot(a, b, preferred_element_type=F32)


def _dot_nt(a, b):
    return lax.dot_general(a, b, _NT, preferred_element_type=F32)


def _dot_tn(a, b):
    return lax.dot_general(a, b, _TN, preferred_element_type=F32)


def _rms(x, g):
    return x * lax.rsqrt(jnp.mean(x * x, axis=-1, keepdims=True) + EPS) * g


def _log_sigmoid(x):
    return jnp.minimum(x, 0.0) - jnp.log1p(jnp.exp(-jnp.abs(x)))


def _lane_col(slab, j):
    lane = lax.broadcasted_iota(jnp.int32, slab.shape, 1)
    return jnp.sum(jnp.where(lane == j, slab, 0.0), axis=-1, keepdims=True)


def _split3(x):
    hi = x.astype(BF16)
    r1 = x - hi.astype(F32)
    mid = r1.astype(BF16)
    lo = (r1 - mid.astype(F32)).astype(BF16)
    return hi, mid, lo


def _cumsum_rows(x):
    r = lax.broadcasted_iota(jnp.int32, (CHUNK, CHUNK), 0)
    c = lax.broadcasted_iota(jnp.int32, (CHUNK, CHUNK), 1)
    tril = (r >= c).astype(BF16)
    hi, mid, lo = _split3(x)
    return _dot(tril, hi) + _dot(tril, mid) + _dot(tril, lo)


def _head_stages(h, tok_ref, tr_ref, a_slab, a_t, cs_t, ghead_ref, C_ref, n_ref, m_ref,
                 mask_st, out):
    cols = slice(h * HEAD_DIM, (h + 1) * HEAD_DIM)
    head = lambda c0: slice(c0 + h * HEAD_DIM, c0 + (h + 1) * HEAD_DIM)
    C, n_rows, m = C_ref[h], n_ref[h], m_ref[h:h + 1, 0:1]
    qb = tok_ref[:, head(Q0)].astype(BF16)
    kb = tok_ref[:, head(K0)].astype(BF16)
    by_q = _dot_nt(jnp.concatenate([kb, C.astype(BF16), n_rows.astype(BF16)], axis=0), qb)
    yield
    sT = by_q[:CHUNK]
    cqT = by_q[CHUNK:CHUNK + HEAD_DIM]
    qn = by_q[CHUNK + HEAD_DIM:CHUNK + HEAD_DIM + 1]
    a_row, b_row = a_t[h:h + 1, :], cs_t[h:h + 1, :]
    aT = jnp.where(mask_st, a_slab[:, h:h + 1], -jnp.inf)
    M = jnp.maximum(jnp.max(aT, axis=0, keepdims=True), m)
    swT = sT * jnp.exp(aT - M)
    g = jnp.exp(m - M)
    den = jnp.sum(swT, axis=0, keepdims=True) + g * qn
    inv = 1.0 / jnp.maximum(jnp.abs(den), jnp.exp(-(b_row + M)))
    vT = tr_ref[head(VT0), :]
    hT = (_dot(vT.astype(BF16), swT.astype(BF16)) + g * cqT) * inv
    hnT = hT * lax.rsqrt(jnp.mean(hT * hT, axis=0, keepdims=True) + EPS)
    out[h] = (hnT * ghead_ref[cols, :] * jax.nn.sigmoid(tr_ref[head(OT0), :])).astype(BF16)
    yield
    M_last = M[:, CHUNK - 1:CHUNK]
    g_end = jnp.exp(m - M_last)
    w_end = jnp.exp(a_row - M_last)
    w_rows = jnp.broadcast_to(w_end, (N_ROWS, CHUNK))
    by_k = _dot(jnp.concatenate([(vT * w_end).astype(BF16), w_rows.astype(BF16)], axis=0), kb)
    C_ref[h] = g_end * C + by_k[:HEAD_DIM]
    n_ref[h] = g_end * n_rows + by_k[HEAD_DIM:]
    m_ref[h:h + 1, :] = jnp.broadcast_to(b_row[:, CHUNK - 1:CHUNK] + M_last, (1, LANES))
    yield


def _skewed(streams):
    streams = list(streams)
    live = []
    while streams or live:
        if streams:
            live.append(streams.pop(0))
        for g in list(live):
            try:
                next(g)
            except StopIteration:
                live.remove(g)
        if streams or live:
            yield


def _norm_tile(x_ref, gmix_ref, hn_ref):
    hn_ref[...] = _rms(x_ref[...], gmix_ref[...]).astype(BF16)


def _in_proj_items(hn_ref, wqk_ref, wu_ref, wgate_ref, wtr_ref, bslab_ref, tok_ref, tr_ref,
                   then=None):
    tok_ref[:, GI0:GI0 + 2 * GATE_SLAB] = _dot(hn_ref[...], wgate_ref[...]) + bslab_ref[...]
    yield
    tok_ref[:, Q0:Q0 + MLSTM_WIDTH] = _dot(hn_ref[...], wqk_ref[:, :MLSTM_WIDTH])
    yield
    tok_ref[:, K0:K0 + MLSTM_WIDTH] = (_dot(hn_ref[...], wqk_ref[:, MLSTM_WIDTH:])
                                       * (HEAD_DIM ** -0.5))
    yield
    tok_ref[:, U0:U0 + POOL_WIDTH] = _dot(hn_ref[...], wu_ref[...])
    yield
    for r0 in (VT0, OT0):
        tr_ref[r0:r0 + MLSTM_WIDTH, :] = _dot_nt(wtr_ref[r0:r0 + MLSTM_WIDTH, :], hn_ref[...])
        yield
    if then is not None:
        then()
        yield


def _mix_items(x_ref, tok_ref, tr_ref, t, ghead_ref, wpool_ref, pscale_ref, wout_ref,
               C_ref, n_ref, m_ref, pb_ref, out_ref, tm):
    r = lax.broadcasted_iota(jnp.int32, (CHUNK, CHUNK), 0)
    c = lax.broadcasted_iota(jnp.int32, (CHUNK, CHUNK), 1)
    mask_st = r <= c

    assert tm == CHUNK, "one mLSTM chunk per tile"
    cs = _cumsum_rows(_log_sigmoid(tok_ref[:, GF0:GF0 + GATE_SLAB]))
    a_slab = tok_ref[:, GI0:GI0 + GATE_SLAB] - cs
    cs_t = cs.T
    a_t = a_slab.T
    yield
    heads = [None] * N_HEADS
    yield from _skewed(
        _head_stages(h, tok_ref, tr_ref, a_slab, a_t, cs_t, ghead_ref, C_ref, n_ref, m_ref,
                     mask_st, heads)
        for h in range(N_HEADS))
    yield
    hmT = jnp.concatenate(heads, axis=0)

    u = tok_ref[:, U0:U0 + POOL_WIDTH]
    ext = jnp.concatenate([pb_ref[...], u], axis=0)
    pb_ref[...] = u[tm - POOL_CARRY:, :]
    pos = t * tm + lax.broadcasted_iota(jnp.int32, (tm, POOL_GROUP), 0)
    pooled = []
    for gi, win in enumerate(POOL_WINDOWS):
        cols = slice(gi * POOL_GROUP, (gi + 1) * POOL_GROUP)
        a = ext[:, cols]
        step = 1
        while step < win:
            a = a[step:, :] + a[:a.shape[0] - step, :]
            step *= 2
        a = a[a.shape[0] - tm:, :]
        cnt = jnp.minimum(win, pos + 1).astype(F32)
        p = a / cnt - u[:, cols]
        pooled.append((_dot(p.astype(BF16), wpool_ref[gi]) * pscale_ref[:, cols]).astype(BF16))
    yield
    pm = jnp.concatenate(pooled, axis=-1)
    out_ref[...] = (x_ref[...] + _dot_tn(hmT, wout_ref[:MLSTM_WIDTH, :])
                    + _dot(pm, wout_ref[MLSTM_WIDTH:, :]))
    yield


def _interleave(*streams):
    streams = list(streams)
    while streams:
        for s in list(streams):
            try:
                next(s)
            except StopIteration:
                streams.remove(s)


def _prompt_mixer_kernel(xcur_ref, xnext_ref, gmix_ref, wqk_ref, wu_ref, wgate_ref, wvo_ref,
                         bslab_ref, ghead_ref, wpool_ref, pscale_ref, wout_ref,
                         fg_ref, fu_ref, fd_ref,
                         xo_ref, C_ref, n_ref, m_ref, pb_ref, fgb_ref, fub_ref, fdb_ref,
                         tok_a, tr_a, hn_a, tok_b, tr_b, hn_b, wtr_s, *, tm, nt):
    fgb_ref[...] = fg_ref[...].astype(BF16)
    fub_ref[...] = fu_ref[...].astype(BF16)
    fdb_ref[...] = fd_ref[...].astype(BF16)
    s = pl.program_id(0)
    t0 = lax.rem(2 * s, nt)
    proj_args = (wqk_ref, wu_ref, wgate_ref, wtr_s, bslab_ref)
    mix_args = (ghead_ref, wpool_ref, pscale_ref, wout_ref, C_ref, n_ref, m_ref, pb_ref)

    @pl.when(s == 0)
    def _():
        wtr_s[...] = wvo_ref[...].T
        _norm_tile(xcur_ref.at[0], gmix_ref, hn_a)
        _interleave(_in_proj_items(hn_a, *proj_args, tok_a, tr_a))
        _norm_tile(xcur_ref.at[1], gmix_ref, hn_b)

    @pl.when(t0 == 0)
    def _():
        C_ref[...] = jnp.zeros_like(C_ref)
        n_ref[...] = jnp.zeros_like(n_ref)
        m_ref[...] = jnp.zeros_like(m_ref)
        pb_ref[...] = jnp.zeros_like(pb_ref)

    _interleave(
        _in_proj_items(hn_b, *proj_args, tok_b, tr_b,
                       then=lambda: _norm_tile(xnext_ref.at[0], gmix_ref, hn_a)),
        _mix_items(xcur_ref.at[0], tok_a, tr_a, t0, *mix_args, xo_ref.at[0], tm))
    _interleave(
        _in_proj_items(hn_a, *proj_args, tok_a, tr_a,
                       then=lambda: _norm_tile(xnext_ref.at[1], gmix_ref, hn_b)),
        _mix_items(xcur_ref.at[1], tok_b, tr_b, t0 + 1, *mix_args, xo_ref.at[1], tm))


def _const_spec(shape):
    nd = len(shape)
    return pl.BlockSpec(shape, lambda *_: (0,) * nd, pipeline_mode=pl.Buffered(1))


def _layer_spec(a, l):
    nd = a.ndim - 1
    return pl.BlockSpec((None,) + a.shape[1:], lambda *_: (l,) + (0,) * nd,
                        pipeline_mode=pl.Buffered(1))


def _in_hbm(*arrays):
    return [pltpu.with_memory_space_constraint(a, pltpu.MemorySpace.HBM) for a in arrays]


def _layer_bytes(*arrays):
    return sum(a.size // a.shape[0] * a.dtype.itemsize for a in arrays)


def _vmem_limit(nbytes):
    return int(min(V7X_VMEM_BYTES - (4 << 20), max(nbytes, 16 << 20)))


def _slab_rows(rows, nsteps):
    for steps_per_slab in (1, 2):
        per = rows * steps_per_slab // nsteps
        if per * nsteps == rows * steps_per_slab and per % N_ROWS == 0:
            return per, steps_per_slab
    raise ValueError(f"cannot split {rows} weight rows over {nsteps} grid steps")


def _prompt_mixer(x, l, gmix, wqk, wu, wgate, wvo, bslab, ghead_rep, wpool, pscale, wout,
                  ffn_w, *, tm):
    B, T, D = x.shape
    nt = T // tm
    assert nt % 2 == 0, "a grid step covers two tiles of one sequence"
    ntiles = B * nt
    xt = x.reshape(ntiles, tm, D)
    pair = pl.BlockSpec((2, tm, D), lambda s: (s, 0, 0))
    nxt = pl.BlockSpec((2, tm, D), lambda s: (jnp.minimum(s + 1, ntiles // 2 - 1), 0, 0))
    seq = lambda s: (2 * s) // nt
    out_shapes = (
        jax.ShapeDtypeStruct((ntiles, tm, D), F32),
        jax.ShapeDtypeStruct((B, N_HEADS, HEAD_DIM, HEAD_DIM), F32),
        jax.ShapeDtypeStruct((B, N_HEADS, N_ROWS, HEAD_DIM), F32),
        jax.ShapeDtypeStruct((B, SUBLANES, LANES), F32),
        jax.ShapeDtypeStruct((B, POOL_CARRY, POOL_WIDTH), F32),
    )
    out_specs = (
        pair,
        pl.BlockSpec((None, N_HEADS, HEAD_DIM, HEAD_DIM), lambda s: (seq(s), 0, 0, 0)),
        pl.BlockSpec((None, N_HEADS, N_ROWS, HEAD_DIM), lambda s: (seq(s), 0, 0, 0)),
        pl.BlockSpec((None, SUBLANES, LANES), lambda s: (seq(s), 0, 0)),
        pl.BlockSpec((None, POOL_CARRY, POOL_WIDTH), lambda s: (seq(s), 0, 0)),
    )
    consts = (gmix, wqk, wu, wgate, wvo, bslab, ghead_rep, wpool, pscale, wout)
    in_specs = [pair, nxt] + [_layer_spec(a, l) for a in consts]
    nsteps = ntiles // 2
    for w in ffn_w:
        rows, cols = w.shape[1:]
        per, stride = _slab_rows(rows, nsteps)
        in_specs.append(pl.BlockSpec((None, per, cols), lambda s, k=stride: (l, s // k, 0)))
        out_specs += (pl.BlockSpec((per, cols), lambda s, k=stride: (s // k, 0)),)
        out_shapes += (jax.ShapeDtypeStruct((rows, cols), BF16),)
    est = _layer_bytes(*consts) + 12 * tm * D * 4 + 12 * tm * SAMPLE_COLS * 4 + (4 << 20)
    scratch = [pltpu.VMEM((tm, TOK_COLS), F32), pltpu.VMEM((TR_ROWS, tm), F32),
               pltpu.VMEM((tm, D), BF16)] * 2 + [pltpu.VMEM((TR_ROWS, D), BF16)]
    outs = pl.pallas_call(
        functools.partial(_prompt_mixer_kernel, tm=tm, nt=nt),
        grid=(ntiles // 2,), in_specs=in_specs, out_specs=out_specs, out_shape=out_shapes,
        scratch_shapes=scratch,
        compiler_params=pltpu.CompilerParams(
            dimension_semantics=("arbitrary",),
            vmem_limit_bytes=_vmem_limit(est)),
        name="prompt_mixer",
    )(*_in_hbm(xt, xt, *consts, *ffn_w))
    return (outs[0].reshape(B, T, D),) + tuple(outs[1:])


def _ffn_tile(x, gffn_ref, wg_s, wu_s, wd_s, gfin_ref):
    h2 = _rms(x, gffn_ref[...]).astype(BF16)
    a = _dot(h2, wg_s[...])
    bu = _dot(h2, wu_s[...])
    act = (a * jax.nn.sigmoid(a) * bu).astype(BF16)
    y = x + _dot(act, wd_s[...])
    return y if gfin_ref is None else _rms(y, gfin_ref[...])


def _ffn_kernel(*refs, final, n_tiles):
    refs = list(refs)
    x_ref, xs_ref, mixs_ref, wout_ref, gffn_ref, wg_ref, wu_ref, wd_ref = refs[:8]
    refs = refs[8:]
    gfin_ref = refs.pop(0) if final else None
    out_ref, outs_ref = refs
    i = pl.program_id(0)
    weights = (gffn_ref, wg_ref, wu_ref, wd_ref, gfin_ref)

    @pl.when(i < n_tiles)
    def _():
        for r0 in range(0, x_ref.shape[0], FF_SUBTILE):
            rows = slice(r0, r0 + FF_SUBTILE)
            out_ref[rows, :] = _ffn_tile(x_ref[rows, :], *weights)

    @pl.when(i == n_tiles)
    def _():
        xs = xs_ref[...] + _dot(mixs_ref[...].astype(BF16), wout_ref[...])
        outs_ref[...] = _ffn_tile(xs, *weights)


def _ffn(x, xs, mixs, l, gffn, wg, wu, wd, wout, *, tm, gfin=None):
    M, D = x.shape
    dff = wg.shape[-1]
    n_tiles = M // tm
    assert n_tiles * tm == M
    tile = pl.BlockSpec((tm, D), lambda i: (jnp.minimum(i, n_tiles - 1), 0))
    args = [x, xs, mixs, wout, gffn, wg, wu, wd]
    in_specs = [tile, _const_spec(xs.shape), _const_spec(mixs.shape), _layer_spec(wout, l),
                _layer_spec(gffn, l)] + [_const_spec(w.shape) for w in (wg, wu, wd)]
    if gfin is not None:
        args.append(gfin)
        in_specs.append(_const_spec(gfin.shape))
    est = (3 * D * dff * 2 + _layer_bytes(wout) + 6 * xs.size * 4
           + 6 * tm * D * 4 + 4 * tm * dff * 4)
    return pl.pallas_call(
        functools.partial(_ffn_kernel, final=gfin is not None, n_tiles=n_tiles),
        grid=(n_tiles + 1,), in_specs=in_specs,
        out_specs=(tile, pl.BlockSpec(xs.shape, lambda i: (0, 0))),
        out_shape=(jax.ShapeDtypeStruct((M, D), F32), jax.ShapeDtypeStruct(xs.shape, F32)),
        compiler_params=pltpu.CompilerParams(
            dimension_semantics=("arbitrary",), vmem_limit_bytes=_vmem_limit(est)),
        name="ffn_final" if gfin is not None else "ffn",
    )(*_in_hbm(*args))


def _sample_proj_kernel(x_ref, gmix_ref, wqk_ref, wu_ref, wgate_ref, wvo_ref, bslab_ref, proj_ref):
    hn = _rms(x_ref[...], gmix_ref[...]).astype(BF16)
    proj_ref[:, Q0:U0] = _dot(hn, wqk_ref[...])
    proj_ref[:, U0:GI0] = _dot(hn, wu_ref[...])
    proj_ref[:, GI0:TOK_COLS] = _dot(hn, wgate_ref[...]) + bslab_ref[...]
    proj_ref[:, TOK_COLS:] = _dot(hn, wvo_ref[...])


def _sample_proj(x, l, gmix, wqk, wu, wgate, wvo, bslab):
    M = x.shape[0]
    consts = (gmix, wqk, wu, wgate, wvo, bslab)
    est = 2 * _layer_bytes(*consts) + 8 * M * SAMPLE_COLS * 4
    return pl.pallas_call(
        _sample_proj_kernel,
        grid=(1,),
        in_specs=[_const_spec(x.shape)] + [_layer_spec(a, l) for a in consts],
        out_specs=pl.BlockSpec((M, SAMPLE_COLS), lambda i: (0, 0)),
        out_shape=jax.ShapeDtypeStruct((M, SAMPLE_COLS), F32),
        compiler_params=pltpu.CompilerParams(
            dimension_semantics=("arbitrary",), vmem_limit_bytes=_vmem_limit(est)),
        name="sample_proj",
    )(*_in_hbm(x, *consts))


def _head_out(h, g_head, o):
    hn = h * lax.rsqrt(jnp.mean(h * h, axis=-1, keepdims=True) + EPS)
    return hn * g_head * jax.nn.sigmoid(o)


def _sample_step_kernel(proj_ref, ghead_ref, wpool_ref, pscale_ref, C_ref, n_ref, m_ref, buf_ref,
                        mix_ref, no_ref, mo_ref, bufo_ref, g_ref, wvt_ref, kb_ref,
                        qb_s, cq_s, *, bb):
    i = pl.program_id(0)
    nb = proj_ref.shape[0]
    scale = HEAD_DIM ** -0.5
    head = lambda c0, h: proj_ref[:, c0 + h * HEAD_DIM:c0 + (h + 1) * HEAD_DIM]

    def gate_terms(h):
        i_c = _lane_col(proj_ref[:, GI0:GI0 + GATE_SLAB], h)
        lf = _lane_col(_log_sigmoid(proj_ref[:, GF0:GF0 + GATE_SLAB]), h)
        m0 = _lane_col(m_ref[...], h)
        inter = lf + m0
        m_t = jnp.maximum(inter, i_c)
        return jnp.exp(i_c - m_t), jnp.exp(inter - m_t), m_t

    @pl.when(i == 0)
    def _():
        for h in range(N_HEADS):
            w, g, _ = gate_terms(h)
            g_ref[h] = jnp.broadcast_to(g, (nb, LANES))
            wvt_ref[h] = (w * head(SV0, h)).T
            kb_ref[h] = (head(K0, h) * scale).astype(BF16)
            qb_s[h] = head(Q0, h).astype(BF16)
            cq_s[h] = jnp.zeros((HEAD_DIM, nb), F32)

    lane = lax.broadcasted_iota(jnp.int32, (HEAD_DIM, nb), 1)

    for h in range(N_HEADS):
        r = _dot_nt(C_ref[:, h].reshape(bb * HEAD_DIM, HEAD_DIM).astype(BF16), qb_s[h])
        cq = cq_s[h]
        for j in range(bb):
            cq = jnp.where(lane == i * bb + j, r[j * HEAD_DIM:(j + 1) * HEAD_DIM], cq)
        cq_s[h] = cq

    @pl.when(i == pl.num_programs(0) - 1)
    def _():
        heads, n_heads = [], []
        lane_m = lax.broadcasted_iota(jnp.int32, (nb, LANES), 1)
        m_out = jnp.zeros((nb, LANES), F32)
        for h in range(N_HEADS):
            cols = slice(h * HEAD_DIM, (h + 1) * HEAD_DIM)
            w, g, m_t = gate_terms(h)
            qf = head(Q0, h)
            kf = head(K0, h) * scale
            vf = head(SV0, h)
            n0 = n_ref[:, cols]
            qk = jnp.sum(qf.astype(BF16).astype(F32) * kf.astype(BF16).astype(F32),
                         axis=-1, keepdims=True)
            s = qk * w
            num = s * vf + g * cq_s[h].T
            den = s + g * jnp.sum(n0 * qf, axis=-1, keepdims=True)
            hc = num * (1.0 / jnp.maximum(jnp.abs(den), jnp.exp(-m_t)))
            heads.append(_head_out(hc, ghead_ref[:, cols], head(SO0, h)))
            n_heads.append(g * n0 + w * kf)
            m_out = jnp.where(lane_m == h, m_t, m_out)
        no_ref[...] = jnp.concatenate(n_heads, axis=-1)
        mo_ref[...] = m_out

        u = proj_ref[:, U0:U0 + POOL_WIDTH]
        pooled = []
        for gi, win in enumerate(POOL_WINDOWS):
            cols = slice(gi * POOL_GROUP, (gi + 1) * POOL_GROUP)
            a = u[:, cols]
            for jrow in range(POOL_BUF - (win - 1), POOL_BUF):
                a = a + buf_ref[jrow, :, cols]
            p = a / float(win) - u[:, cols]
            pooled.append(_dot(p.astype(BF16), wpool_ref[gi]) * pscale_ref[:, cols])
        mix_ref[...] = jnp.concatenate(heads + pooled, axis=-1)
        for jrow in range(POOL_BUF - 1):
            bufo_ref[jrow] = buf_ref[jrow + 1]
        bufo_ref[POOL_BUF - 1] = u


def _sample_step(proj, l, ghead, wpool, pscale, C, n, m, buf_t, *, bb):
    nb = proj.shape[0]
    D = MLSTM_WIDTH + POOL_WIDTH
    cblk = pl.BlockSpec((None, bb, N_HEADS, HEAD_DIM, HEAD_DIM), lambda i: (l, i, 0, 0, 0))
    in_specs = [_const_spec(proj.shape)] + [_layer_spec(a, l) for a in (ghead, wpool, pscale)] + [
        cblk, _layer_spec(n, l), _layer_spec(m, l), _layer_spec(buf_t, l)]
    args = [proj, ghead, wpool, pscale, C, n, m, buf_t]
    shapes = ((nb, D), n.shape[1:], (nb, LANES), buf_t.shape[1:],
              (N_HEADS, nb, LANES), (N_HEADS, HEAD_DIM, nb), (N_HEADS, nb, HEAD_DIM))
    dtypes = (F32,) * 6 + (BF16,)
    out_shapes = tuple(jax.ShapeDtypeStruct(s, d) for s, d in zip(shapes, dtypes))
    out_specs = tuple(pl.BlockSpec(s, lambda i, k=len(s): (0,) * k) for s in shapes)
    scratch = [
        pltpu.VMEM((N_HEADS, nb, HEAD_DIM), BF16),
        pltpu.VMEM((N_HEADS, HEAD_DIM, nb), F32),
    ]
    est = (3 * bb * N_HEADS * HEAD_DIM * HEAD_DIM * 4 + 4 * _layer_bytes(buf_t)
           + 8 * proj.size * 4)
    return pl.pallas_call(
        functools.partial(_sample_step_kernel, bb=bb),
        grid=(nb // bb,), in_specs=in_specs, out_specs=out_specs, out_shape=out_shapes,
        scratch_shapes=scratch,
        compiler_params=pltpu.CompilerParams(
            dimension_semantics=("arbitrary",), vmem_limit_bytes=_vmem_limit(est)),
        name="sample_step",
    )(*_in_hbm(*args))


def _state_update_kernel(C_ref, g_ref, wvt_ref, kb_ref, Co_ref, *, bb):
    i = pl.program_id(1)
    nb = kb_ref.shape[1]
    lane = lax.broadcasted_iota(jnp.int32, (HEAD_DIM, nb), 1)
    for h in range(N_HEADS):
        lhs = [jnp.where(lane == i * bb + j, wvt_ref[h], 0.0).astype(BF16) for j in range(bb)]
        outer = _dot(jnp.concatenate(lhs, axis=0), kb_ref[h])
        for j in range(bb):
            Co_ref[j, h] = (g_ref[h, pl.ds(i * bb + j, 1), :] * C_ref[j, h]
                            + outer[j * HEAD_DIM:(j + 1) * HEAD_DIM])


def _state_update(C, g, wvt, kb, *, bb):
    depth, nb = C.shape[:2]
    cblk = pl.BlockSpec((None, bb, N_HEADS, HEAD_DIM, HEAD_DIM), lambda l, i: (l, i, 0, 0, 0))
    per_layer = lambda a: pl.BlockSpec((None,) + a.shape[1:], lambda l, i: (l, 0, 0, 0))
    est = 5 * bb * N_HEADS * HEAD_DIM * HEAD_DIM * 4 + 2 * _layer_bytes(g, wvt, kb)
    return pl.pallas_call(
        functools.partial(_state_update_kernel, bb=bb),
        grid=(depth, nb // bb),
        in_specs=[cblk, per_layer(g), per_layer(wvt), per_layer(kb)],
        out_specs=cblk, out_shape=jax.ShapeDtypeStruct(C.shape, F32),
        compiler_params=pltpu.CompilerParams(
            dimension_semantics=("arbitrary", "arbitrary"), vmem_limit_bytes=_vmem_limit(est)),
        name="state_update",
    )(*_in_hbm(C, g, wvt, kb))


def _pack_w_in(w_in, b_gate):
    depth, d, _ = w_in.shape
    w = MLSTM_WIDTH
    g0 = 4 * w
    wqk = w_in[:, :, 0:2 * w].astype(BF16)
    wu = w_in[:, :, g0 + 2 * N_HEADS:].astype(BF16)
    pad = jnp.zeros((depth, d, GATE_SLAB - N_HEADS), BF16)
    gates = w_in[:, :, g0:g0 + 2 * N_HEADS].astype(BF16)
    wgate = jnp.concatenate([gates[:, :, :N_HEADS], pad, gates[:, :, N_HEADS:], pad], axis=-1)
    wvo = w_in[:, :, 2 * w:4 * w].astype(BF16)
    bpad = jnp.zeros((depth, GATE_SLAB - N_HEADS), b_gate.dtype)
    bslab = jnp.concatenate([b_gate[:, :N_HEADS], bpad, b_gate[:, N_HEADS:], bpad], axis=-1)
    return wqk, wu, wgate, wvo, bslab[:, None, :]


def kernel(x_prompt, x_sample, state_mlstm_C, state_mlstm_n, state_mlstm_m, state_pool_buf,
           g_mix, w_in, b_gate, g_head, w_pool, pool_scale, w_out, g_ffn, w_gate, w_up,
           w_down, g_final):
    depth = w_in.shape[0]
    B, T, D = x_prompt.shape
    nb = x_sample.shape[0]
    tm_mixer, tm_ffn, bb = 256, 1024, 16

    win = _pack_w_in(w_in, b_gate)
    wpool = w_pool.astype(BF16)
    wout = w_out.astype(BF16)
    ffn_w = (w_gate, w_up, w_down)
    row = lambda a: a[:, None, :]
    gmix, ghead, pscale, gffn = row(g_mix), row(g_head), row(pool_scale), row(g_ffn)
    ghead_rep = jnp.broadcast_to(g_head[:, :, None], g_head.shape + (CHUNK,))
    gfin = g_final[None, :]

    n_in = state_mlstm_n.reshape(depth, nb, MLSTM_WIDTH)
    m_pad = jnp.pad(state_mlstm_m, ((0, 0), (0, 0), (0, LANES - N_HEADS)))
    buf_t = jnp.transpose(state_pool_buf, (0, 2, 1, 3))

    xp = x_prompt
    xs = x_sample.reshape(nb, D)
    Cp, npr, mp, bp, ns, ms, bs, upd = [], [], [], [], [], [], [], []
    for l in range(depth):
        xm, C1, n1, m1, pb1, wg, wu, wd = _prompt_mixer(
            xp, l, gmix, *win, ghead_rep, wpool, pscale, wout, ffn_w, tm=tm_mixer)
        Cp.append(C1)
        npr.append(n1[:, :, 0, :])
        mp.append(m1[:, :N_HEADS, 0])
        bp.append(pb1[:, POOL_CARRY - POOL_BUF:, :])

        proj = _sample_proj(xs, l, gmix, *win)
        mix, n1, m1, buf1, *terms = _sample_step(
            proj, l, ghead, wpool, pscale, state_mlstm_C, n_in, m_pad, buf_t, bb=bb)
        ns.append(n1.reshape(nb, N_HEADS, HEAD_DIM))
        ms.append(m1[:, :N_HEADS])
        bs.append(jnp.transpose(buf1, (1, 0, 2)))
        upd.append(terms)

        xp, xs = _ffn(xm.reshape(B * T, D), xs, mix, l, gffn, wg, wu, wd, wout, tm=tm_ffn,
                      gfin=gfin if l == depth - 1 else None)
        xp = xp.reshape(B, T, D)
    y_prompt = xp
    y_sample = xs.reshape(nb, 1, D)
    C_sample = _state_update(state_mlstm_C, *(jnp.stack(t, 0) for t in zip(*upd)), bb=bb)

    st = lambda xs_: jnp.stack(xs_, 0)
    return (y_prompt, y_sample, st(Cp), st(npr), st(mp), st(bp),
            C_sample, st(ns), st(ms), st(bs))
```

```python
import functools

import jax
import jax.numpy as jnp
from jax import lax
from jax.experimental import pallas as pl
from jax.experimental.pallas import tpu as pltpu

F32 = jnp.float32
BF16 = jnp.bfloat16

EPS = 1e-6
N_HEADS = 4
HEAD_DIM = 128
MLSTM_WIDTH = N_HEADS * HEAD_DIM
POOL_WINDOWS = (2, 4, 8, 16)
POOL_GROUP = 128
POOL_WIDTH = POOL_GROUP * len(POOL_WINDOWS)
POOL_BUF = max(POOL_WINDOWS) - 1
POOL_CARRY = POOL_BUF + 1
CHUNK = 256
LANES = 128
FF_SUBTILE = 256
SUBLANES = 8
N_ROWS = 2 * SUBLANES
GATE_SLAB = LANES

Q0, K0, U0 = 0, MLSTM_WIDTH, 2 * MLSTM_WIDTH
GI0 = U0 + POOL_WIDTH
GF0 = GI0 + GATE_SLAB
TOK_COLS = GF0 + GATE_SLAB
VT0, OT0 = 0, MLSTM_WIDTH
TR_ROWS = 2 * MLSTM_WIDTH
SV0 = TOK_COLS
SO0 = TOK_COLS + MLSTM_WIDTH
SAMPLE_COLS = TOK_COLS + TR_ROWS

V7X_VMEM_BYTES = 64 * 1024 * 1024

_NT = (((1,), (1,)), ((), ()))
_TN = (((0,), (0,)), ((), ()))


def _dot(a, b):
    return jnp.dot(a, b, preferred_element_type=F32)


def _dot_nt(a, b):
    return lax.dot_general(a, b, _NT, preferred_element_type=F32)


def _dot_tn(a, b):
    return lax.dot_general(a, b, _TN, preferred_element_type=F32)


def _rms(x, g):
    return x * lax.rsqrt(jnp.mean(x * x, axis=-1, keepdims=True) + EPS) * g


def _log_sigmoid(x):
    return jnp.minimum(x, 0.0) - jnp.log1p(jnp.exp(-jnp.abs(x)))


def _lane_col(slab, j):
    lane = lax.broadcasted_iota(jnp.int32, slab.shape, 1)
    return jnp.sum(jnp.where(lane == j, slab, 0.0), axis=-1, keepdims=True)


def _split3(x):
    hi = x.astype(BF16)
    r1 = x - hi.astype(F32)
    mid = r1.astype(BF16)
    lo = (r1 - mid.astype(F32)).astype(BF16)
    return hi, mid, lo


def _cumsum_rows(x):
    r = lax.broadcasted_iota(jnp.int32, (CHUNK, CHUNK), 0)
    c = lax.broadcasted_iota(jnp.int32, (CHUNK, CHUNK), 1)
    tril = (r >= c).astype(BF16)
    hi, mid, lo = _split3(x)
    return _dot(tril, hi) + _dot(tril, mid) + _dot(tril, lo)


def _head_stages(h, tok_ref, tr_ref, a_slab, a_t, cs_t, ghead_ref, C_ref, n_ref, m_ref,
                 mask_st, out):
    cols = slice(h * HEAD_DIM, (h + 1) * HEAD_DIM)
    head = lambda c0: slice(c0 + h * HEAD_DIM, c0 + (h + 1) * HEAD_DIM)
    C, n_rows, m = C_ref[h], n_ref[h], m_ref[h:h + 1, 0:1]
    qb = tok_ref[:, head(Q0)].astype(BF16)
    kb = tok_ref[:, head(K0)].astype(BF16)
    by_q = _dot_nt(jnp.concatenate([kb, C.astype(BF16), n_rows.astype(BF16)], axis=0), qb)
    yield
    sT = by_q[:CHUNK]
    cqT = by_q[CHUNK:CHUNK + HEAD_DIM]
    qn = by_q[CHUNK + HEAD_DIM:CHUNK + HEAD_DIM + 1]
    a_row, b_row = a_t[h:h + 1, :], cs_t[h:h + 1, :]
    aT = jnp.where(mask_st, a_slab[:, h:h + 1], -jnp.inf)
    M = jnp.maximum(jnp.max(aT, axis=0, keepdims=True), m)
    swT = sT * jnp.exp(aT - M)
    g = jnp.exp(m - M)
    den = jnp.sum(swT, axis=0, keepdims=True) + g * qn
    inv = 1.0 / jnp.maximum(jnp.abs(den), jnp.exp(-(b_row + M)))
    vT = tr_ref[head(VT0), :]
    hT = (_dot(vT.astype(BF16), swT.astype(BF16)) + g * cqT) * inv
    hnT = hT * lax.rsqrt(jnp.mean(hT * hT, axis=0, keepdims=True) + EPS)
    out[h] = (hnT * ghead_ref[cols, :] * jax.nn.sigmoid(tr_ref[head(OT0), :])).astype(BF16)
    yield
    M_last = M[:, CHUNK - 1:CHUNK]
    g_end = jnp.exp(m - M_last)
    w_end = jnp.exp(a_row - M_last)
    w_rows = jnp.broadcast_to(w_end, (N_ROWS, CHUNK))
    by_k = _dot(jnp.concatenate([(vT * w_end).astype(BF16), w_rows.astype(BF16)], axis=0), kb)
    C_ref[h] = g_end * C + by_k[:HEAD_DIM]
    n_ref[h] = g_end * n_rows + by_k[HEAD_DIM:]
    m_ref[h:h + 1, :] = jnp.broadcast_to(b_row[:, CHUNK - 1:CHUNK] + M_last, (1, LANES))
    yield


def _skewed(streams):
    streams = list(streams)
    live = []
    while streams or live:
        if streams:
            live.append(streams.pop(0))
        for g in list(live):
            try:
                next(g)
            except StopIteration:
                live.remove(g)
        if streams or live:
            yield


def _norm_tile(x_ref, gmix_ref, hn_ref):
    hn_ref[...] = _rms(x_ref[...], gmix_ref[...]).astype(BF16)


def _in_proj_items(hn_ref, wqk_ref, wu_ref, wgate_ref, wtr_ref, bslab_ref, tok_ref, tr_ref,
                   then=None):
    tok_ref[:, GI0:GI0 + 2 * GATE_SLAB] = _dot(hn_ref[...], wgate_ref[...]) + bslab_ref[...]
    yield
    tok_ref[:, Q0:Q0 + MLSTM_WIDTH] = _dot(hn_ref[...], wqk_ref[:, :MLSTM_WIDTH])
    yield
    tok_ref[:, K0:K0 + MLSTM_WIDTH] = (_dot(hn_ref[...], wqk_ref[:, MLSTM_WIDTH:])
                                       * (HEAD_DIM ** -0.5))
    yield
    tok_ref[:, U0:U0 + POOL_WIDTH] = _dot(hn_ref[...], wu_ref[...])
    yield
    for r0 in (VT0, OT0):
        tr_ref[r0:r0 + MLSTM_WIDTH, :] = _dot_nt(wtr_ref[r0:r0 + MLSTM_WIDTH, :], hn_ref[...])
        yield
    if then is not None:
        then()
        yield


def _mix_items(x_ref, tok_ref, tr_ref, t, ghead_ref, wpool_ref, pscale_ref, wout_ref,
               C_ref, n_ref, m_ref, pb_ref, out_ref, tm):
    r = lax.broadcasted_iota(jnp.int32, (CHUNK, CHUNK), 0)
    c = lax.broadcasted_iota(jnp.int32, (CHUNK, CHUNK), 1)
    mask_st = r <= c

    assert tm == CHUNK, "one mLSTM chunk per tile"
    cs = _cumsum_rows(_log_sigmoid(tok_ref[:, GF0:GF0 + GATE_SLAB]))
    a_slab = tok_ref[:, GI0:GI0 + GATE_SLAB] - cs
    cs_t = cs.T
    a_t = a_slab.T
    yield
    heads = [None] * N_HEADS
    yield from _skewed(
        _head_stages(h, tok_ref, tr_ref, a_slab, a_t, cs_t, ghead_ref, C_ref, n_ref, m_ref,
                     mask_st, heads)
        for h in range(N_HEADS))
    yield
    hmT = jnp.concatenate(heads, axis=0)

    u = tok_ref[:, U0:U0 + POOL_WIDTH]
    ext = jnp.concatenate([pb_ref[...], u], axis=0)
    pb_ref[...] = u[tm - POOL_CARRY:, :]
    pos = t * tm + lax.broadcasted_iota(jnp.int32, (tm, POOL_GROUP), 0)
    pooled = []
    for gi, win in enumerate(POOL_WINDOWS):
        cols = slice(gi * POOL_GROUP, (gi + 1) * POOL_GROUP)
        a = ext[:, cols]
        step = 1
        while step < win:
            a = a[step:, :] + a[:a.shape[0] - step, :]
            step *= 2
        a = a[a.shape[0] - tm:, :]
        cnt = jnp.minimum(win, pos + 1).astype(F32)
        p = a / cnt - u[:, cols]
        pooled.append((_dot(p.astype(BF16), wpool_ref[gi]) * pscale_ref[:, cols]).astype(BF16))
    yield
    pm = jnp.concatenate(pooled, axis=-1)
    out_ref[...] = (x_ref[...] + _dot_tn(hmT, wout_ref[:MLSTM_WIDTH, :])
                    + _dot(pm, wout_ref[MLSTM_WIDTH:, :]))
    yield


def _interleave(*streams):
    streams = list(streams)
    while streams:
        for s in list(streams):
            try:
                next(s)
            except StopIteration:
                streams.remove(s)


def _prompt_mixer_kernel(xcur_ref, xnext_ref, gmix_ref, wqk_ref, wu_ref, wgate_ref, wvo_ref,
                         bslab_ref, ghead_ref, wpool_ref, pscale_ref, wout_ref,
                         fg_ref, fu_ref, fd_ref,
                         xo_ref, C_ref, n_ref, m_ref, pb_ref, fgb_ref, fub_ref, fdb_ref,
                         tok_a, tr_a, hn_a, tok_b, tr_b, hn_b, wtr_s, *, tm, nt, tiles):
    fgb_ref[...] = fg_ref[...].astype(BF16)
    fub_ref[...] = fu_ref[...].astype(BF16)
    fdb_ref[...] = fd_ref[...].astype(BF16)
    s = pl.program_id(0)
    t0 = lax.rem(tiles * s, nt)
    proj_args = (wqk_ref, wu_ref, wgate_ref, wtr_s, bslab_ref)
    mix_args = (ghead_ref, wpool_ref, pscale_ref, wout_ref, C_ref, n_ref, m_ref, pb_ref)
    slots = ((tok_a, tr_a, hn_a), (tok_b, tr_b, hn_b))
    x_tile = lambda k: xcur_ref.at[k] if k < tiles else xnext_ref.at[k - tiles]

    @pl.when(s == 0)
    def _():
        wtr_s[...] = wvo_ref[...].T
        _norm_tile(x_tile(0), gmix_ref, hn_a)
        _interleave(_in_proj_items(hn_a, *proj_args, tok_a, tr_a))
        _norm_tile(x_tile(1), gmix_ref, hn_b)

    @pl.when(t0 == 0)
    def _():
        C_ref[...] = jnp.zeros_like(C_ref)
        n_ref[...] = jnp.zeros_like(n_ref)
        m_ref[...] = jnp.zeros_like(m_ref)
        pb_ref[...] = jnp.zeros_like(pb_ref)

    for k in range(tiles):
        tok, tr, hn = slots[k % 2]
        tok_n, tr_n, hn_n = slots[(k + 1) % 2]
        _interleave(
            _in_proj_items(hn_n, *proj_args, tok_n, tr_n,
                           then=functools.partial(_norm_tile, x_tile(k + 2), gmix_ref, hn)),
            _mix_items(x_tile(k), tok, tr, t0 + k, *mix_args, xo_ref.at[k], tm))


def _const_spec(shape):
    nd = len(shape)
    return pl.BlockSpec(shape, lambda *_: (0,) * nd, pipeline_mode=pl.Buffered(1))


def _layer_spec(a, l):
    nd = a.ndim - 1
    return pl.BlockSpec((None,) + a.shape[1:], lambda *_: (l,) + (0,) * nd,
                        pipeline_mode=pl.Buffered(1))


def _in_hbm(*arrays):
    return [pltpu.with_memory_space_constraint(a, pltpu.MemorySpace.HBM) for a in arrays]


def _layer_bytes(*arrays):
    return sum(a.size // a.shape[0] * a.dtype.itemsize for a in arrays)


def _vmem_limit(nbytes):
    return int(min(V7X_VMEM_BYTES - (4 << 20), max(nbytes, 16 << 20)))


def _slab_rows(rows, nsteps):
    for steps_per_slab in (1, 2):
        per = rows * steps_per_slab // nsteps
        if per * nsteps == rows * steps_per_slab and per % N_ROWS == 0:
            return per, steps_per_slab
    raise ValueError(f"cannot split {rows} weight rows over {nsteps} grid steps")


def _prompt_mixer(x, l, gmix, wqk, wu, wgate, wvo, bslab, ghead_rep, wpool, pscale, wout,
                  ffn_w, *, tm, tiles):
    B, T, D = x.shape
    nt = T // tm
    assert tiles % 2 == 0 and nt % tiles == 0, "a grid step covers an even number of tiles of one sequence"
    ntiles = B * nt
    nsteps = ntiles // tiles
    xt = x.reshape(ntiles, tm, D)
    pair = pl.BlockSpec((tiles, tm, D), lambda s: (s, 0, 0))
    nxt = pl.BlockSpec((tiles, tm, D), lambda s: (jnp.minimum(s + 1, nsteps - 1), 0, 0))
    seq = lambda s: (tiles * s) // nt
    out_shapes = (
        jax.ShapeDtypeStruct((ntiles, tm, D), F32),
        jax.ShapeDtypeStruct((B, N_HEADS, HEAD_DIM, HEAD_DIM), F32),
        jax.ShapeDtypeStruct((B, N_HEADS, N_ROWS, HEAD_DIM), F32),
        jax.ShapeDtypeStruct((B, SUBLANES, LANES), F32),
        jax.ShapeDtypeStruct((B, POOL_CARRY, POOL_WIDTH), F32),
    )
    out_specs = (
        pair,
        pl.BlockSpec((None, N_HEADS, HEAD_DIM, HEAD_DIM), lambda s: (seq(s), 0, 0, 0)),
        pl.BlockSpec((None, N_HEADS, N_ROWS, HEAD_DIM), lambda s: (seq(s), 0, 0, 0)),
        pl.BlockSpec((None, SUBLANES, LANES), lambda s: (seq(s), 0, 0)),
        pl.BlockSpec((None, POOL_CARRY, POOL_WIDTH), lambda s: (seq(s), 0, 0)),
    )
    consts = (gmix, wqk, wu, wgate, wvo, bslab, ghead_rep, wpool, pscale, wout)
    in_specs = [pair, nxt] + [_layer_spec(a, l) for a in consts]
    for w in ffn_w:
        rows, cols = w.shape[1:]
        per, stride = _slab_rows(rows, nsteps)
        in_specs.append(pl.BlockSpec((None, per, cols), lambda s, k=stride: (l, s // k, 0)))
        out_specs += (pl.BlockSpec((per, cols), lambda s, k=stride: (s // k, 0)),)
        out_shapes += (jax.ShapeDtypeStruct((rows, cols), BF16),)
    est = _layer_bytes(*consts) + 6 * tiles * tm * D * 4 + 12 * tm * SAMPLE_COLS * 4 + (4 << 20)
    scratch = [pltpu.VMEM((tm, TOK_COLS), F32), pltpu.VMEM((TR_ROWS, tm), F32),
               pltpu.VMEM((tm, D), BF16)] * 2 + [pltpu.VMEM((TR_ROWS, D), BF16)]
    outs = pl.pallas_call(
        functools.partial(_prompt_mixer_kernel, tm=tm, nt=nt, tiles=tiles),
        grid=(nsteps,), in_specs=in_specs, out_specs=out_specs, out_shape=out_shapes,
        scratch_shapes=scratch,
        compiler_params=pltpu.CompilerParams(
            dimension_semantics=("arbitrary",),
            vmem_limit_bytes=_vmem_limit(est)),
        name="prompt_mixer",
    )(*_in_hbm(xt, xt, *consts, *ffn_w))
    return (outs[0].reshape(B, T, D),) + tuple(outs[1:])


def _ffn_tile(x, gffn_ref, wg_s, wu_s, wd_s, gfin_ref):
    h2 = _rms(x, gffn_ref[...]).astype(BF16)
    a = _dot(h2, wg_s[...])
    bu = _dot(h2, wu_s[...])
    act = (a * jax.nn.sigmoid(a) * bu).astype(BF16)
    y = x + _dot(act, wd_s[...])
    return y if gfin_ref is None else _rms(y, gfin_ref[...])


def _ffn_kernel(*refs, final, n_tiles):
    refs = list(refs)
    x_ref, xs_ref, mixs_ref, wout_ref, gffn_ref, wg_ref, wu_ref, wd_ref = refs[:8]
    refs = refs[8:]
    gfin_ref = refs.pop(0) if final else None
    out_ref, outs_ref = refs
    i = pl.program_id(0)
    weights = (gffn_ref, wg_ref, wu_ref, wd_ref, gfin_ref)

    @pl.when(i < n_tiles)
    def _():
        for r0 in range(0, x_ref.shape[0], FF_SUBTILE):
            rows = slice(r0, r0 + FF_SUBTILE)
            out_ref[rows, :] = _ffn_tile(x_ref[rows, :], *weights)

    @pl.when(i == n_tiles)
    def _():
        xs = xs_ref[...] + _dot(mixs_ref[...].astype(BF16), wout_ref[...])
        outs_ref[...] = _ffn_tile(xs, *weights)


def _ffn(x, xs, mixs, l, gffn, wg, wu, wd, wout, *, tm, gfin=None):
    M, D = x.shape
    dff = wg.shape[-1]
    n_tiles = M // tm
    assert n_tiles * tm == M
    tile = pl.BlockSpec((tm, D), lambda i: (jnp.minimum(i, n_tiles - 1), 0))
    args = [x, xs, mixs, wout, gffn, wg, wu, wd]
    in_specs = [tile, _const_spec(xs.shape), _const_spec(mixs.shape), _layer_spec(wout, l),
                _layer_spec(gffn, l)] + [_const_spec(w.shape) for w in (wg, wu, wd)]
    if gfin is not None:
        args.append(gfin)
        in_specs.append(_const_spec(gfin.shape))
    est = (3 * D * dff * 2 + _layer_bytes(wout) + 6 * xs.size * 4
           + 6 * tm * D * 4 + 4 * tm * dff * 4)
    return pl.pallas_call(
        functools.partial(_ffn_kernel, final=gfin is not None, n_tiles=n_tiles),
        grid=(n_tiles + 1,), in_specs=in_specs,
        out_specs=(tile, pl.BlockSpec(xs.shape, lambda i: (0, 0))),
        out_shape=(jax.ShapeDtypeStruct((M, D), F32), jax.ShapeDtypeStruct(xs.shape, F32)),
        compiler_params=pltpu.CompilerParams(
            dimension_semantics=("arbitrary",), vmem_limit_bytes=_vmem_limit(est)),
        name="ffn_final" if gfin is not None else "ffn",
    )(*_in_hbm(*args))


def _sample_proj_kernel(x_ref, gmix_ref, wqk_ref, wu_ref, wgate_ref, wvo_ref, bslab_ref, proj_ref):
    hn = _rms(x_ref[...], gmix_ref[...]).astype(BF16)
    proj_ref[:, Q0:U0] = _dot(hn, wqk_ref[...])
    proj_ref[:, U0:GI0] = _dot(hn, wu_ref[...])
    proj_ref[:, GI0:TOK_COLS] = _dot(hn, wgate_ref[...]) + bslab_ref[...]
    proj_ref[:, TOK_COLS:] = _dot(hn, wvo_ref[...])


def _sample_proj(x, l, gmix, wqk, wu, wgate, wvo, bslab):
    M = x.shape[0]
    consts = (gmix, wqk, wu, wgate, wvo, bslab)
    est = 2 * _layer_bytes(*consts) + 8 * M * SAMPLE_COLS * 4
    return pl.pallas_call(
        _sample_proj_kernel,
        grid=(1,),
        in_specs=[_const_spec(x.shape)] + [_layer_spec(a, l) for a in consts],
        out_specs=pl.BlockSpec((M, SAMPLE_COLS), lambda i: (0, 0)),
        out_shape=jax.ShapeDtypeStruct((M, SAMPLE_COLS), F32),
        compiler_params=pltpu.CompilerParams(
            dimension_semantics=("arbitrary",), vmem_limit_bytes=_vmem_limit(est)),
        name="sample_proj",
    )(*_in_hbm(x, *consts))


def _head_out(h, g_head, o):
    hn = h * lax.rsqrt(jnp.mean(h * h, axis=-1, keepdims=True) + EPS)
    return hn * g_head * jax.nn.sigmoid(o)


def _sample_step_kernel(proj_ref, ghead_ref, wpool_ref, pscale_ref, C_ref, n_ref, m_ref, buf_ref,
                        mix_ref, no_ref, mo_ref, bufo_ref, g_ref, wvt_ref, kb_ref,
                        qb_s, cq_s, *, bb):
    i = pl.program_id(0)
    nb = proj_ref.shape[0]
    scale = HEAD_DIM ** -0.5
    head = lambda c0, h: proj_ref[:, c0 + h * HEAD_DIM:c0 + (h + 1) * HEAD_DIM]

    def gate_terms(h):
        i_c = _lane_col(proj_ref[:, GI0:GI0 + GATE_SLAB], h)
        lf = _lane_col(_log_sigmoid(proj_ref[:, GF0:GF0 + GATE_SLAB]), h)
        m0 = _lane_col(m_ref[...], h)
        inter = lf + m0
        m_t = jnp.maximum(inter, i_c)
        return jnp.exp(i_c - m_t), jnp.exp(inter - m_t), m_t

    @pl.when(i == 0)
    def _():
        for h in range(N_HEADS):
            w, g, _ = gate_terms(h)
            g_ref[h] = jnp.broadcast_to(g, (nb, LANES))
            wvt_ref[h] = (w * head(SV0, h)).T
            kb_ref[h] = (head(K0, h) * scale).astype(BF16)
            qb_s[h] = head(Q0, h).astype(BF16)
            cq_s[h] = jnp.zeros((HEAD_DIM, nb), F32)

    lane = lax.broadcasted_iota(jnp.int32, (HEAD_DIM, nb), 1)

    for h in range(N_HEADS):
        r = _dot_nt(C_ref[:, h].reshape(bb * HEAD_DIM, HEAD_DIM).astype(BF16), qb_s[h])
        cq = cq_s[h]
        for j in range(bb):
            cq = jnp.where(lane == i * bb + j, r[j * HEAD_DIM:(j + 1) * HEAD_DIM], cq)
        cq_s[h] = cq

    @pl.when(i == pl.num_programs(0) - 1)
    def _():
        heads, n_heads = [], []
        lane_m = lax.broadcasted_iota(jnp.int32, (nb, LANES), 1)
        m_out = jnp.zeros((nb, LANES), F32)
        for h in range(N_HEADS):
            cols = slice(h * HEAD_DIM, (h + 1) * HEAD_DIM)
            w, g, m_t = gate_terms(h)
            qf = head(Q0, h)
            kf = head(K0, h) * scale
            vf = head(SV0, h)
            n0 = n_ref[:, cols]
            qk = jnp.sum(qf.astype(BF16).astype(F32) * kf.astype(BF16).astype(F32),
                         axis=-1, keepdims=True)
            s = qk * w
            num = s * vf + g * cq_s[h].T
            den = s + g * jnp.sum(n0 * qf, axis=-1, keepdims=True)
            hc = num * (1.0 / jnp.maximum(jnp.abs(den), jnp.exp(-m_t)))
            heads.append(_head_out(hc, ghead_ref[:, cols], head(SO0, h)))
            n_heads.append(g * n0 + w * kf)
            m_out = jnp.where(lane_m == h, m_t, m_out)
        no_ref[...] = jnp.concatenate(n_heads, axis=-1)
        mo_ref[...] = m_out

        u = proj_ref[:, U0:U0 + POOL_WIDTH]
        pooled = []
        for gi, win in enumerate(POOL_WINDOWS):
            cols = slice(gi * POOL_GROUP, (gi + 1) * POOL_GROUP)
            a = u[:, cols]
            for jrow in range(POOL_BUF - (win - 1), POOL_BUF):
                a = a + buf_ref[jrow, :, cols]
            p = a / float(win) - u[:, cols]
            pooled.append(_dot(p.astype(BF16), wpool_ref[gi]) * pscale_ref[:, cols])
        mix_ref[...] = jnp.concatenate(heads + pooled, axis=-1)
        for jrow in range(POOL_BUF - 1):
            bufo_ref[jrow] = buf_ref[jrow + 1]
        bufo_ref[POOL_BUF - 1] = u


def _sample_step(proj, l, ghead, wpool, pscale, C, n, m, buf_t, *, bb):
    nb = proj.shape[0]
    D = MLSTM_WIDTH + POOL_WIDTH
    cblk = pl.BlockSpec((None, bb, N_HEADS, HEAD_DIM, HEAD_DIM), lambda i: (l, i, 0, 0, 0))
    in_specs = [_const_spec(proj.shape)] + [_layer_spec(a, l) for a in (ghead, wpool, pscale)] + [
        cblk, _layer_spec(n, l), _layer_spec(m, l), _layer_spec(buf_t, l)]
    args = [proj, ghead, wpool, pscale, C, n, m, buf_t]
    shapes = ((nb, D), n.shape[1:], (nb, LANES), buf_t.shape[1:],
              (N_HEADS, nb, LANES), (N_HEADS, HEAD_DIM, nb), (N_HEADS, nb, HEAD_DIM))
    dtypes = (F32,) * 6 + (BF16,)
    out_shapes = tuple(jax.ShapeDtypeStruct(s, d) for s, d in zip(shapes, dtypes))
    out_specs = tuple(pl.BlockSpec(s, lambda i, k=len(s): (0,) * k) for s in shapes)
    scratch = [
        pltpu.VMEM((N_HEADS, nb, HEAD_DIM), BF16),
        pltpu.VMEM((N_HEADS, HEAD_DIM, nb), F32),
    ]
    est = (3 * bb * N_HEADS * HEAD_DIM * HEAD_DIM * 4 + 4 * _layer_bytes(buf_t)
           + 8 * proj.size * 4)
    return pl.pallas_call(
        functools.partial(_sample_step_kernel, bb=bb),
        grid=(nb // bb,), in_specs=in_specs, out_specs=out_specs, out_shape=out_shapes,
        scratch_shapes=scratch,
        compiler_params=pltpu.CompilerParams(
            dimension_semantics=("arbitrary",), vmem_limit_bytes=_vmem_limit(est)),
        name="sample_step",
    )(*_in_hbm(*args))


def _state_update_kernel(C_ref, g_ref, wvt_ref, kb_ref, Co_ref, *, bb):
    i = pl.program_id(1)
    nb = kb_ref.shape[1]
    lane = lax.broadcasted_iota(jnp.int32, (HEAD_DIM, nb), 1)
    for h in range(N_HEADS):
        lhs = [jnp.where(lane == i * bb + j, wvt_ref[h], 0.0).astype(BF16) for j in range(bb)]
        outer = _dot(jnp.concatenate(lhs, axis=0), kb_ref[h])
        for j in range(bb):
            Co_ref[j, h] = (g_ref[h, pl.ds(i * bb + j, 1), :] * C_ref[j, h]
                            + outer[j * HEAD_DIM:(j + 1) * HEAD_DIM])


def _state_update(C, g, wvt, kb, *, bb):
    depth, nb = C.shape[:2]
    cblk = pl.BlockSpec((None, bb, N_HEADS, HEAD_DIM, HEAD_DIM), lambda l, i: (l, i, 0, 0, 0))
    per_layer = lambda a: pl.BlockSpec((None,) + a.shape[1:], lambda l, i: (l, 0, 0, 0))
    est = 5 * bb * N_HEADS * HEAD_DIM * HEAD_DIM * 4 + 2 * _layer_bytes(g, wvt, kb)
    return pl.pallas_call(
        functools.partial(_state_update_kernel, bb=bb),
        grid=(depth, nb // bb),
        in_specs=[cblk, per_layer(g), per_layer(wvt), per_layer(kb)],
        out_specs=cblk, out_shape=jax.ShapeDtypeStruct(C.shape, F32),
        compiler_params=pltpu.CompilerParams(
            dimension_semantics=("arbitrary", "arbitrary"), vmem_limit_bytes=_vmem_limit(est)),
        name="state_update",
    )(*_in_hbm(C, g, wvt, kb))


def _pack_w_in(w_in, b_gate):
    depth, d, _ = w_in.shape
    w = MLSTM_WIDTH
    g0 = 4 * w
    wqk = w_in[:, :, 0:2 * w].astype(BF16)
    wu = w_in[:, :, g0 + 2 * N_HEADS:].astype(BF16)
    pad = jnp.zeros((depth, d, GATE_SLAB - N_HEADS), BF16)
    gates = w_in[:, :, g0:g0 + 2 * N_HEADS].astype(BF16)
    wgate = jnp.concatenate([gates[:, :, :N_HEADS], pad, gates[:, :, N_HEADS:], pad], axis=-1)
    wvo = w_in[:, :, 2 * w:4 * w].astype(BF16)
    bpad = jnp.zeros((depth, GATE_SLAB - N_HEADS), b_gate.dtype)
    bslab = jnp.concatenate([b_gate[:, :N_HEADS], bpad, b_gate[:, N_HEADS:], bpad], axis=-1)
    return wqk, wu, wgate, wvo, bslab[:, None, :]


def kernel(x_prompt, x_sample, state_mlstm_C, state_mlstm_n, state_mlstm_m, state_pool_buf,
           g_mix, w_in, b_gate, g_head, w_pool, pool_scale, w_out, g_ffn, w_gate, w_up,
           w_down, g_final):
    depth = w_in.shape[0]
    B, T, D = x_prompt.shape
    nb = x_sample.shape[0]
    tm_mixer, tm_ffn, bb = 256, 1024, 16

    win = _pack_w_in(w_in, b_gate)
    wpool = w_pool.astype(BF16)
    wout = w_out.astype(BF16)
    ffn_w = (w_gate, w_up, w_down)
    row = lambda a: a[:, None, :]
    gmix, ghead, pscale, gffn = row(g_mix), row(g_head), row(pool_scale), row(g_ffn)
    ghead_rep = jnp.broadcast_to(g_head[:, :, None], g_head.shape + (CHUNK,))
    gfin = g_final[None, :]

    n_in = state_mlstm_n.reshape(depth, nb, MLSTM_WIDTH)
    m_pad = jnp.pad(state_mlstm_m, ((0, 0), (0, 0), (0, LANES - N_HEADS)))
    buf_t = jnp.transpose(state_pool_buf, (0, 2, 1, 3))

    xp = x_prompt
    xs = x_sample.reshape(nb, D)
    Cp, npr, mp, bp, ns, ms, bs, upd = [], [], [], [], [], [], [], []
    for l in range(depth):
        xm, C1, n1, m1, pb1, wg, wu, wd = _prompt_mixer(
            xp, l, gmix, *win, ghead_rep, wpool, pscale, wout, ffn_w, tm=tm_mixer, tiles=4)
        Cp.append(C1)
        npr.append(n1[:, :, 0, :])
        mp.append(m1[:, :N_HEADS, 0])
        bp.append(pb1[:, POOL_CARRY - POOL_BUF:, :])

        proj = _sample_proj(xs, l, gmix, *win)
        mix, n1, m1, buf1, *terms = _sample_step(
            proj, l, ghead, wpool, pscale, state_mlstm_C, n_in, m_pad, buf_t, bb=bb)
        ns.append(n1.reshape(nb, N_HEADS, HEAD_DIM))
        ms.append(m1[:, :N_HEADS])
        bs.append(jnp.transpose(buf1, (1, 0, 2)))
        upd.append(terms)

        xp, xs = _ffn(xm.reshape(B * T, D), xs, mix, l, gffn, wg, wu, wd, wout, tm=tm_ffn,
                      gfin=gfin if l == depth - 1 else None)
        xp = xp.reshape(B, T, D)
    y_prompt = xp
    y_sample = xs.reshape(nb, 1, D)
    C_sample = _state_update(state_mlstm_C, *(jnp.stack(t, 0) for t in zip(*upd)), bb=bb)

    st = lambda xs_: jnp.stack(xs_, 0)
    return (y_prompt, y_sample, st(Cp), st(npr), st(mp), st(bp),
            C_sample, st(ns), st(ms), st(bs))
```

```python
import functools

import jax
import jax.numpy as jnp
from jax import lax
from jax.experimental import pallas as pl
from jax.experimental.pallas import tpu as pltpu

F32 = jnp.float32
BF16 = jnp.bfloat16

EPS = 1e-6
N_HEADS = 4
HEAD_DIM = 128
MLSTM_WIDTH = N_HEADS * HEAD_DIM
POOL_WINDOWS = (2, 4, 8, 16)
POOL_GROUP = 128
POOL_WIDTH = POOL_GROUP * len(POOL_WINDOWS)
POOL_BUF = max(POOL_WINDOWS) - 1
POOL_CARRY = POOL_BUF + 1
CHUNK = 256
LANES = 128
FF_SUBTILE = 256
SUBLANES = 8
N_ROWS = 2 * SUBLANES
GATE_SLAB = LANES

Q0, K0, U0 = 0, MLSTM_WIDTH, 2 * MLSTM_WIDTH
GI0 = U0 + POOL_WIDTH
GF0 = GI0 + GATE_SLAB
TOK_COLS = GF0 + GATE_SLAB
VT0, OT0 = 0, MLSTM_WIDTH
TR_ROWS = 2 * MLSTM_WIDTH
SV0 = TOK_COLS
SO0 = TOK_COLS + MLSTM_WIDTH
SAMPLE_COLS = TOK_COLS + TR_ROWS

V7X_VMEM_BYTES = 64 * 1024 * 1024

_NT = (((1,), (1,)), ((), ()))
_TN = (((0,), (0,)), ((), ()))


def _dot(a, b):
    return jnp.dot(a, b, preferred_element_type=F32)


def _dot_nt(a, b):
    return lax.dot_general(a, b, _NT, preferred_element_type=F32)


def _dot_tn(a, b):
    return lax.dot_general(a, b, _TN, preferred_element_type=F32)


def _rms(x, g):
    return x * lax.rsqrt(jnp.mean(x * x, axis=-1, keepdims=True) + EPS) * g


def _log_sigmoid(x):
    return jnp.minimum(x, 0.0) - jnp.log1p(jnp.exp(-jnp.abs(x)))


def _lane_col(slab, j):
    lane = lax.broadcasted_iota(jnp.int32, slab.shape, 1)
    return jnp.sum(jnp.where(lane == j, slab, 0.0), axis=-1, keepdims=True)


def _split3(x):
    hi = x.astype(BF16)
    r1 = x - hi.astype(F32)
    mid = r1.astype(BF16)
    lo = (r1 - mid.astype(F32)).astype(BF16)
    return hi, mid, lo


def _cumsum_rows(x):
    r = lax.broadcasted_iota(jnp.int32, (CHUNK, CHUNK), 0)
    c = lax.broadcasted_iota(jnp.int32, (CHUNK, CHUNK), 1)
    tril = (r >= c).astype(BF16)
    hi, mid, lo = _split3(x)
    return _dot(tril, hi) + _dot(tril, mid) + _dot(tril, lo)


def _head_stages(h, tok_ref, tr_ref, a_slab, a_t, cs_t, ghead_ref, C_ref, n_ref, m_ref,
                 mask_st, out):
    cols = slice(h * HEAD_DIM, (h + 1) * HEAD_DIM)
    head = lambda c0: slice(c0 + h * HEAD_DIM, c0 + (h + 1) * HEAD_DIM)
    C, n_rows, m = C_ref[h], n_ref[h], m_ref[h:h + 1, 0:1]
    qb = tok_ref[:, head(Q0)].astype(BF16)
    kb = tok_ref[:, head(K0)].astype(BF16)
    by_q = _dot_nt(jnp.concatenate([kb, C.astype(BF16), n_rows.astype(BF16)], axis=0), qb)
    yield
    sT = by_q[:CHUNK]
    cqT = by_q[CHUNK:CHUNK + HEAD_DIM]
    qn = by_q[CHUNK + HEAD_DIM:CHUNK + HEAD_DIM + 1]
    a_row, b_row = a_t[h:h + 1, :], cs_t[h:h + 1, :]
    aT = jnp.where(mask_st, a_slab[:, h:h + 1], -jnp.inf)
    M = jnp.maximum(jnp.max(aT, axis=0, keepdims=True), m)
    swT = sT * jnp.exp(aT - M)
    g = jnp.exp(m - M)
    den = jnp.sum(swT, axis=0, keepdims=True) + g * qn
    inv = 1.0 / jnp.maximum(jnp.abs(den), jnp.exp(-(b_row + M)))
    vT = tr_ref[head(VT0), :]
    hT = (_dot(vT.astype(BF16), swT.astype(BF16)) + g * cqT) * inv
    hnT = hT * lax.rsqrt(jnp.mean(hT * hT, axis=0, keepdims=True) + EPS)
    out[h] = (hnT * ghead_ref[cols, :] * jax.nn.sigmoid(tr_ref[head(OT0), :])).astype(BF16)
    yield
    M_last = M[:, CHUNK - 1:CHUNK]
    g_end = jnp.exp(m - M_last)
    w_end = jnp.exp(a_row - M_last)
    w_rows = jnp.broadcast_to(w_end, (N_ROWS, CHUNK))
    by_k = _dot(jnp.concatenate([(vT * w_end).astype(BF16), w_rows.astype(BF16)], axis=0), kb)
    C_ref[h] = g_end * C + by_k[:HEAD_DIM]
    n_ref[h] = g_end * n_rows + by_k[HEAD_DIM:]
    m_ref[h:h + 1, :] = jnp.broadcast_to(b_row[:, CHUNK - 1:CHUNK] + M_last, (1, LANES))
    yield


def _skewed(streams):
    streams = list(streams)
    live = []
    while streams or live:
        if streams:
            live.append(streams.pop(0))
        for g in list(live):
            try:
                next(g)
            except StopIteration:
                live.remove(g)
        if streams or live:
            yield


def _norm_tile(x_ref, gmix_ref, hn_ref):
    hn_ref[...] = _rms(x_ref[...], gmix_ref[...]).astype(BF16)


def _in_proj_items(hn_ref, wqk_ref, wu_ref, wgate_ref, wtr_ref, bslab_ref, tok_ref, tr_ref,
                   then=None):
    tok_ref[:, GI0:GI0 + 2 * GATE_SLAB] = _dot(hn_ref[...], wgate_ref[...]) + bslab_ref[...]
    yield
    tok_ref[:, Q0:Q0 + MLSTM_WIDTH] = _dot(hn_ref[...], wqk_ref[:, :MLSTM_WIDTH])
    yield
    tok_ref[:, K0:K0 + MLSTM_WIDTH] = (_dot(hn_ref[...], wqk_ref[:, MLSTM_WIDTH:])
                                       * (HEAD_DIM ** -0.5))
    yield
    tok_ref[:, U0:U0 + POOL_WIDTH] = _dot(hn_ref[...], wu_ref[...])
    yield
    for r0 in (VT0, OT0):
        tr_ref[r0:r0 + MLSTM_WIDTH, :] = _dot_nt(wtr_ref[r0:r0 + MLSTM_WIDTH, :], hn_ref[...])
        yield
    if then is not None:
        then()
        yield


def _mix_items(x_ref, tok_ref, tr_ref, t, ghead_ref, wpool_ref, pscale_ref, wout_ref,
               C_ref, n_ref, m_ref, pb_ref, out_ref, tm):
    r = lax.broadcasted_iota(jnp.int32, (CHUNK, CHUNK), 0)
    c = lax.broadcasted_iota(jnp.int32, (CHUNK, CHUNK), 1)
    mask_st = r <= c

    assert tm == CHUNK, "one mLSTM chunk per tile"
    cs = _cumsum_rows(_log_sigmoid(tok_ref[:, GF0:GF0 + GATE_SLAB]))
    a_slab = tok_ref[:, GI0:GI0 + GATE_SLAB] - cs
    cs_t = cs.T
    a_t = a_slab.T
    yield
    heads = [None] * N_HEADS
    yield from _skewed(
        _head_stages(h, tok_ref, tr_ref, a_slab, a_t, cs_t, ghead_ref, C_ref, n_ref, m_ref,
                     mask_st, heads)
        for h in range(N_HEADS))
    yield
    hmT = jnp.concatenate(heads, axis=0)

    u = tok_ref[:, U0:U0 + POOL_WIDTH]
    ext = jnp.concatenate([pb_ref[...], u], axis=0)
    pb_ref[...] = u[tm - POOL_CARRY:, :]
    pos = t * tm + lax.broadcasted_iota(jnp.int32, (tm, POOL_GROUP), 0)
    pooled = []
    for gi, win in enumerate(POOL_WINDOWS):
        cols = slice(gi * POOL_GROUP, (gi + 1) * POOL_GROUP)
        a = ext[:, cols]
        step = 1
        while step < win:
            a = a[step:, :] + a[:a.shape[0] - step, :]
            step *= 2
        a = a[a.shape[0] - tm:, :]
        cnt = jnp.minimum(win, pos + 1).astype(F32)
        p = a / cnt - u[:, cols]
        pooled.append((_dot(p.astype(BF16), wpool_ref[gi]) * pscale_ref[:, cols]).astype(BF16))
    yield
    pm = jnp.concatenate(pooled, axis=-1)
    out_ref[...] = (x_ref[...] + _dot_tn(hmT, wout_ref[:MLSTM_WIDTH, :])
                    + _dot(pm, wout_ref[MLSTM_WIDTH:, :]))
    yield


def _interleave(*streams):
    streams = list(streams)
    while streams:
        for s in list(streams):
            try:
                next(s)
            except StopIteration:
                streams.remove(s)


def _prompt_mixer_kernel(xcur_ref, xnext_ref, gmix_ref, wqk_ref, wu_ref, wgate_ref, wvo_ref,
                         bslab_ref, ghead_ref, wpool_ref, pscale_ref, wout_ref,
                         fg_ref, fu_ref, fd_ref,
                         xo_ref, C_ref, n_ref, m_ref, pb_ref, fgb_ref, fub_ref, fdb_ref,
                         tok_a, tr_a, hn_a, tok_b, tr_b, hn_b, wtr_s, *, tm, nt, tiles):
    fgb_ref[...] = fg_ref[...].astype(BF16)
    fub_ref[...] = fu_ref[...].astype(BF16)
    fdb_ref[...] = fd_ref[...].astype(BF16)
    s = pl.program_id(0)
    t0 = lax.rem(tiles * s, nt)
    proj_args = (wqk_ref, wu_ref, wgate_ref, wtr_s, bslab_ref)
    mix_args = (ghead_ref, wpool_ref, pscale_ref, wout_ref, C_ref, n_ref, m_ref, pb_ref)
    slots = ((tok_a, tr_a, hn_a), (tok_b, tr_b, hn_b))
    x_tile = lambda k: xcur_ref.at[k] if k < tiles else xnext_ref.at[k - tiles]

    @pl.when(s == 0)
    def _():
        wtr_s[...] = wvo_ref[...].T
        _norm_tile(x_tile(0), gmix_ref, hn_a)
        _interleave(_in_proj_items(hn_a, *proj_args, tok_a, tr_a))
        _norm_tile(x_tile(1), gmix_ref, hn_b)

    @pl.when(t0 == 0)
    def _():
        C_ref[...] = jnp.zeros_like(C_ref)
        n_ref[...] = jnp.zeros_like(n_ref)
        m_ref[...] = jnp.zeros_like(m_ref)
        pb_ref[...] = jnp.zeros_like(pb_ref)

    for k in range(tiles):
        tok, tr, hn = slots[k % 2]
        tok_n, tr_n, hn_n = slots[(k + 1) % 2]
        _interleave(
            _in_proj_items(hn_n, *proj_args, tok_n, tr_n,
                           then=functools.partial(_norm_tile, x_tile(k + 2), gmix_ref, hn)),
            _mix_items(x_tile(k), tok, tr, t0 + k, *mix_args, xo_ref.at[k], tm))


def _const_spec(shape):
    nd = len(shape)
    return pl.BlockSpec(shape, lambda *_: (0,) * nd, pipeline_mode=pl.Buffered(1))


def _layer_spec(a, l):
    nd = a.ndim - 1
    return pl.BlockSpec((None,) + a.shape[1:], lambda *_: (l,) + (0,) * nd,
                        pipeline_mode=pl.Buffered(1))


def _in_hbm(*arrays):
    return [pltpu.with_memory_space_constraint(a, pltpu.MemorySpace.HBM) for a in arrays]


def _layer_bytes(*arrays):
    return sum(a.size // a.shape[0] * a.dtype.itemsize for a in arrays)


def _vmem_limit(nbytes):
    return int(min(V7X_VMEM_BYTES - (4 << 20), max(nbytes, 16 << 20)))


def _slab_rows(rows, nsteps):
    for steps_per_slab in (1, 2):
        per = rows * steps_per_slab // nsteps
        if per * nsteps == rows * steps_per_slab and per % N_ROWS == 0:
            return per, steps_per_slab
    raise ValueError(f"cannot split {rows} weight rows over {nsteps} grid steps")


def _prompt_mixer(x, l, gmix, wqk, wu, wgate, wvo, bslab, ghead_rep, wpool, pscale, wout,
                  ffn_w, *, tm, tiles):
    B, T, D = x.shape
    nt = T // tm
    assert tiles % 2 == 0 and nt % tiles == 0, "a grid step covers an even number of tiles of one sequence"
    ntiles = B * nt
    nsteps = ntiles // tiles
    xt = x.reshape(ntiles, tm, D)
    pair = pl.BlockSpec((tiles, tm, D), lambda s: (s, 0, 0))
    nxt = pl.BlockSpec((2, tm, D),
                       lambda s: (jnp.minimum((s + 1) * (tiles // 2), ntiles // 2 - 1), 0, 0))
    seq = lambda s: (tiles * s) // nt
    out_shapes = (
        jax.ShapeDtypeStruct((ntiles, tm, D), F32),
        jax.ShapeDtypeStruct((B, N_HEADS, HEAD_DIM, HEAD_DIM), F32),
        jax.ShapeDtypeStruct((B, N_HEADS, N_ROWS, HEAD_DIM), F32),
        jax.ShapeDtypeStruct((B, SUBLANES, LANES), F32),
        jax.ShapeDtypeStruct((B, POOL_CARRY, POOL_WIDTH), F32),
    )
    out_specs = (
        pair,
        pl.BlockSpec((None, N_HEADS, HEAD_DIM, HEAD_DIM), lambda s: (seq(s), 0, 0, 0)),
        pl.BlockSpec((None, N_HEADS, N_ROWS, HEAD_DIM), lambda s: (seq(s), 0, 0, 0)),
        pl.BlockSpec((None, SUBLANES, LANES), lambda s: (seq(s), 0, 0)),
        pl.BlockSpec((None, POOL_CARRY, POOL_WIDTH), lambda s: (seq(s), 0, 0)),
    )
    consts = (gmix, wqk, wu, wgate, wvo, bslab, ghead_rep, wpool, pscale, wout)
    in_specs = [pair, nxt] + [_layer_spec(a, l) for a in consts]
    for w in ffn_w:
        rows, cols = w.shape[1:]
        per, stride = _slab_rows(rows, nsteps)
        in_specs.append(pl.BlockSpec((None, per, cols), lambda s, k=stride: (l, s // k, 0)))
        out_specs += (pl.BlockSpec((per, cols), lambda s, k=stride: (s // k, 0)),)
        out_shapes += (jax.ShapeDtypeStruct((rows, cols), BF16),)
    est = _layer_bytes(*consts) + 6 * tiles * tm * D * 4 + 12 * tm * SAMPLE_COLS * 4 + (4 << 20)
    scratch = [pltpu.VMEM((tm, TOK_COLS), F32), pltpu.VMEM((TR_ROWS, tm), F32),
               pltpu.VMEM((tm, D), BF16)] * 2 + [pltpu.VMEM((TR_ROWS, D), BF16)]
    outs = pl.pallas_call(
        functools.partial(_prompt_mixer_kernel, tm=tm, nt=nt, tiles=tiles),
        grid=(nsteps,), in_specs=in_specs, out_specs=out_specs, out_shape=out_shapes,
        scratch_shapes=scratch,
        compiler_params=pltpu.CompilerParams(
            dimension_semantics=("arbitrary",),
            vmem_limit_bytes=_vmem_limit(est)),
        name="prompt_mixer",
    )(*_in_hbm(xt, xt, *consts, *ffn_w))
    return (outs[0].reshape(B, T, D),) + tuple(outs[1:])


def _ffn_tile(x, gffn_ref, wg_s, wu_s, wd_s, gfin_ref):
    h2 = _rms(x, gffn_ref[...]).astype(BF16)
    a = _dot(h2, wg_s[...])
    bu = _dot(h2, wu_s[...])
    act = (a * jax.nn.sigmoid(a) * bu).astype(BF16)
    y = x + _dot(act, wd_s[...])
    return y if gfin_ref is None else _rms(y, gfin_ref[...])


def _ffn_kernel(*refs, final, n_tiles):
    refs = list(refs)
    x_ref, xs_ref, mixs_ref, wout_ref, gffn_ref, wg_ref, wu_ref, wd_ref = refs[:8]
    refs = refs[8:]
    gfin_ref = refs.pop(0) if final else None
    out_ref, outs_ref = refs
    i = pl.program_id(0)
    weights = (gffn_ref, wg_ref, wu_ref, wd_ref, gfin_ref)

    @pl.when(i < n_tiles)
    def _():
        for r0 in range(0, x_ref.shape[0], FF_SUBTILE):
            rows = slice(r0, r0 + FF_SUBTILE)
            out_ref[rows, :] = _ffn_tile(x_ref[rows, :], *weights)

    @pl.when(i == n_tiles)
    def _():
        xs = xs_ref[...] + _dot(mixs_ref[...].astype(BF16), wout_ref[...])
        outs_ref[...] = _ffn_tile(xs, *weights)


def _ffn(x, xs, mixs, l, gffn, wg, wu, wd, wout, *, tm, gfin=None):
    M, D = x.shape
    dff = wg.shape[-1]
    n_tiles = M // tm
    assert n_tiles * tm == M
    tile = pl.BlockSpec((tm, D), lambda i: (jnp.minimum(i, n_tiles - 1), 0))
    args = [x, xs, mixs, wout, gffn, wg, wu, wd]
    in_specs = [tile, _const_spec(xs.shape), _const_spec(mixs.shape), _layer_spec(wout, l),
                _layer_spec(gffn, l)] + [_const_spec(w.shape) for w in (wg, wu, wd)]
    if gfin is not None:
        args.append(gfin)
        in_specs.append(_const_spec(gfin.shape))
    est = (3 * D * dff * 2 + _layer_bytes(wout) + 6 * xs.size * 4
           + 6 * tm * D * 4 + 4 * tm * dff * 4)
    return pl.pallas_call(
        functools.partial(_ffn_kernel, final=gfin is not None, n_tiles=n_tiles),
        grid=(n_tiles + 1,), in_specs=in_specs,
        out_specs=(tile, pl.BlockSpec(xs.shape, lambda i: (0, 0))),
        out_shape=(jax.ShapeDtypeStruct((M, D), F32), jax.ShapeDtypeStruct(xs.shape, F32)),
        compiler_params=pltpu.CompilerParams(
            dimension_semantics=("arbitrary",), vmem_limit_bytes=_vmem_limit(est)),
        name="ffn_final" if gfin is not None else "ffn",
    )(*_in_hbm(*args))


def _sample_proj_kernel(x_ref, gmix_ref, wqk_ref, wu_ref, wgate_ref, wvo_ref, bslab_ref, proj_ref):
    hn = _rms(x_ref[...], gmix_ref[...]).astype(BF16)
    proj_ref[:, Q0:U0] = _dot(hn, wqk_ref[...])
    proj_ref[:, U0:GI0] = _dot(hn, wu_ref[...])
    proj_ref[:, GI0:TOK_COLS] = _dot(hn, wgate_ref[...]) + bslab_ref[...]
    proj_ref[:, TOK_COLS:] = _dot(hn, wvo_ref[...])


def _sample_proj(x, l, gmix, wqk, wu, wgate, wvo, bslab):
    M = x.shape[0]
    consts = (gmix, wqk, wu, wgate, wvo, bslab)
    est = 2 * _layer_bytes(*consts) + 8 * M * SAMPLE_COLS * 4
    return pl.pallas_call(
        _sample_proj_kernel,
        grid=(1,),
        in_specs=[_const_spec(x.shape)] + [_layer_spec(a, l) for a in consts],
        out_specs=pl.BlockSpec((M, SAMPLE_COLS), lambda i: (0, 0)),
        out_shape=jax.ShapeDtypeStruct((M, SAMPLE_COLS), F32),
        compiler_params=pltpu.CompilerParams(
            dimension_semantics=("arbitrary",), vmem_limit_bytes=_vmem_limit(est)),
        name="sample_proj",
    )(*_in_hbm(x, *consts))


def _head_out(h, g_head, o):
    hn = h * lax.rsqrt(jnp.mean(h * h, axis=-1, keepdims=True) + EPS)
    return hn * g_head * jax.nn.sigmoid(o)


def _sample_step_kernel(proj_ref, ghead_ref, wpool_ref, pscale_ref, C_ref, n_ref, m_ref, buf_ref,
                        mix_ref, no_ref, mo_ref, bufo_ref, g_ref, wvt_ref, kb_ref,
                        qb_s, cq_s, *, bb):
    i = pl.program_id(0)
    nb = proj_ref.shape[0]
    scale = HEAD_DIM ** -0.5
    head = lambda c0, h: proj_ref[:, c0 + h * HEAD_DIM:c0 + (h + 1) * HEAD_DIM]

    def gate_terms(h):
        i_c = _lane_col(proj_ref[:, GI0:GI0 + GATE_SLAB], h)
        lf = _lane_col(_log_sigmoid(proj_ref[:, GF0:GF0 + GATE_SLAB]), h)
        m0 = _lane_col(m_ref[...], h)
        inter = lf + m0
        m_t = jnp.maximum(inter, i_c)
        return jnp.exp(i_c - m_t), jnp.exp(inter - m_t), m_t

    @pl.when(i == 0)
    def _():
        for h in range(N_HEADS):
            w, g, _ = gate_terms(h)
            g_ref[h] = jnp.broadcast_to(g, (nb, LANES))
            wvt_ref[h] = (w * head(SV0, h)).T
            kb_ref[h] = (head(K0, h) * scale).astype(BF16)
            qb_s[h] = head(Q0, h).astype(BF16)
            cq_s[h] = jnp.zeros((HEAD_DIM, nb), F32)

    lane = lax.broadcasted_iota(jnp.int32, (HEAD_DIM, nb), 1)

    for h in range(N_HEADS):
        r = _dot_nt(C_ref[:, h].reshape(bb * HEAD_DIM, HEAD_DIM).astype(BF16), qb_s[h])
        cq = cq_s[h]
        for j in range(bb):
            cq = jnp.where(lane == i * bb + j, r[j * HEAD_DIM:(j + 1) * HEAD_DIM], cq)
        cq_s[h] = cq

    @pl.when(i == pl.num_programs(0) - 1)
    def _():
        heads, n_heads = [], []
        lane_m = lax.broadcasted_iota(jnp.int32, (nb, LANES), 1)
        m_out = jnp.zeros((nb, LANES), F32)
        for h in range(N_HEADS):
            cols = slice(h * HEAD_DIM, (h + 1) * HEAD_DIM)
            w, g, m_t = gate_terms(h)
            qf = head(Q0, h)
            kf = head(K0, h) * scale
            vf = head(SV0, h)
            n0 = n_ref[:, cols]
            qk = jnp.sum(qf.astype(BF16).astype(F32) * kf.astype(BF16).astype(F32),
                         axis=-1, keepdims=True)
            s = qk * w
            num = s * vf + g * cq_s[h].T
            den = s + g * jnp.sum(n0 * qf, axis=-1, keepdims=True)
            hc = num * (1.0 / jnp.maximum(jnp.abs(den), jnp.exp(-m_t)))
            heads.append(_head_out(hc, ghead_ref[:, cols], head(SO0, h)))
            n_heads.append(g * n0 + w * kf)
            m_out = jnp.where(lane_m == h, m_t, m_out)
        no_ref[...] = jnp.concatenate(n_heads, axis=-1)
        mo_ref[...] = m_out

        u = proj_ref[:, U0:U0 + POOL_WIDTH]
        pooled = []
        for gi, win in enumerate(POOL_WINDOWS):
            cols = slice(gi * POOL_GROUP, (gi + 1) * POOL_GROUP)
            a = u[:, cols]
            for jrow in range(POOL_BUF - (win - 1), POOL_BUF):
                a = a + buf_ref[jrow, :, cols]
            p = a / float(win) - u[:, cols]
            pooled.append(_dot(p.astype(BF16), wpool_ref[gi]) * pscale_ref[:, cols])
        mix_ref[...] = jnp.concatenate(heads + pooled, axis=-1)
        for jrow in range(POOL_BUF - 1):
            bufo_ref[jrow] = buf_ref[jrow + 1]
        bufo_ref[POOL_BUF - 1] = u


def _sample_step(proj, l, ghead, wpool, pscale, C, n, m, buf_t, *, bb):
    nb = proj.shape[0]
    D = MLSTM_WIDTH + POOL_WIDTH
    cblk = pl.BlockSpec((None, bb, N_HEADS, HEAD_DIM, HEAD_DIM), lambda i: (l, i, 0, 0, 0))
    in_specs = [_const_spec(proj.shape)] + [_layer_spec(a, l) for a in (ghead, wpool, pscale)] + [
        cblk, _layer_spec(n, l), _layer_spec(m, l), _layer_spec(buf_t, l)]
    args = [proj, ghead, wpool, pscale, C, n, m, buf_t]
    shapes = ((nb, D), n.shape[1:], (nb, LANES), buf_t.shape[1:],
              (N_HEADS, nb, LANES), (N_HEADS, HEAD_DIM, nb), (N_HEADS, nb, HEAD_DIM))
    dtypes = (F32,) * 6 + (BF16,)
    out_shapes = tuple(jax.ShapeDtypeStruct(s, d) for s, d in zip(shapes, dtypes))
    out_specs = tuple(pl.BlockSpec(s, lambda i, k=len(s): (0,) * k) for s in shapes)
    scratch = [
        pltpu.VMEM((N_HEADS, nb, HEAD_DIM), BF16),
        pltpu.VMEM((N_HEADS, HEAD_DIM, nb), F32),
    ]
    est = (3 * bb * N_HEADS * HEAD_DIM * HEAD_DIM * 4 + 4 * _layer_bytes(buf_t)
           + 8 * proj.size * 4)
    return pl.pallas_call(
        functools.partial(_sample_step_kernel, bb=bb),
        grid=(nb // bb,), in_specs=in_specs, out_specs=out_specs, out_shape=out_shapes,
        scratch_shapes=scratch,
        compiler_params=pltpu.CompilerParams(
            dimension_semantics=("arbitrary",), vmem_limit_bytes=_vmem_limit(est)),
        name="sample_step",
    )(*_in_hbm(*args))


def _state_update_kernel(C_ref, g_ref, wvt_ref, kb_ref, Co_ref, *, bb):
    i = pl.program_id(1)
    nb = kb_ref.shape[1]
    lane = lax.broadcasted_iota(jnp.int32, (HEAD_DIM, nb), 1)
    for h in range(N_HEADS):
        lhs = [jnp.where(lane == i * bb + j, wvt_ref[h], 0.0).astype(BF16) for j in range(bb)]
        outer = _dot(jnp.concatenate(lhs, axis=0), kb_ref[h])
        for j in range(bb):
            Co_ref[j, h] = (g_ref[h, pl.ds(i * bb + j, 1), :] * C_ref[j, h]
                            + outer[j * HEAD_DIM:(j + 1) * HEAD_DIM])


def _state_update(C, g, wvt, kb, *, bb):
    depth, nb = C.shape[:2]
    cblk = pl.BlockSpec((None, bb, N_HEADS, HEAD_DIM, HEAD_DIM), lambda l, i: (l, i, 0, 0, 0))
    per_layer = lambda a: pl.BlockSpec((None,) + a.shape[1:], lambda l, i: (l, 0, 0, 0))
    est = 5 * bb * N_HEADS * HEAD_DIM * HEAD_DIM * 4 + 2 * _layer_bytes(g, wvt, kb)
    return pl.pallas_call(
        functools.partial(_state_update_kernel, bb=bb),
        grid=(depth, nb // bb),
        in_specs=[cblk, per_layer(g), per_layer(wvt), per_layer(kb)],
        out_specs=cblk, out_shape=jax.ShapeDtypeStruct(C.shape, F32),
        compiler_params=pltpu.CompilerParams(
            dimension_semantics=("arbitrary", "arbitrary"), vmem_limit_bytes=_vmem_limit(est)),
        name="state_update",
    )(*_in_hbm(C, g, wvt, kb))


def _pack_w_in(w_in, b_gate):
    depth, d, _ = w_in.shape
    w = MLSTM_WIDTH
    g0 = 4 * w
    wqk = w_in[:, :, 0:2 * w].astype(BF16)
    wu = w_in[:, :, g0 + 2 * N_HEADS:].astype(BF16)
    pad = jnp.zeros((depth, d, GATE_SLAB - N_HEADS), BF16)
    gates = w_in[:, :, g0:g0 + 2 * N_HEADS].astype(BF16)
    wgate = jnp.concatenate([gates[:, :, :N_HEADS], pad, gates[:, :, N_HEADS:], pad], axis=-1)
    wvo = w_in[:, :, 2 * w:4 * w].astype(BF16)
    bpad = jnp.zeros((depth, GATE_SLAB - N_HEADS), b_gate.dtype)
    bslab = jnp.concatenate([b_gate[:, :N_HEADS], bpad, b_gate[:, N_HEADS:], bpad], axis=-1)
    return wqk, wu, wgate, wvo, bslab[:, None, :]


def kernel(x_prompt, x_sample, state_mlstm_C, state_mlstm_n, state_mlstm_m, state_pool_buf,
           g_mix, w_in, b_gate, g_head, w_pool, pool_scale, w_out, g_ffn, w_gate, w_up,
           w_down, g_final):
    depth = w_in.shape[0]
    B, T, D = x_prompt.shape
    nb = x_sample.shape[0]
    tm_mixer, tm_ffn, bb, bb_update = 256, 1024, 16, 32

    win = _pack_w_in(w_in, b_gate)
    wpool = w_pool.astype(BF16)
    wout = w_out.astype(BF16)
    ffn_w = (w_gate, w_up, w_down)
    row = lambda a: a[:, None, :]
    gmix, ghead, pscale, gffn = row(g_mix), row(g_head), row(pool_scale), row(g_ffn)
    ghead_rep = jnp.broadcast_to(g_head[:, :, None], g_head.shape + (CHUNK,))
    gfin = g_final[None, :]

    n_in = state_mlstm_n.reshape(depth, nb, MLSTM_WIDTH)
    m_pad = jnp.pad(state_mlstm_m, ((0, 0), (0, 0), (0, LANES - N_HEADS)))
    buf_t = jnp.transpose(state_pool_buf, (0, 2, 1, 3))

    xp = x_prompt
    xs = x_sample.reshape(nb, D)
    Cp, npr, mp, bp, ns, ms, bs, upd = [], [], [], [], [], [], [], []
    for l in range(depth):
        xm, C1, n1, m1, pb1, wg, wu, wd = _prompt_mixer(
            xp, l, gmix, *win, ghead_rep, wpool, pscale, wout, ffn_w, tm=tm_mixer, tiles=4)
        Cp.append(C1)
        npr.append(n1[:, :, 0, :])
        mp.append(m1[:, :N_HEADS, 0])
        bp.append(pb1[:, POOL_CARRY - POOL_BUF:, :])

        proj = _sample_proj(xs, l, gmix, *win)
        mix, n1, m1, buf1, *terms = _sample_step(
            proj, l, ghead, wpool, pscale, state_mlstm_C, n_in, m_pad, buf_t, bb=bb)
        ns.append(n1.reshape(nb, N_HEADS, HEAD_DIM))
        ms.append(m1[:, :N_HEADS])
        bs.append(jnp.transpose(buf1, (1, 0, 2)))
        upd.append(terms)

        xp, xs = _ffn(xm.reshape(B * T, D), xs, mix, l, gffn, wg, wu, wd, wout, tm=tm_ffn,
                      gfin=gfin if l == depth - 1 else None)
        xp = xp.reshape(B, T, D)
    y_prompt = xp
    y_sample = xs.reshape(nb, 1, D)
    C_sample = _state_update(state_mlstm_C, *(jnp.stack(t, 0) for t in zip(*upd)), bb=bb_update)

    st = lambda xs_: jnp.stack(xs_, 0)
    return (y_prompt, y_sample, st(Cp), st(npr), st(mp), st(bp),
            C_sample, st(ns), st(ms), st(bs))
```

```python
import functools

import jax
import jax.numpy as jnp
from jax import lax
from jax.experimental import pallas as pl
from jax.experimental.pallas import tpu as pltpu

F32 = jnp.float32
BF16 = jnp.bfloat16

EPS = 1e-6
N_HEADS = 4
HEAD_DIM = 128
MLSTM_WIDTH = N_HEADS * HEAD_DIM
POOL_WINDOWS = (2, 4, 8, 16)
POOL_GROUP = 128
POOL_WIDTH = POOL_GROUP * len(POOL_WINDOWS)
POOL_BUF = max(POOL_WINDOWS) - 1
POOL_CARRY = POOL_BUF + 1
CHUNK = 256
LANES = 128
FF_SUBTILE = 256
SUBLANES = 8
N_ROWS = 2 * SUBLANES
GATE_SLAB = LANES

Q0, K0, U0 = 0, MLSTM_WIDTH, 2 * MLSTM_WIDTH
GI0 = U0 + POOL_WIDTH
GF0 = GI0 + GATE_SLAB
TOK_COLS = GF0 + GATE_SLAB
VT0, OT0 = 0, MLSTM_WIDTH
TR_ROWS = 2 * MLSTM_WIDTH
SV0 = TOK_COLS
SO0 = TOK_COLS + MLSTM_WIDTH
SAMPLE_COLS = TOK_COLS + TR_ROWS

V7X_VMEM_BYTES = 64 * 1024 * 1024
VMEM_RESERVE_BYTES = 4 * 1024 * 1024
VMEM_FLOOR_BYTES = 16 * 1024 * 1024

_NT = (((1,), (1,)), ((), ()))
_TN = (((0,), (0,)), ((), ()))


def _dot(a, b):
    return jnp.dot(a, b, preferred_element_type=F32)


def _dot_nt(a, b):
    return lax.dot_general(a, b, _NT, preferred_element_type=F32)


def _dot_tn(a, b):
    return lax.dot_general(a, b, _TN, preferred_element_type=F32)


def _rms(x, g):
    return x * lax.rsqrt(jnp.mean(x * x, axis=-1, keepdims=True) + EPS) * g


def _log_sigmoid(x):
    return jnp.minimum(x, 0.0) - jnp.log1p(jnp.exp(-jnp.abs(x)))


def _lane_col(slab, j):
    lane = lax.broadcasted_iota(jnp.int32, slab.shape, 1)
    return jnp.sum(jnp.where(lane == j, slab, 0.0), axis=-1, keepdims=True)


def _split3(x):
    hi = x.astype(BF16)
    r1 = x - hi.astype(F32)
    mid = r1.astype(BF16)
    lo = (r1 - mid.astype(F32)).astype(BF16)
    return hi, mid, lo


def _cumsum_rows(x):
    r = lax.broadcasted_iota(jnp.int32, (CHUNK, CHUNK), 0)
    c = lax.broadcasted_iota(jnp.int32, (CHUNK, CHUNK), 1)
    tril = (r >= c).astype(BF16)
    hi, mid, lo = _split3(x)
    return _dot(tril, hi) + _dot(tril, mid) + _dot(tril, lo)


def _head_stages(h, tok_ref, tr_ref, a_slab, a_t, cs_t, ghead_ref, C_ref, n_ref, m_ref,
                 mask_st, out):
    cols = slice(h * HEAD_DIM, (h + 1) * HEAD_DIM)
    head = lambda c0: slice(c0 + h * HEAD_DIM, c0 + (h + 1) * HEAD_DIM)
    C, n_rows, m = C_ref[h], n_ref[h], m_ref[h:h + 1, 0:1]
    qb = tok_ref[:, head(Q0)].astype(BF16)
    kb = tok_ref[:, head(K0)].astype(BF16)
    by_q = _dot_nt(jnp.concatenate([kb, C.astype(BF16), n_rows.astype(BF16)], axis=0), qb)
    yield
    sT = by_q[:CHUNK]
    cqT = by_q[CHUNK:CHUNK + HEAD_DIM]
    qn = by_q[CHUNK + HEAD_DIM:CHUNK + HEAD_DIM + 1]
    a_row, b_row = a_t[h:h + 1, :], cs_t[h:h + 1, :]
    aT = jnp.where(mask_st, a_slab[:, h:h + 1], -jnp.inf)
    M = jnp.maximum(jnp.max(aT, axis=0, keepdims=True), m)
    swT = sT * jnp.exp(aT - M)
    g = jnp.exp(m - M)
    den = jnp.sum(swT, axis=0, keepdims=True) + g * qn
    inv = 1.0 / jnp.maximum(jnp.abs(den), jnp.exp(-(b_row + M)))
    vT = tr_ref[head(VT0), :]
    hT = (_dot(vT.astype(BF16), swT.astype(BF16)) + g * cqT) * inv
    hnT = hT * lax.rsqrt(jnp.mean(hT * hT, axis=0, keepdims=True) + EPS)
    out[h] = (hnT * ghead_ref[cols, :] * jax.nn.sigmoid(tr_ref[head(OT0), :])).astype(BF16)
    yield
    M_last = M[:, CHUNK - 1:CHUNK]
    g_end = jnp.exp(m - M_last)
    w_end = jnp.exp(a_row - M_last)
    w_rows = jnp.broadcast_to(w_end, (N_ROWS, CHUNK))
    by_k = _dot(jnp.concatenate([(vT * w_end).astype(BF16), w_rows.astype(BF16)], axis=0), kb)
    C_ref[h] = g_end * C + by_k[:HEAD_DIM]
    n_ref[h] = g_end * n_rows + by_k[HEAD_DIM:]
    m_ref[h:h + 1, :] = jnp.broadcast_to(b_row[:, CHUNK - 1:CHUNK] + M_last, (1, LANES))
    yield


def _skewed(streams):
    streams = list(streams)
    live = []
    while streams or live:
        if streams:
            live.append(streams.pop(0))
        for g in list(live):
            try:
                next(g)
            except StopIteration:
                live.remove(g)
        if streams or live:
            yield


def _norm_tile(x_ref, gmix_ref, hn_ref):
    hn_ref[...] = _rms(x_ref[...], gmix_ref[...]).astype(BF16)


def _in_proj_items(hn_ref, wqk_ref, wu_ref, wgate_ref, wtr_ref, bslab_ref, tok_ref, tr_ref,
                   then=None):
    tok_ref[:, GI0:GI0 + 2 * GATE_SLAB] = _dot(hn_ref[...], wgate_ref[...]) + bslab_ref[...]
    yield
    tok_ref[:, Q0:Q0 + MLSTM_WIDTH] = _dot(hn_ref[...], wqk_ref[:, :MLSTM_WIDTH])
    yield
    tok_ref[:, K0:K0 + MLSTM_WIDTH] = (_dot(hn_ref[...], wqk_ref[:, MLSTM_WIDTH:])
                                       * (HEAD_DIM ** -0.5))
    yield
    tok_ref[:, U0:U0 + POOL_WIDTH] = _dot(hn_ref[...], wu_ref[...])
    yield
    for r0 in (VT0, OT0):
        tr_ref[r0:r0 + MLSTM_WIDTH, :] = _dot_nt(wtr_ref[r0:r0 + MLSTM_WIDTH, :], hn_ref[...])
        yield
    if then is not None:
        then()
        yield


def _mix_items(x_ref, tok_ref, tr_ref, t, ghead_ref, wpool_ref, pscale_ref, wout_ref,
               C_ref, n_ref, m_ref, pb_ref, out_ref, tm):
    r = lax.broadcasted_iota(jnp.int32, (CHUNK, CHUNK), 0)
    c = lax.broadcasted_iota(jnp.int32, (CHUNK, CHUNK), 1)
    mask_st = r <= c

    assert tm == CHUNK, "one mLSTM chunk per tile"
    cs = _cumsum_rows(_log_sigmoid(tok_ref[:, GF0:GF0 + GATE_SLAB]))
    a_slab = tok_ref[:, GI0:GI0 + GATE_SLAB] - cs
    cs_t = cs.T
    a_t = a_slab.T
    yield
    heads = [None] * N_HEADS
    yield from _skewed(
        _head_stages(h, tok_ref, tr_ref, a_slab, a_t, cs_t, ghead_ref, C_ref, n_ref, m_ref,
                     mask_st, heads)
        for h in range(N_HEADS))
    yield
    hmT = jnp.concatenate(heads, axis=0)

    u = tok_ref[:, U0:U0 + POOL_WIDTH]
    ext = jnp.concatenate([pb_ref[...], u], axis=0)
    pb_ref[...] = u[tm - POOL_CARRY:, :]
    pos = t * tm + lax.broadcasted_iota(jnp.int32, (tm, POOL_GROUP), 0)
    pooled = []
    for gi, win in enumerate(POOL_WINDOWS):
        cols = slice(gi * POOL_GROUP, (gi + 1) * POOL_GROUP)
        a = ext[:, cols]
        step = 1
        while step < win:
            a = a[step:, :] + a[:a.shape[0] - step, :]
            step *= 2
        a = a[a.shape[0] - tm:, :]
        cnt = jnp.minimum(win, pos + 1).astype(F32)
        p = a / cnt - u[:, cols]
        pooled.append((_dot(p.astype(BF16), wpool_ref[gi]) * pscale_ref[:, cols]).astype(BF16))
    yield
    pm = jnp.concatenate(pooled, axis=-1)
    out_ref[...] = (x_ref[...] + _dot_tn(hmT, wout_ref[:MLSTM_WIDTH, :])
                    + _dot(pm, wout_ref[MLSTM_WIDTH:, :]))
    yield


def _interleave(*streams):
    streams = list(streams)
    while streams:
        for s in list(streams):
            try:
                next(s)
            except StopIteration:
                streams.remove(s)


def _prompt_mixer_kernel(xcur_ref, xnext_ref, gmix_ref, wqk_ref, wu_ref, wgate_ref, wvo_ref,
                         bslab_ref, ghead_ref, wpool_ref, pscale_ref, wout_ref,
                         fg_ref, fu_ref, fd_ref,
                         xo_ref, C_ref, n_ref, m_ref, pb_ref, fgb_ref, fub_ref, fdb_ref,
                         tok_a, tr_a, hn_a, tok_b, tr_b, hn_b, wtr_s, *, tm, nt, tiles):
    fgb_ref[...] = fg_ref[...].astype(BF16)
    fub_ref[...] = fu_ref[...].astype(BF16)
    fdb_ref[...] = fd_ref[...].astype(BF16)
    s = pl.program_id(0)
    t0 = lax.rem(tiles * s, nt)
    proj_args = (wqk_ref, wu_ref, wgate_ref, wtr_s, bslab_ref)
    mix_args = (ghead_ref, wpool_ref, pscale_ref, wout_ref, C_ref, n_ref, m_ref, pb_ref)
    slots = ((tok_a, tr_a, hn_a), (tok_b, tr_b, hn_b))
    x_tile = lambda k: xcur_ref.at[k] if k < tiles else xnext_ref.at[k - tiles]

    @pl.when(s == 0)
    def _():
        wtr_s[...] = wvo_ref[...].T
        _norm_tile(x_tile(0), gmix_ref, hn_a)
        _interleave(_in_proj_items(hn_a, *proj_args, tok_a, tr_a))
        _norm_tile(x_tile(1), gmix_ref, hn_b)

    @pl.when(t0 == 0)
    def _():
        C_ref[...] = jnp.zeros_like(C_ref)
        n_ref[...] = jnp.zeros_like(n_ref)
        m_ref[...] = jnp.zeros_like(m_ref)
        pb_ref[...] = jnp.zeros_like(pb_ref)

    for k in range(tiles):
        tok, tr, hn = slots[k % 2]
        tok_n, tr_n, hn_n = slots[(k + 1) % 2]
        _interleave(
            _in_proj_items(hn_n, *proj_args, tok_n, tr_n,
                           then=functools.partial(_norm_tile, x_tile(k + 2), gmix_ref, hn)),
            _mix_items(x_tile(k), tok, tr, t0 + k, *mix_args, xo_ref.at[k], tm))


def _const_spec(shape):
    nd = len(shape)
    return pl.BlockSpec(shape, lambda *_: (0,) * nd, pipeline_mode=pl.Buffered(1))


def _layer_spec(a, l):
    nd = a.ndim - 1
    return pl.BlockSpec((None,) + a.shape[1:], lambda *_: (l,) + (0,) * nd,
                        pipeline_mode=pl.Buffered(1))


def _layer_cols_spec(a, l, width, j):
    return pl.BlockSpec((None, a.shape[1], width), lambda *_: (l, 0, j),
                        pipeline_mode=pl.Buffered(1))


def _in_hbm(*arrays):
    return [pltpu.with_memory_space_constraint(a, pltpu.MemorySpace.HBM) for a in arrays]


def _layer_bytes(*arrays):
    return sum(a.size // a.shape[0] * a.dtype.itemsize for a in arrays)


def _vmem_limit(nbytes):
    return int(min(V7X_VMEM_BYTES - VMEM_RESERVE_BYTES, max(nbytes, VMEM_FLOOR_BYTES)))


def _slab_rows(rows, nsteps):
    for steps_per_slab in (1, 2):
        per = rows * steps_per_slab // nsteps
        if per * nsteps == rows * steps_per_slab and per % N_ROWS == 0:
            return per, steps_per_slab
    raise ValueError(f"cannot split {rows} weight rows over {nsteps} grid steps")


def _prompt_mixer(x, l, gmix, w_b, wu, wgate, bslab, ghead_rep, wpool, pscale, wout,
                  ffn_w, *, tm, tiles):
    B, T, D = x.shape
    nt = T // tm
    assert tiles % 2 == 0 and nt % tiles == 0, "a grid step covers an even number of tiles of one sequence"
    ntiles = B * nt
    nsteps = ntiles // tiles
    xt = x.reshape(ntiles, tm, D)
    pair = pl.BlockSpec((tiles, tm, D), lambda s: (s, 0, 0))
    nxt = pl.BlockSpec((2, tm, D),
                       lambda s: (jnp.minimum((s + 1) * (tiles // 2), ntiles // 2 - 1), 0, 0))
    seq = lambda s: (tiles * s) // nt
    out_shapes = (
        jax.ShapeDtypeStruct((ntiles, tm, D), F32),
        jax.ShapeDtypeStruct((B, N_HEADS, HEAD_DIM, HEAD_DIM), F32),
        jax.ShapeDtypeStruct((B, N_HEADS, N_ROWS, HEAD_DIM), F32),
        jax.ShapeDtypeStruct((B, SUBLANES, LANES), F32),
        jax.ShapeDtypeStruct((B, POOL_CARRY, POOL_WIDTH), F32),
    )
    out_specs = (
        pair,
        pl.BlockSpec((None, N_HEADS, HEAD_DIM, HEAD_DIM), lambda s: (seq(s), 0, 0, 0)),
        pl.BlockSpec((None, N_HEADS, N_ROWS, HEAD_DIM), lambda s: (seq(s), 0, 0, 0)),
        pl.BlockSpec((None, SUBLANES, LANES), lambda s: (seq(s), 0, 0)),
        pl.BlockSpec((None, POOL_CARRY, POOL_WIDTH), lambda s: (seq(s), 0, 0)),
    )
    consts = (gmix, w_b, wu, wgate, w_b, bslab, ghead_rep, wpool, pscale, wout)
    in_specs = [pair, nxt] + [_layer_spec(a, l) for a in consts]
    in_specs[3] = _layer_cols_spec(w_b, l, 2 * MLSTM_WIDTH, 0)
    in_specs[6] = _layer_cols_spec(w_b, l, 2 * MLSTM_WIDTH, 1)
    slab_bytes = 0
    for w in ffn_w:
        rows, cols = w.shape[1:]
        per, stride = _slab_rows(rows, nsteps)
        slab_bytes += 2 * per * cols * (4 + 2)
        in_specs.append(pl.BlockSpec((None, per, cols), lambda s, k=stride: (l, s // k, 0)))
        out_specs += (pl.BlockSpec((per, cols), lambda s, k=stride: (s // k, 0)),)
        out_shapes += (jax.ShapeDtypeStruct((rows, cols), BF16),)
    est = (_layer_bytes(wu, wgate, wpool, wout) + 2 * D * TR_ROWS * 2 + slab_bytes
           + 6 * tiles * tm * D * 4 + 12 * tm * SAMPLE_COLS * 4)
    scratch = [pltpu.VMEM((tm, TOK_COLS), F32), pltpu.VMEM((TR_ROWS, tm), F32),
               pltpu.VMEM((tm, D), BF16)] * 2 + [pltpu.VMEM((TR_ROWS, D), BF16)]
    outs = pl.pallas_call(
        functools.partial(_prompt_mixer_kernel, tm=tm, nt=nt, tiles=tiles),
        grid=(nsteps,), in_specs=in_specs, out_specs=out_specs, out_shape=out_shapes,
        scratch_shapes=scratch,
        compiler_params=pltpu.CompilerParams(
            dimension_semantics=("arbitrary",),
            vmem_limit_bytes=_vmem_limit(est)),
        name="prompt_mixer",
    )(*_in_hbm(xt, xt, *consts, *ffn_w))
    return (outs[0].reshape(B, T, D),) + tuple(outs[1:])


def _ffn_tile(x, gffn_ref, wg_s, wu_s, wd_s, gfin_ref):
    h2 = _rms(x, gffn_ref[...]).astype(BF16)
    a = _dot(h2, wg_s[...])
    bu = _dot(h2, wu_s[...])
    act = (a * jax.nn.sigmoid(a) * bu).astype(BF16)
    y = x + _dot(act, wd_s[...])
    return y if gfin_ref is None else _rms(y, gfin_ref[...])


def _ffn_kernel(*refs, final, n_tiles):
    refs = list(refs)
    x_ref, xs_ref, mixs_ref, wout_ref, gffn_ref, wg_ref, wu_ref, wd_ref = refs[:8]
    refs = refs[8:]
    gfin_ref = refs.pop(0) if final else None
    out_ref, outs_ref = refs
    i = pl.program_id(0)
    weights = (gffn_ref, wg_ref, wu_ref, wd_ref, gfin_ref)

    @pl.when(i < n_tiles)
    def _():
        for r0 in range(0, x_ref.shape[0], FF_SUBTILE):
            rows = slice(r0, r0 + FF_SUBTILE)
            out_ref[rows, :] = _ffn_tile(x_ref[rows, :], *weights)

    @pl.when(i == n_tiles)
    def _():
        xs = xs_ref[...] + _dot(mixs_ref[...].astype(BF16), wout_ref[...])
        outs_ref[...] = _ffn_tile(xs, *weights)


def _ffn(x, xs, mixs, l, gffn, wg, wu, wd, wout, *, tm, gfin=None):
    M, D = x.shape
    dff = wg.shape[-1]
    n_tiles = M // tm
    assert n_tiles * tm == M
    tile = pl.BlockSpec((tm, D), lambda i: (jnp.minimum(i, n_tiles - 1), 0))
    args = [x, xs, mixs, wout, gffn, wg, wu, wd]
    in_specs = [tile, _const_spec(xs.shape), _const_spec(mixs.shape), _layer_spec(wout, l),
                _layer_spec(gffn, l)] + [_const_spec(w.shape) for w in (wg, wu, wd)]
    if gfin is not None:
        args.append(gfin)
        in_specs.append(_const_spec(gfin.shape))
    est = (3 * D * dff * 2 + _layer_bytes(wout) + 6 * xs.size * 4
           + 6 * tm * D * 4 + 4 * tm * dff * 4)
    return pl.pallas_call(
        functools.partial(_ffn_kernel, final=gfin is not None, n_tiles=n_tiles),
        grid=(n_tiles + 1,), in_specs=in_specs,
        out_specs=(tile, pl.BlockSpec(xs.shape, lambda i: (0, 0))),
        out_shape=(jax.ShapeDtypeStruct((M, D), F32), jax.ShapeDtypeStruct(xs.shape, F32)),
        compiler_params=pltpu.CompilerParams(
            dimension_semantics=("arbitrary",), vmem_limit_bytes=_vmem_limit(est)),
        name="ffn_final" if gfin is not None else "ffn",
    )(*_in_hbm(*args))


def _sample_proj_kernel(x_ref, gmix_ref, wqk_ref, wu_ref, wgate_ref, wvo_ref, bslab_ref, proj_ref):
    hn = _rms(x_ref[...], gmix_ref[...]).astype(BF16)
    proj_ref[:, Q0:U0] = _dot(hn, wqk_ref[...])
    proj_ref[:, U0:GI0] = _dot(hn, wu_ref[...])
    proj_ref[:, GI0:TOK_COLS] = _dot(hn, wgate_ref[...]) + bslab_ref[...]
    proj_ref[:, TOK_COLS:] = _dot(hn, wvo_ref[...])


def _sample_proj(x, l, gmix, w_b, wu, wgate, bslab):
    M = x.shape[0]
    consts = (gmix, w_b, wu, wgate, w_b, bslab)
    est = 2 * _layer_bytes(w_b, wu, wgate) + 8 * M * SAMPLE_COLS * 4
    return pl.pallas_call(
        _sample_proj_kernel,
        grid=(1,),
        in_specs=[_const_spec(x.shape), _layer_spec(gmix, l),
                  _layer_cols_spec(w_b, l, 2 * MLSTM_WIDTH, 0), _layer_spec(wu, l),
                  _layer_spec(wgate, l), _layer_cols_spec(w_b, l, 2 * MLSTM_WIDTH, 1),
                  _layer_spec(bslab, l)],
        out_specs=pl.BlockSpec((M, SAMPLE_COLS), lambda i: (0, 0)),
        out_shape=jax.ShapeDtypeStruct((M, SAMPLE_COLS), F32),
        compiler_params=pltpu.CompilerParams(
            dimension_semantics=("arbitrary",), vmem_limit_bytes=_vmem_limit(est)),
        name="sample_proj",
    )(*_in_hbm(x, *consts))


def _head_out(h, g_head, o):
    hn = h * lax.rsqrt(jnp.mean(h * h, axis=-1, keepdims=True) + EPS)
    return hn * g_head * jax.nn.sigmoid(o)


def _sample_step_kernel(proj_ref, ghead_ref, wpool_ref, pscale_ref, C_ref, n_ref, m_ref, buf_ref,
                        mix_ref, no_ref, mo_ref, bufo_ref, g_ref, wvt_ref, kb_ref,
                        qb_s, cq_s, *, bb):
    i = pl.program_id(0)
    nb = proj_ref.shape[0]
    scale = HEAD_DIM ** -0.5
    head = lambda c0, h: proj_ref[:, c0 + h * HEAD_DIM:c0 + (h + 1) * HEAD_DIM]

    def gate_terms(h):
        i_c = _lane_col(proj_ref[:, GI0:GI0 + GATE_SLAB], h)
        lf = _lane_col(_log_sigmoid(proj_ref[:, GF0:GF0 + GATE_SLAB]), h)
        m0 = _lane_col(m_ref[...], h)
        inter = lf + m0
        m_t = jnp.maximum(inter, i_c)
        return jnp.exp(i_c - m_t), jnp.exp(inter - m_t), m_t

    @pl.when(i == 0)
    def _():
        for h in range(N_HEADS):
            w, g, _ = gate_terms(h)
            g_ref[h] = jnp.broadcast_to(g, (nb, LANES))
            wvt_ref[h] = (w * head(SV0, h)).T
            kb_ref[h] = (head(K0, h) * scale).astype(BF16)
            qb_s[h] = head(Q0, h).astype(BF16)
            cq_s[h] = jnp.zeros((HEAD_DIM, nb), F32)

    lane = lax.broadcasted_iota(jnp.int32, (HEAD_DIM, nb), 1)

    for h in range(N_HEADS):
        r = _dot_nt(C_ref[:, h].reshape(bb * HEAD_DIM, HEAD_DIM).astype(BF16), qb_s[h])
        cq = cq_s[h]
        for j in range(bb):
            cq = jnp.where(lane == i * bb + j, r[j * HEAD_DIM:(j + 1) * HEAD_DIM], cq)
        cq_s[h] = cq

    @pl.when(i == pl.num_programs(0) - 1)
    def _():
        heads, n_heads = [], []
        lane_m = lax.broadcasted_iota(jnp.int32, (nb, LANES), 1)
        m_out = jnp.zeros((nb, LANES), F32)
        for h in range(N_HEADS):
            cols = slice(h * HEAD_DIM, (h + 1) * HEAD_DIM)
            w, g, m_t = gate_terms(h)
            qf = head(Q0, h)
            kf = head(K0, h) * scale
            vf = head(SV0, h)
            n0 = n_ref[:, cols]
            qk = jnp.sum(qf.astype(BF16).astype(F32) * kf.astype(BF16).astype(F32),
                         axis=-1, keepdims=True)
            s = qk * w
            num = s * vf + g * cq_s[h].T
            den = s + g * jnp.sum(n0 * qf, axis=-1, keepdims=True)
            hc = num * (1.0 / jnp.maximum(jnp.abs(den), jnp.exp(-m_t)))
            heads.append(_head_out(hc, ghead_ref[:, cols], head(SO0, h)))
            n_heads.append(g * n0 + w * kf)
            m_out = jnp.where(lane_m == h, m_t, m_out)
        no_ref[...] = jnp.concatenate(n_heads, axis=-1)
        mo_ref[...] = m_out

        u = proj_ref[:, U0:U0 + POOL_WIDTH]
        pooled = []
        for gi, win in enumerate(POOL_WINDOWS):
            cols = slice(gi * POOL_GROUP, (gi + 1) * POOL_GROUP)
            a = u[:, cols]
            for jrow in range(POOL_BUF - (win - 1), POOL_BUF):
                a = a + buf_ref[jrow, :, cols]
            p = a / float(win) - u[:, cols]
            pooled.append(_dot(p.astype(BF16), wpool_ref[gi]) * pscale_ref[:, cols])
        mix_ref[...] = jnp.concatenate(heads + pooled, axis=-1)
        for jrow in range(POOL_BUF - 1):
            bufo_ref[jrow] = buf_ref[jrow + 1]
        bufo_ref[POOL_BUF - 1] = u


def _sample_step(proj, l, ghead, wpool, pscale, C, n, m, buf_t, *, bb):
    nb = proj.shape[0]
    D = MLSTM_WIDTH + POOL_WIDTH
    cblk = pl.BlockSpec((None, bb, N_HEADS, HEAD_DIM, HEAD_DIM), lambda i: (l, i, 0, 0, 0))
    in_specs = [_const_spec(proj.shape)] + [_layer_spec(a, l) for a in (ghead, wpool, pscale)] + [
        cblk, _layer_spec(n, l), _layer_spec(m, l), _layer_spec(buf_t, l)]
    args = [proj, ghead, wpool, pscale, C, n, m, buf_t]
    shapes = ((nb, D), n.shape[1:], (nb, LANES), buf_t.shape[1:],
              (N_HEADS, nb, LANES), (N_HEADS, HEAD_DIM, nb), (N_HEADS, nb, HEAD_DIM))
    dtypes = (F32,) * 6 + (BF16,)
    out_shapes = tuple(jax.ShapeDtypeStruct(s, d) for s, d in zip(shapes, dtypes))
    out_specs = tuple(pl.BlockSpec(s, lambda i, k=len(s): (0,) * k) for s in shapes)
    scratch = [
        pltpu.VMEM((N_HEADS, nb, HEAD_DIM), BF16),
        pltpu.VMEM((N_HEADS, HEAD_DIM, nb), F32),
    ]
    est = (3 * bb * N_HEADS * HEAD_DIM * HEAD_DIM * 4 + 4 * _layer_bytes(buf_t)
           + 8 * proj.size * 4)
    return pl.pallas_call(
        functools.partial(_sample_step_kernel, bb=bb),
        grid=(nb // bb,), in_specs=in_specs, out_specs=out_specs, out_shape=out_shapes,
        scratch_shapes=scratch,
        compiler_params=pltpu.CompilerParams(
            dimension_semantics=("arbitrary",), vmem_limit_bytes=_vmem_limit(est)),
        name="sample_step",
    )(*_in_hbm(*args))


def _state_update_kernel(C_ref, g_ref, wvt_ref, kb_ref, Co_ref, *, bb):
    i = pl.program_id(1)
    nb = kb_ref.shape[1]
    lane = lax.broadcasted_iota(jnp.int32, (HEAD_DIM, nb), 1)
    for h in range(N_HEADS):
        lhs = [jnp.where(lane == i * bb + j, wvt_ref[h], 0.0).astype(BF16) for j in range(bb)]
        outer = _dot(jnp.concatenate(lhs, axis=0), kb_ref[h])
        for j in range(bb):
            Co_ref[j, h] = (g_ref[h, pl.ds(i * bb + j, 1), :] * C_ref[j, h]
                            + outer[j * HEAD_DIM:(j + 1) * HEAD_DIM])


def _state_update(C, g, wvt, kb, *, bb):
    depth, nb = C.shape[:2]
    cblk = pl.BlockSpec((None, bb, N_HEADS, HEAD_DIM, HEAD_DIM), lambda l, i: (l, i, 0, 0, 0))
    per_layer = lambda a: pl.BlockSpec((None,) + a.shape[1:], lambda l, i: (l, 0, 0, 0))
    est = 5 * bb * N_HEADS * HEAD_DIM * HEAD_DIM * 4 + 2 * _layer_bytes(g, wvt, kb)
    return pl.pallas_call(
        functools.partial(_state_update_kernel, bb=bb),
        grid=(depth, nb // bb),
        in_specs=[cblk, per_layer(g), per_layer(wvt), per_layer(kb)],
        out_specs=cblk, out_shape=jax.ShapeDtypeStruct(C.shape, F32),
        compiler_params=pltpu.CompilerParams(
            dimension_semantics=("arbitrary", "arbitrary"), vmem_limit_bytes=_vmem_limit(est)),
        name="state_update",
    )(*_in_hbm(C, g, wvt, kb))


def _pack_w_in(w_in, b_gate):
    depth, d, _ = w_in.shape
    g0 = 4 * MLSTM_WIDTH
    w_b = w_in.astype(BF16)
    wu = w_b[:, :, g0 + 2 * N_HEADS:]
    pad = jnp.zeros((depth, d, GATE_SLAB - N_HEADS), BF16)
    gates = w_b[:, :, g0:g0 + 2 * N_HEADS]
    wgate = jnp.concatenate([gates[:, :, :N_HEADS], pad, gates[:, :, N_HEADS:], pad], axis=-1)
    bpad = jnp.zeros((depth, GATE_SLAB - N_HEADS), b_gate.dtype)
    bslab = jnp.concatenate([b_gate[:, :N_HEADS], bpad, b_gate[:, N_HEADS:], bpad], axis=-1)
    return w_b, wu, wgate, bslab[:, None, :]


def kernel(x_prompt, x_sample, state_mlstm_C, state_mlstm_n, state_mlstm_m, state_pool_buf,
           g_mix, w_in, b_gate, g_head, w_pool, pool_scale, w_out, g_ffn, w_gate, w_up,
           w_down, g_final):
    depth = w_in.shape[0]
    B, T, D = x_prompt.shape
    nb = x_sample.shape[0]
    tm_mixer, mixer_tiles, tm_ffn, bb, bb_update = CHUNK, 4, 1024, 16, 32

    win = _pack_w_in(w_in, b_gate)
    wpool = w_pool.astype(BF16)
    wout = w_out.astype(BF16)
    ffn_w = (w_gate, w_up, w_down)
    row = lambda a: a[:, None, :]
    gmix, ghead, pscale, gffn = row(g_mix), row(g_head), row(pool_scale), row(g_ffn)
    ghead_rep = jnp.broadcast_to(g_head[:, :, None], g_head.shape + (CHUNK,))
    gfin = g_final[None, :]

    n_in = state_mlstm_n.reshape(depth, nb, MLSTM_WIDTH)
    m_pad = jnp.pad(state_mlstm_m, ((0, 0), (0, 0), (0, LANES - N_HEADS)))
    buf_t = jnp.transpose(state_pool_buf, (0, 2, 1, 3))

    xp = x_prompt
    xs = x_sample.reshape(nb, D)
    Cp, npr, mp, bp, ns, ms, bs, upd = [], [], [], [], [], [], [], []
    for l in range(depth):
        xm, C1, n1, m1, pb1, wg, wu, wd = _prompt_mixer(
            xp, l, gmix, *win, ghead_rep, wpool, pscale, wout, ffn_w, tm=tm_mixer,
            tiles=mixer_tiles)
        Cp.append(C1)
        npr.append(n1[:, :, 0, :])
        mp.append(m1[:, :N_HEADS, 0])
        bp.append(pb1[:, POOL_CARRY - POOL_BUF:, :])

        proj = _sample_proj(xs, l, gmix, *win)
        mix, n1, m1, buf1, *terms = _sample_step(
            proj, l, ghead, wpool, pscale, state_mlstm_C, n_in, m_pad, buf_t, bb=bb)
        ns.append(n1.reshape(nb, N_HEADS, HEAD_DIM))
        ms.append(m1[:, :N_HEADS])
        bs.append(jnp.transpose(buf1, (1, 0, 2)))
        upd.append(terms)

        xp, xs = _ffn(xm.reshape(B * T, D), xs, mix, l, gffn, wg, wu, wd, wout, tm=tm_ffn,
                      gfin=gfin if l == depth - 1 else None)
        xp = xp.reshape(B, T, D)
    y_prompt = xp
    y_sample = xs.reshape(nb, 1, D)
    C_sample = _state_update(state_mlstm_C, *(jnp.stack(t, 0) for t in zip(*upd)), bb=bb_update)

    st = lambda xs_: jnp.stack(xs_, 0)
    return (y_prompt, y_sample, st(Cp), st(npr), st(mp), st(bp),
            C_sample, st(ns), st(ms), st(bs))
```

```python
import functools

import jax
import jax.numpy as jnp
from jax import lax
from jax.experimental import pallas as pl
from jax.experimental.pallas import tpu as pltpu

F32 = jnp.float32
BF16 = jnp.bfloat16

EPS = 1e-6
N_HEADS = 4
HEAD_DIM = 128
MLSTM_WIDTH = N_HEADS * HEAD_DIM
POOL_WINDOWS = (2, 4, 8, 16)
POOL_GROUP = 128
POOL_WIDTH = POOL_GROUP * len(POOL_WINDOWS)
POOL_BUF = max(POOL_WINDOWS) - 1
POOL_CARRY = POOL_BUF + 1
CHUNK = 256
LANES = 128
FF_SUBTILE = 256
SUBLANES = 8
N_ROWS = 2 * SUBLANES
GATE_SLAB = LANES

Q0, K0, U0 = 0, MLSTM_WIDTH, 2 * MLSTM_WIDTH
GI0 = U0 + POOL_WIDTH
GF0 = GI0 + GATE_SLAB
TOK_COLS = GF0 + GATE_SLAB
VT0, OT0 = 0, MLSTM_WIDTH
TR_ROWS = 2 * MLSTM_WIDTH
SV0 = TOK_COLS
SO0 = TOK_COLS + MLSTM_WIDTH
SAMPLE_COLS = TOK_COLS + TR_ROWS

V7X_VMEM_BYTES = 64 * 1024 * 1024
VMEM_RESERVE_BYTES = 4 * 1024 * 1024
VMEM_FLOOR_BYTES = 16 * 1024 * 1024

_NT = (((1,), (1,)), ((), ()))
_TN = (((0,), (0,)), ((), ()))


def _dot(a, b):
    return jnp.dot(a, b, preferred_element_type=F32)


def _dot_nt(a, b):
    return lax.dot_general(a, b, _NT, preferred_element_type=F32)


def _dot_tn(a, b):
    return lax.dot_general(a, b, _TN, preferred_element_type=F32)


def _rms(x, g):
    return x * lax.rsqrt(jnp.mean(x * x, axis=-1, keepdims=True) + EPS) * g


def _log_sigmoid(x):
    return jnp.minimum(x, 0.0) - jnp.log1p(jnp.exp(-jnp.abs(x)))


def _lane_col(slab, j):
    lane = lax.broadcasted_iota(jnp.int32, slab.shape, 1)
    return jnp.sum(jnp.where(lane == j, slab, 0.0), axis=-1, keepdims=True)


def _split3(x):
    hi = x.astype(BF16)
    r1 = x - hi.astype(F32)
    mid = r1.astype(BF16)
    lo = (r1 - mid.astype(F32)).astype(BF16)
    return hi, mid, lo


def _cumsum_rows(x):
    r = lax.broadcasted_iota(jnp.int32, (CHUNK, CHUNK), 0)
    c = lax.broadcasted_iota(jnp.int32, (CHUNK, CHUNK), 1)
    tril = (r >= c).astype(BF16)
    hi, mid, lo = _split3(x)
    return _dot(tril, hi) + _dot(tril, mid) + _dot(tril, lo)


def _head_stages(h, tok_ref, tr_ref, a_slab, a_t, cs_t, ghead_ref, C_ref, n_ref, m_ref,
                 mask_st, out):
    cols = slice(h * HEAD_DIM, (h + 1) * HEAD_DIM)
    head = lambda c0: slice(c0 + h * HEAD_DIM, c0 + (h + 1) * HEAD_DIM)
    C, n_rows, m = C_ref[h], n_ref[h], m_ref[h:h + 1, 0:1]
    qb = tok_ref[:, head(Q0)].astype(BF16)
    kb = tok_ref[:, head(K0)].astype(BF16)
    by_q = _dot_nt(jnp.concatenate([kb, C.astype(BF16), n_rows.astype(BF16)], axis=0), qb)
    yield
    sT = by_q[:CHUNK]
    cqT = by_q[CHUNK:CHUNK + HEAD_DIM]
    qn = by_q[CHUNK + HEAD_DIM:CHUNK + HEAD_DIM + 1]
    a_row, b_row = a_t[h:h + 1, :], cs_t[h:h + 1, :]
    aT = jnp.where(mask_st, a_slab[:, h:h + 1], -jnp.inf)
    M = jnp.maximum(jnp.max(aT, axis=0, keepdims=True), m)
    swT = sT * jnp.exp(aT - M)
    g = jnp.exp(m - M)
    den = jnp.sum(swT, axis=0, keepdims=True) + g * qn
    inv = 1.0 / jnp.maximum(jnp.abs(den), jnp.exp(-(b_row + M)))
    vT = tr_ref[head(VT0), :]
    hT = (_dot(vT.astype(BF16), swT.astype(BF16)) + g * cqT) * inv
    hnT = hT * lax.rsqrt(jnp.mean(hT * hT, axis=0, keepdims=True) + EPS)
    out[h] = (hnT * ghead_ref[cols, :] * jax.nn.sigmoid(tr_ref[head(OT0), :])).astype(BF16)
    yield
    M_last = M[:, CHUNK - 1:CHUNK]
    g_end = jnp.exp(m - M_last)
    w_end = jnp.exp(a_row - M_last)
    w_rows = jnp.broadcast_to(w_end, (N_ROWS, CHUNK))
    by_k = _dot(jnp.concatenate([(vT * w_end).astype(BF16), w_rows.astype(BF16)], axis=0), kb)
    C_ref[h] = g_end * C + by_k[:HEAD_DIM]
    n_ref[h] = g_end * n_rows + by_k[HEAD_DIM:]
    m_ref[h:h + 1, :] = jnp.broadcast_to(b_row[:, CHUNK - 1:CHUNK] + M_last, (1, LANES))
    yield


def _skewed(streams):
    streams = list(streams)
    live = []
    while streams or live:
        if streams:
            live.append(streams.pop(0))
        for g in list(live):
            try:
                next(g)
            except StopIteration:
                live.remove(g)
        if streams or live:
            yield


def _norm_tile(x_ref, gmix_ref, hn_ref):
    hn_ref[...] = _rms(x_ref[...], gmix_ref[...]).astype(BF16)


def _in_proj_items(hn_ref, wqk_ref, wu_ref, wgate_ref, wtr_ref, bslab_ref, tok_ref, tr_ref,
                   then=None):
    tok_ref[:, GI0:GI0 + 2 * GATE_SLAB] = _dot(hn_ref[...], wgate_ref[...]) + bslab_ref[...]
    yield
    tok_ref[:, Q0:Q0 + MLSTM_WIDTH] = _dot(hn_ref[...], wqk_ref[:, :MLSTM_WIDTH])
    yield
    tok_ref[:, K0:K0 + MLSTM_WIDTH] = (_dot(hn_ref[...], wqk_ref[:, MLSTM_WIDTH:])
                                       * (HEAD_DIM ** -0.5))
    yield
    tok_ref[:, U0:U0 + POOL_WIDTH] = _dot(hn_ref[...], wu_ref[...])
    yield
    for r0 in (VT0, OT0):
        tr_ref[r0:r0 + MLSTM_WIDTH, :] = _dot_nt(wtr_ref[r0:r0 + MLSTM_WIDTH, :], hn_ref[...])
        yield
    if then is not None:
        then()
        yield


def _mix_items(x_ref, tok_ref, tr_ref, t, ghead_ref, wpool_ref, pscale_ref, wout_ref,
               C_ref, n_ref, m_ref, pb_ref, out_ref, tm):
    r = lax.broadcasted_iota(jnp.int32, (CHUNK, CHUNK), 0)
    c = lax.broadcasted_iota(jnp.int32, (CHUNK, CHUNK), 1)
    mask_st = r <= c

    assert tm == CHUNK, "one mLSTM chunk per tile"
    cs = _cumsum_rows(_log_sigmoid(tok_ref[:, GF0:GF0 + GATE_SLAB]))
    a_slab = tok_ref[:, GI0:GI0 + GATE_SLAB] - cs
    cs_t = cs.T
    a_t = a_slab.T
    yield
    heads = [None] * N_HEADS
    yield from _skewed(
        _head_stages(h, tok_ref, tr_ref, a_slab, a_t, cs_t, ghead_ref, C_ref, n_ref, m_ref,
                     mask_st, heads)
        for h in range(N_HEADS))
    yield
    hmT = jnp.concatenate(heads, axis=0)

    u = tok_ref[:, U0:U0 + POOL_WIDTH]
    ext = jnp.concatenate([pb_ref[...], u], axis=0)
    pb_ref[...] = u[tm - POOL_CARRY:, :]
    pos = t * tm + lax.broadcasted_iota(jnp.int32, (tm, POOL_GROUP), 0)
    pooled = []
    for gi, win in enumerate(POOL_WINDOWS):
        cols = slice(gi * POOL_GROUP, (gi + 1) * POOL_GROUP)
        a = ext[:, cols]
        step = 1
        while step < win:
            a = a[step:, :] + a[:a.shape[0] - step, :]
            step *= 2
        a = a[a.shape[0] - tm:, :]
        cnt = jnp.minimum(win, pos + 1).astype(F32)
        p = a / cnt - u[:, cols]
        pooled.append((_dot(p.astype(BF16), wpool_ref[gi]) * pscale_ref[:, cols]).astype(BF16))
    yield
    pm = jnp.concatenate(pooled, axis=-1)
    out_ref[...] = (x_ref[...] + _dot_tn(hmT, wout_ref[:MLSTM_WIDTH, :])
                    + _dot(pm, wout_ref[MLSTM_WIDTH:, :]))
    yield


def _interleave(*streams):
    streams = list(streams)
    while streams:
        for s in list(streams):
            try:
                next(s)
            except StopIteration:
                streams.remove(s)


def _prompt_mixer_kernel(xcur_ref, xnext_ref, gmix_ref, wqk_ref, wu_ref, wgate_ref, wvo_ref,
                         bslab_ref, ghead_ref, wpool_ref, pscale_ref, wout_ref,
                         fg_ref, fu_ref, fd_ref,
                         xo_ref, C_ref, n_ref, m_ref, pb_ref, fgb_ref, fub_ref, fdb_ref,
                         tok_a, tr_a, hn_a, tok_b, tr_b, hn_b, wtr_s, *, tm, nt, tiles):
    fgb_ref[...] = fg_ref[...].astype(BF16)
    fub_ref[...] = fu_ref[...].astype(BF16)
    fdb_ref[...] = fd_ref[...].astype(BF16)
    s = pl.program_id(0)
    t0 = lax.rem(tiles * s, nt)
    proj_args = (wqk_ref, wu_ref, wgate_ref, wtr_s, bslab_ref)
    mix_args = (ghead_ref, wpool_ref, pscale_ref, wout_ref, C_ref, n_ref, m_ref, pb_ref)
    slots = ((tok_a, tr_a, hn_a), (tok_b, tr_b, hn_b))
    x_tile = lambda k: xcur_ref.at[k] if k < tiles else xnext_ref.at[k - tiles]

    @pl.when(s == 0)
    def _():
        wtr_s[...] = wvo_ref[...].T
        _norm_tile(x_tile(0), gmix_ref, hn_a)
        _interleave(_in_proj_items(hn_a, *proj_args, tok_a, tr_a))
        _norm_tile(x_tile(1), gmix_ref, hn_b)

    @pl.when(t0 == 0)
    def _():
        C_ref[...] = jnp.zeros_like(C_ref)
        n_ref[...] = jnp.zeros_like(n_ref)
        m_ref[...] = jnp.zeros_like(m_ref)
        pb_ref[...] = jnp.zeros_like(pb_ref)

    for k in range(tiles):
        tok, tr, hn = slots[k % 2]
        tok_n, tr_n, hn_n = slots[(k + 1) % 2]
        _interleave(
            _in_proj_items(hn_n, *proj_args, tok_n, tr_n,
                           then=functools.partial(_norm_tile, x_tile(k + 2), gmix_ref, hn)),
            _mix_items(x_tile(k), tok, tr, t0 + k, *mix_args, xo_ref.at[k], tm))


def _const_spec(shape):
    nd = len(shape)
    return pl.BlockSpec(shape, lambda *_: (0,) * nd, pipeline_mode=pl.Buffered(1))


def _layer_spec(a, l):
    nd = a.ndim - 1
    return pl.BlockSpec((None,) + a.shape[1:], lambda *_: (l,) + (0,) * nd,
                        pipeline_mode=pl.Buffered(1))


def _layer_cols_spec(a, l, width, j):
    return pl.BlockSpec((None, a.shape[1], width), lambda *_: (l, 0, j),
                        pipeline_mode=pl.Buffered(1))


def _in_hbm(*arrays):
    return [pltpu.with_memory_space_constraint(a, pltpu.MemorySpace.HBM) for a in arrays]


def _layer_bytes(*arrays):
    return sum(a.size // a.shape[0] * a.dtype.itemsize for a in arrays)


def _vmem_limit(nbytes):
    return int(min(V7X_VMEM_BYTES - VMEM_RESERVE_BYTES, max(nbytes, VMEM_FLOOR_BYTES)))


def _slab_rows(rows, nsteps):
    for steps_per_slab in (1, 2):
        per = rows * steps_per_slab // nsteps
        if per * nsteps == rows * steps_per_slab and per % N_ROWS == 0:
            return per, steps_per_slab
    raise ValueError(f"cannot split {rows} weight rows over {nsteps} grid steps")


def _prompt_mixer(x, l, gmix, w_b, wu, wgate, bslab, ghead_rep, wpool, pscale, wout,
                  ffn_w, *, tm, tiles):
    B, T, D = x.shape
    nt = T // tm
    assert tiles % 2 == 0 and nt % tiles == 0, "a grid step covers an even number of tiles of one sequence"
    ntiles = B * nt
    nsteps = ntiles // tiles
    xt = x.reshape(ntiles, tm, D)
    pair = pl.BlockSpec((tiles, tm, D), lambda s: (s, 0, 0))
    nxt = pl.BlockSpec((2, tm, D),
                       lambda s: (jnp.minimum((s + 1) * (tiles // 2), ntiles // 2 - 1), 0, 0))
    seq = lambda s: (tiles * s) // nt
    out_shapes = (
        jax.ShapeDtypeStruct((ntiles, tm, D), F32),
        jax.ShapeDtypeStruct((B, N_HEADS, HEAD_DIM, HEAD_DIM), F32),
        jax.ShapeDtypeStruct((B, N_HEADS, N_ROWS, HEAD_DIM), F32),
        jax.ShapeDtypeStruct((B, SUBLANES, LANES), F32),
        jax.ShapeDtypeStruct((B, POOL_CARRY, POOL_WIDTH), F32),
    )
    out_specs = (
        pair,
        pl.BlockSpec((None, N_HEADS, HEAD_DIM, HEAD_DIM), lambda s: (seq(s), 0, 0, 0)),
        pl.BlockSpec((None, N_HEADS, N_ROWS, HEAD_DIM), lambda s: (seq(s), 0, 0, 0)),
        pl.BlockSpec((None, SUBLANES, LANES), lambda s: (seq(s), 0, 0)),
        pl.BlockSpec((None, POOL_CARRY, POOL_WIDTH), lambda s: (seq(s), 0, 0)),
    )
    consts = (gmix, w_b, wu, wgate, w_b, bslab, ghead_rep, wpool, pscale, wout)
    in_specs = [pair, nxt] + [_layer_spec(a, l) for a in consts]
    in_specs[3] = _layer_cols_spec(w_b, l, 2 * MLSTM_WIDTH, 0)
    in_specs[6] = _layer_cols_spec(w_b, l, 2 * MLSTM_WIDTH, 1)
    slab_bytes = 0
    for w in ffn_w:
        rows, cols = w.shape[1:]
        per, stride = _slab_rows(rows, nsteps)
        slab_bytes += 2 * per * cols * (4 + 2)
        in_specs.append(pl.BlockSpec((None, per, cols), lambda s, k=stride: (l, s // k, 0)))
        out_specs += (pl.BlockSpec((per, cols), lambda s, k=stride: (s // k, 0)),)
        out_shapes += (jax.ShapeDtypeStruct((rows, cols), BF16),)
    est = (_layer_bytes(wu, wgate, wpool, wout) + 2 * D * TR_ROWS * 2 + slab_bytes
           + 6 * tiles * tm * D * 4 + 12 * tm * SAMPLE_COLS * 4)
    scratch = [pltpu.VMEM((tm, TOK_COLS), F32), pltpu.VMEM((TR_ROWS, tm), F32),
               pltpu.VMEM((tm, D), BF16)] * 2 + [pltpu.VMEM((TR_ROWS, D), BF16)]
    outs = pl.pallas_call(
        functools.partial(_prompt_mixer_kernel, tm=tm, nt=nt, tiles=tiles),
        grid=(nsteps,), in_specs=in_specs, out_specs=out_specs, out_shape=out_shapes,
        scratch_shapes=scratch,
        compiler_params=pltpu.CompilerParams(
            dimension_semantics=("arbitrary",),
            vmem_limit_bytes=_vmem_limit(est)),
        name="prompt_mixer",
    )(*_in_hbm(xt, xt, *consts, *ffn_w))
    return (outs[0].reshape(B, T, D),) + tuple(outs[1:])


def _ffn_tile(x, gffn_ref, wg_s, wu_s, wd_s, gfin_ref):
    h2 = _rms(x, gffn_ref[...]).astype(BF16)
    a = _dot(h2, wg_s[...])
    bu = _dot(h2, wu_s[...])
    act = (a * jax.nn.sigmoid(a) * bu).astype(BF16)
    y = x + _dot(act, wd_s[...])
    return y if gfin_ref is None else _rms(y, gfin_ref[...])


def _state_update_block(C_ref, g_ref, wvt_ref, k_ref, Co_ref, first_seq):
    nb = k_ref.shape[1]
    lane = lax.broadcasted_iota(jnp.int32, (HEAD_DIM, nb), 1)
    for h in range(N_HEADS):
        wvt = wvt_ref[h]
        for j in range(C_ref.shape[0]):
            b = first_seq + j
            col = jnp.sum(jnp.where(lane == b, wvt, 0.0), axis=-1, keepdims=True)
            k_row = k_ref[h, pl.ds(b, 1), :]
            Co_ref[j, h] = g_ref[h, pl.ds(b, 1), :] * C_ref[j, h] + col * k_row


def _ffn_kernel(*refs, final, n_tiles, upd_blocks):
    refs = list(refs)
    x_ref, xs_ref, mixs_ref, wout_ref, gffn_ref, wg_ref, wu_ref, wd_ref = refs[:8]
    refs = refs[8:]
    gfin_ref = refs.pop(0) if final else None
    if upd_blocks:
        C_ref, g_ref, wvt_ref, kb_ref = refs[:4]
        refs = refs[4:]
        out_ref, outs_ref, Co_ref = refs
    else:
        out_ref, outs_ref = refs
    i = pl.program_id(0)
    weights = (gffn_ref, wg_ref, wu_ref, wd_ref, gfin_ref)

    @pl.when(i < n_tiles)
    def _():
        for r0 in range(0, x_ref.shape[0], FF_SUBTILE):
            rows = slice(r0, r0 + FF_SUBTILE)
            out_ref[rows, :] = _ffn_tile(x_ref[rows, :], *weights)
        if upd_blocks:
            _state_update_block(C_ref, g_ref, wvt_ref, kb_ref, Co_ref,
                                lax.rem(i, upd_blocks) * C_ref.shape[0])

    @pl.when(i == n_tiles)
    def _():
        xs = xs_ref[...] + _dot(mixs_ref[...].astype(BF16), wout_ref[...])
        outs_ref[...] = _ffn_tile(xs, *weights)


def _ffn(x, xs, mixs, l, gffn, wg, wu, wd, wout, *, tm, gfin=None, state_update=None):
    M, D = x.shape
    dff = wg.shape[-1]
    n_tiles = M // tm
    assert n_tiles * tm == M
    last = lambda i: jnp.minimum(i, n_tiles - 1)
    tile = pl.BlockSpec((tm, D), lambda i: (last(i), 0))
    args = [x, xs, mixs, wout, gffn, wg, wu, wd]
    in_specs = [tile, _const_spec(xs.shape), _const_spec(mixs.shape), _layer_spec(wout, l),
                _layer_spec(gffn, l)] + [_const_spec(w.shape) for w in (wg, wu, wd)]
    if gfin is not None:
        args.append(gfin)
        in_specs.append(_const_spec(gfin.shape))
    out_specs = [tile, pl.BlockSpec(xs.shape, lambda i: (0, 0))]
    out_shapes = [jax.ShapeDtypeStruct((M, D), F32), jax.ShapeDtypeStruct(xs.shape, F32)]
    est = (3 * D * dff * 2 + _layer_bytes(wout) + 6 * xs.size * 4
           + 6 * tm * D * 4 + 4 * FF_SUBTILE * dff * 4)
    upd_blocks = 0
    if state_update is not None:
        C = state_update[0]
        depth, nb = C.shape[:2]
        assert n_tiles % depth == 0, "the state blocks of all layers are spread over the tile steps"
        upd_blocks = n_tiles // depth
        bbu = nb // upd_blocks
        assert bbu * upd_blocks == nb
        cblk = pl.BlockSpec((None, bbu) + C.shape[2:],
                            lambda i: (last(i) // upd_blocks, last(i) % upd_blocks, 0, 0, 0))
        per_layer = lambda a: pl.BlockSpec((None,) + a.shape[1:],
                                           lambda i: (last(i) // upd_blocks, 0, 0, 0))
        args += list(state_update)
        in_specs += [cblk] + [per_layer(a) for a in state_update[1:]]
        out_specs.append(cblk)
        out_shapes.append(jax.ShapeDtypeStruct(C.shape, F32))
        est += 4 * bbu * C[0, 0].size * 4 + 2 * _layer_bytes(*state_update[1:])
    return pl.pallas_call(
        functools.partial(_ffn_kernel, final=gfin is not None, n_tiles=n_tiles,
                          upd_blocks=upd_blocks),
        grid=(n_tiles + 1,), in_specs=in_specs, out_specs=tuple(out_specs),
        out_shape=tuple(out_shapes),
        compiler_params=pltpu.CompilerParams(
            dimension_semantics=("arbitrary",), vmem_limit_bytes=_vmem_limit(est)),
        name="ffn_final" if gfin is not None else "ffn",
    )(*_in_hbm(*args))


def _sample_proj_kernel(x_ref, gmix_ref, wqk_ref, wu_ref, wgate_ref, wvo_ref, bslab_ref, proj_ref):
    hn = _rms(x_ref[...], gmix_ref[...]).astype(BF16)
    proj_ref[:, Q0:U0] = _dot(hn, wqk_ref[...])
    proj_ref[:, U0:GI0] = _dot(hn, wu_ref[...])
    proj_ref[:, GI0:TOK_COLS] = _dot(hn, wgate_ref[...]) + bslab_ref[...]
    proj_ref[:, TOK_COLS:] = _dot(hn, wvo_ref[...])


def _sample_proj(x, l, gmix, w_b, wu, wgate, bslab):
    M = x.shape[0]
    consts = (gmix, w_b, wu, wgate, w_b, bslab)
    est = 2 * _layer_bytes(w_b, wu, wgate) + 8 * M * SAMPLE_COLS * 4
    return pl.pallas_call(
        _sample_proj_kernel,
        grid=(1,),
        in_specs=[_const_spec(x.shape), _layer_spec(gmix, l),
                  _layer_cols_spec(w_b, l, 2 * MLSTM_WIDTH, 0), _layer_spec(wu, l),
                  _layer_spec(wgate, l), _layer_cols_spec(w_b, l, 2 * MLSTM_WIDTH, 1),
                  _layer_spec(bslab, l)],
        out_specs=pl.BlockSpec((M, SAMPLE_COLS), lambda i: (0, 0)),
        out_shape=jax.ShapeDtypeStruct((M, SAMPLE_COLS), F32),
        compiler_params=pltpu.CompilerParams(
            dimension_semantics=("arbitrary",), vmem_limit_bytes=_vmem_limit(est)),
        name="sample_proj",
    )(*_in_hbm(x, *consts))


def _head_out(h, g_head, o):
    hn = h * lax.rsqrt(jnp.mean(h * h, axis=-1, keepdims=True) + EPS)
    return hn * g_head * jax.nn.sigmoid(o)


def _sample_step_kernel(proj_ref, ghead_ref, wpool_ref, pscale_ref, C_ref, n_ref, m_ref, buf_ref,
                        mix_ref, no_ref, mo_ref, bufo_ref, g_ref, wvt_ref, kb_ref,
                        qb_s, cq_s, *, bb):
    i = pl.program_id(0)
    nb = proj_ref.shape[0]
    scale = HEAD_DIM ** -0.5
    head = lambda c0, h: proj_ref[:, c0 + h * HEAD_DIM:c0 + (h + 1) * HEAD_DIM]

    def gate_terms(h):
        i_c = _lane_col(proj_ref[:, GI0:GI0 + GATE_SLAB], h)
        lf = _lane_col(_log_sigmoid(proj_ref[:, GF0:GF0 + GATE_SLAB]), h)
        m0 = _lane_col(m_ref[...], h)
        inter = lf + m0
        m_t = jnp.maximum(inter, i_c)
        return jnp.exp(i_c - m_t), jnp.exp(inter - m_t), m_t

    @pl.when(i == 0)
    def _():
        for h in range(N_HEADS):
            w, g, _ = gate_terms(h)
            g_ref[h] = jnp.broadcast_to(g, (nb, LANES))
            wvt_ref[h] = (w * head(SV0, h)).T
            kb_ref[h] = (head(K0, h) * scale).astype(BF16)
            qb_s[h] = head(Q0, h).astype(BF16)
            cq_s[h] = jnp.zeros((HEAD_DIM, nb), F32)

    lane = lax.broadcasted_iota(jnp.int32, (HEAD_DIM, nb), 1)

    for h in range(N_HEADS):
        r = _dot_nt(C_ref[:, h].reshape(bb * HEAD_DIM, HEAD_DIM).astype(BF16), qb_s[h])
        cq = cq_s[h]
        for j in range(bb):
            cq = jnp.where(lane == i * bb + j, r[j * HEAD_DIM:(j + 1) * HEAD_DIM], cq)
        cq_s[h] = cq

    @pl.when(i == pl.num_programs(0) - 1)
    def _():
        heads, n_heads = [], []
        lane_m = lax.broadcasted_iota(jnp.int32, (nb, LANES), 1)
        m_out = jnp.zeros((nb, LANES), F32)
        for h in range(N_HEADS):
            cols = slice(h * HEAD_DIM, (h + 1) * HEAD_DIM)
            w, g, m_t = gate_terms(h)
            qf = head(Q0, h)
            kf = head(K0, h) * scale
            vf = head(SV0, h)
            n0 = n_ref[:, cols]
            qk = jnp.sum(qf.astype(BF16).astype(F32) * kf.astype(BF16).astype(F32),
                         axis=-1, keepdims=True)
            s = qk * w
            num = s * vf + g * cq_s[h].T
            den = s + g * jnp.sum(n0 * qf, axis=-1, keepdims=True)
            hc = num * (1.0 / jnp.maximum(jnp.abs(den), jnp.exp(-m_t)))
            heads.append(_head_out(hc, ghead_ref[:, cols], head(SO0, h)))
            n_heads.append(g * n0 + w * kf)
            m_out = jnp.where(lane_m == h, m_t, m_out)
        no_ref[...] = jnp.concatenate(n_heads, axis=-1)
        mo_ref[...] = m_out

        u = proj_ref[:, U0:U0 + POOL_WIDTH]
        pooled = []
        for gi, win in enumerate(POOL_WINDOWS):
            cols = slice(gi * POOL_GROUP, (gi + 1) * POOL_GROUP)
            a = u[:, cols]
            for jrow in range(POOL_BUF - (win - 1), POOL_BUF):
                a = a + buf_ref[jrow, :, cols]
            p = a / float(win) - u[:, cols]
            pooled.append(_dot(p.astype(BF16), wpool_ref[gi]) * pscale_ref[:, cols])
        mix_ref[...] = jnp.concatenate(heads + pooled, axis=-1)
        for jrow in range(POOL_BUF - 1):
            bufo_ref[jrow] = buf_ref[jrow + 1]
        bufo_ref[POOL_BUF - 1] = u


def _sample_step(proj, l, ghead, wpool, pscale, C, n, m, buf_t, *, bb):
    nb = proj.shape[0]
    D = MLSTM_WIDTH + POOL_WIDTH
    cblk = pl.BlockSpec((None, bb, N_HEADS, HEAD_DIM, HEAD_DIM), lambda i: (l, i, 0, 0, 0))
    in_specs = [_const_spec(proj.shape)] + [_layer_spec(a, l) for a in (ghead, wpool, pscale)] + [
        cblk, _layer_spec(n, l), _layer_spec(m, l), _layer_spec(buf_t, l)]
    args = [proj, ghead, wpool, pscale, C, n, m, buf_t]
    shapes = ((nb, D), n.shape[1:], (nb, LANES), buf_t.shape[1:],
              (N_HEADS, nb, LANES), (N_HEADS, HEAD_DIM, nb), (N_HEADS, nb, HEAD_DIM))
    dtypes = (F32,) * 6 + (BF16,)
    out_shapes = tuple(jax.ShapeDtypeStruct(s, d) for s, d in zip(shapes, dtypes))
    out_specs = tuple(pl.BlockSpec(s, lambda i, k=len(s): (0,) * k) for s in shapes)
    scratch = [
        pltpu.VMEM((N_HEADS, nb, HEAD_DIM), BF16),
        pltpu.VMEM((N_HEADS, HEAD_DIM, nb), F32),
    ]
    est = (3 * bb * N_HEADS * HEAD_DIM * HEAD_DIM * 4 + 4 * _layer_bytes(buf_t)
           + 8 * proj.size * 4)
    return pl.pallas_call(
        functools.partial(_sample_step_kernel, bb=bb),
        grid=(nb // bb,), in_specs=in_specs, out_specs=out_specs, out_shape=out_shapes,
        scratch_shapes=scratch,
        compiler_params=pltpu.CompilerParams(
            dimension_semantics=("arbitrary",), vmem_limit_bytes=_vmem_limit(est)),
        name="sample_step",
    )(*_in_hbm(*args))


def _pack_w_in(w_in, b_gate):
    depth, d, _ = w_in.shape
    g0 = 4 * MLSTM_WIDTH
    w_b = w_in.astype(BF16)
    wu = w_b[:, :, g0 + 2 * N_HEADS:]
    pad = jnp.zeros((depth, d, GATE_SLAB - N_HEADS), BF16)
    gates = w_b[:, :, g0:g0 + 2 * N_HEADS]
    wgate = jnp.concatenate([gates[:, :, :N_HEADS], pad, gates[:, :, N_HEADS:], pad], axis=-1)
    bpad = jnp.zeros((depth, GATE_SLAB - N_HEADS), b_gate.dtype)
    bslab = jnp.concatenate([b_gate[:, :N_HEADS], bpad, b_gate[:, N_HEADS:], bpad], axis=-1)
    return w_b, wu, wgate, bslab[:, None, :]


def kernel(x_prompt, x_sample, state_mlstm_C, state_mlstm_n, state_mlstm_m, state_pool_buf,
           g_mix, w_in, b_gate, g_head, w_pool, pool_scale, w_out, g_ffn, w_gate, w_up,
           w_down, g_final):
    depth = w_in.shape[0]
    B, T, D = x_prompt.shape
    nb = x_sample.shape[0]
    tm_mixer, mixer_tiles, tm_ffn, tm_ffn_last, bb = CHUNK, 4, 1024, 512, 16

    win = _pack_w_in(w_in, b_gate)
    wpool = w_pool.astype(BF16)
    wout = w_out.astype(BF16)
    ffn_w = (w_gate, w_up, w_down)
    row = lambda a: a[:, None, :]
    gmix, ghead, pscale, gffn = row(g_mix), row(g_head), row(pool_scale), row(g_ffn)
    ghead_rep = jnp.broadcast_to(g_head[:, :, None], g_head.shape + (CHUNK,))
    gfin = g_final[None, :]

    n_in = state_mlstm_n.reshape(depth, nb, MLSTM_WIDTH)
    m_pad = jnp.pad(state_mlstm_m, ((0, 0), (0, 0), (0, LANES - N_HEADS)))
    buf_t = jnp.transpose(state_pool_buf, (0, 2, 1, 3))

    xp = x_prompt
    xs = x_sample.reshape(nb, D)
    Cp, npr, mp, bp, ns, ms, bs, upd = [], [], [], [], [], [], [], []
    for l in range(depth):
        xm, C1, n1, m1, pb1, wg, wu, wd = _prompt_mixer(
            xp, l, gmix, *win, ghead_rep, wpool, pscale, wout, ffn_w, tm=tm_mixer,
            tiles=mixer_tiles)
        Cp.append(C1)
        npr.append(n1[:, :, 0, :])
        mp.append(m1[:, :N_HEADS, 0])
        bp.append(pb1[:, POOL_CARRY - POOL_BUF:, :])

        proj = _sample_proj(xs, l, gmix, *win)
        mix, n1, m1, buf1, *terms = _sample_step(
            proj, l, ghead, wpool, pscale, state_mlstm_C, n_in, m_pad, buf_t, bb=bb)
        ns.append(n1.reshape(nb, N_HEADS, HEAD_DIM))
        ms.append(m1[:, :N_HEADS])
        bs.append(jnp.transpose(buf1, (1, 0, 2)))
        upd.append(terms)

        last = l == depth - 1
        if last:
            g_all, wvt_all, kb_all = (jnp.stack(t, 0) for t in zip(*upd))
            terms = (g_all, wvt_all, kb_all.astype(F32))
        xp, xs, *C_sample = _ffn(
            xm.reshape(B * T, D), xs, mix, l, gffn, wg, wu, wd, wout,
            tm=tm_ffn_last if last else tm_ffn, gfin=gfin if last else None,
            state_update=(state_mlstm_C,) + terms if last else None)
        xp = xp.reshape(B, T, D)
    y_prompt = xp
    y_sample = xs.reshape(nb, 1, D)
    (C_sample,) = C_sample

    st = lambda xs_: jnp.stack(xs_, 0)
    return (y_prompt, y_sample, st(Cp), st(npr), st(mp), st(bp),
            C_sample, st(ns), st(ms), st(bs))
```

```python
import functools

import jax
import jax.numpy as jnp
from jax import lax
from jax.experimental import pallas as pl
from jax.experimental.pallas import tpu as pltpu

F32 = jnp.float32
BF16 = jnp.bfloat16

EPS = 1e-6
N_HEADS = 4
HEAD_DIM = 128
MLSTM_WIDTH = N_HEADS * HEAD_DIM
POOL_WINDOWS = (2, 4, 8, 16)
POOL_GROUP = 128
POOL_WIDTH = POOL_GROUP * len(POOL_WINDOWS)
POOL_BUF = max(POOL_WINDOWS) - 1
POOL_CARRY = POOL_BUF + 1
CHUNK = 256
LANES = 128
FF_SUBTILE = 256
SUBLANES = 8
N_ROWS = 2 * SUBLANES
GATE_SLAB = LANES

Q0, K0, U0 = 0, MLSTM_WIDTH, 2 * MLSTM_WIDTH
GI0 = U0 + POOL_WIDTH
GF0 = GI0 + GATE_SLAB
TOK_COLS = GF0 + GATE_SLAB
VT0, OT0 = 0, MLSTM_WIDTH
TR_ROWS = 2 * MLSTM_WIDTH
SV0 = TOK_COLS
SO0 = TOK_COLS + MLSTM_WIDTH
SAMPLE_COLS = TOK_COLS + TR_ROWS

V7X_VMEM_BYTES = 64 * 1024 * 1024
VMEM_RESERVE_BYTES = 4 * 1024 * 1024
VMEM_FLOOR_BYTES = 16 * 1024 * 1024

_NT = (((1,), (1,)), ((), ()))
_TN = (((0,), (0,)), ((), ()))


def _dot(a, b):
    return jnp.dot(a, b, preferred_element_type=F32)


def _dot_nt(a, b):
    return lax.dot_general(a, b, _NT, preferred_element_type=F32)


def _dot_tn(a, b):
    return lax.dot_general(a, b, _TN, preferred_element_type=F32)


def _rms(x, g):
    return x * lax.rsqrt(jnp.mean(x * x, axis=-1, keepdims=True) + EPS) * g


def _log_sigmoid(x):
    return jnp.minimum(x, 0.0) - jnp.log1p(jnp.exp(-jnp.abs(x)))


def _lane_col(slab, j):
    lane = lax.broadcasted_iota(jnp.int32, slab.shape, 1)
    return jnp.sum(jnp.where(lane == j, slab, 0.0), axis=-1, keepdims=True)


def _split3(x):
    hi = x.astype(BF16)
    r1 = x - hi.astype(F32)
    mid = r1.astype(BF16)
    lo = (r1 - mid.astype(F32)).astype(BF16)
    return hi, mid, lo


def _cumsum_rows(x):
    r = lax.broadcasted_iota(jnp.int32, (CHUNK, CHUNK), 0)
    c = lax.broadcasted_iota(jnp.int32, (CHUNK, CHUNK), 1)
    tril = (r >= c).astype(BF16)
    hi, mid, lo = _split3(x)
    return _dot(tril, hi) + _dot(tril, mid) + _dot(tril, lo)


def _head_stages(h, tok_ref, tr_ref, a_slab, a_t, cs_t, ghead_ref, C_ref, n_ref, m_ref,
                 mask_st, out):
    cols = slice(h * HEAD_DIM, (h + 1) * HEAD_DIM)
    head = lambda c0: slice(c0 + h * HEAD_DIM, c0 + (h + 1) * HEAD_DIM)
    C, n_rows, m = C_ref[h], n_ref[h], m_ref[h:h + 1, 0:1]
    qb = tok_ref[:, head(Q0)].astype(BF16)
    kb = tok_ref[:, head(K0)].astype(BF16)
    by_q = _dot_nt(jnp.concatenate([kb, C.astype(BF16), n_rows.astype(BF16)], axis=0), qb)
    yield
    sT = by_q[:CHUNK]
    cqT = by_q[CHUNK:CHUNK + HEAD_DIM]
    qn = by_q[CHUNK + HEAD_DIM:CHUNK + HEAD_DIM + 1]
    a_row, b_row = a_t[h:h + 1, :], cs_t[h:h + 1, :]
    aT = jnp.where(mask_st, a_slab[:, h:h + 1], -jnp.inf)
    M = jnp.maximum(jnp.max(aT, axis=0, keepdims=True), m)
    swT = sT * jnp.exp(aT - M)
    g = jnp.exp(m - M)
    den = jnp.sum(swT, axis=0, keepdims=True) + g * qn
    inv = 1.0 / jnp.maximum(jnp.abs(den), jnp.exp(-(b_row + M)))
    vT = tr_ref[head(VT0), :]
    hT = (_dot(vT.astype(BF16), swT.astype(BF16)) + g * cqT) * inv
    hnT = hT * lax.rsqrt(jnp.mean(hT * hT, axis=0, keepdims=True) + EPS)
    out[h] = (hnT * ghead_ref[cols, :] * jax.nn.sigmoid(tr_ref[head(OT0), :])).astype(BF16)
    yield
    M_last = M[:, CHUNK - 1:CHUNK]
    g_end = jnp.exp(m - M_last)
    w_end = jnp.exp(a_row - M_last)
    w_rows = jnp.broadcast_to(w_end, (N_ROWS, CHUNK))
    by_k = _dot(jnp.concatenate([(vT * w_end).astype(BF16), w_rows.astype(BF16)], axis=0), kb)
    C_ref[h] = g_end * C + by_k[:HEAD_DIM]
    n_ref[h] = g_end * n_rows + by_k[HEAD_DIM:]
    m_ref[h:h + 1, :] = jnp.broadcast_to(b_row[:, CHUNK - 1:CHUNK] + M_last, (1, LANES))
    yield


def _skewed(streams):
    streams = list(streams)
    live = []
    while streams or live:
        if streams:
            live.append(streams.pop(0))
        for g in list(live):
            try:
                next(g)
            except StopIteration:
                live.remove(g)
        if streams or live:
            yield


def _norm_tile(x_ref, gmix_ref, hn_ref):
    hn_ref[...] = _rms(x_ref[...], gmix_ref[...]).astype(BF16)


def _in_proj_items(hn_ref, wqk_ref, wu_ref, wgate_ref, wtr_ref, bslab_ref, tok_ref, tr_ref,
                   then=None):
    tok_ref[:, GI0:GI0 + 2 * GATE_SLAB] = _dot(hn_ref[...], wgate_ref[...]) + bslab_ref[...]
    yield
    tok_ref[:, Q0:Q0 + MLSTM_WIDTH] = _dot(hn_ref[...], wqk_ref[:, :MLSTM_WIDTH])
    yield
    tok_ref[:, K0:K0 + MLSTM_WIDTH] = (_dot(hn_ref[...], wqk_ref[:, MLSTM_WIDTH:])
                                       * (HEAD_DIM ** -0.5))
    yield
    tok_ref[:, U0:U0 + POOL_WIDTH] = _dot(hn_ref[...], wu_ref[...])
    yield
    for r0 in (VT0, OT0):
        tr_ref[r0:r0 + MLSTM_WIDTH, :] = _dot_nt(wtr_ref[r0:r0 + MLSTM_WIDTH, :], hn_ref[...])
        yield
    if then is not None:
        then()
        yield


def _mix_items(x_ref, tok_ref, tr_ref, t, ghead_ref, wpool_ref, pscale_ref, wout_ref,
               C_ref, n_ref, m_ref, pb_ref, out_ref, tm):
    r = lax.broadcasted_iota(jnp.int32, (CHUNK, CHUNK), 0)
    c = lax.broadcasted_iota(jnp.int32, (CHUNK, CHUNK), 1)
    mask_st = r <= c

    assert tm == CHUNK, "one mLSTM chunk per tile"
    cs = _cumsum_rows(_log_sigmoid(tok_ref[:, GF0:GF0 + GATE_SLAB]))
    a_slab = tok_ref[:, GI0:GI0 + GATE_SLAB] - cs
    cs_t = cs.T
    a_t = a_slab.T
    yield
    heads = [None] * N_HEADS
    yield from _skewed(
        _head_stages(h, tok_ref, tr_ref, a_slab, a_t, cs_t, ghead_ref, C_ref, n_ref, m_ref,
                     mask_st, heads)
        for h in range(N_HEADS))
    yield
    hmT = jnp.concatenate(heads, axis=0)

    u = tok_ref[:, U0:U0 + POOL_WIDTH]
    ext = jnp.concatenate([pb_ref[...], u], axis=0)
    pb_ref[...] = u[tm - POOL_CARRY:, :]
    pos = t * tm + lax.broadcasted_iota(jnp.int32, (tm, POOL_GROUP), 0)
    pooled = []
    for gi, win in enumerate(POOL_WINDOWS):
        cols = slice(gi * POOL_GROUP, (gi + 1) * POOL_GROUP)
        a = ext[:, cols]
        step = 1
        while step < win:
            a = a[step:, :] + a[:a.shape[0] - step, :]
            step *= 2
        a = a[a.shape[0] - tm:, :]
        cnt = jnp.minimum(win, pos + 1).astype(F32)
        p = a / cnt - u[:, cols]
        pooled.append((_dot(p.astype(BF16), wpool_ref[gi]) * pscale_ref[:, cols]).astype(BF16))
    yield
    pm = jnp.concatenate(pooled, axis=-1)
    out_ref[...] = (x_ref[...] + _dot_tn(hmT, wout_ref[:MLSTM_WIDTH, :])
                    + _dot(pm, wout_ref[MLSTM_WIDTH:, :]))
    yield


def _interleave(*streams):
    streams = list(streams)
    while streams:
        for s in list(streams):
            try:
                next(s)
            except StopIteration:
                streams.remove(s)


def _prompt_mixer_kernel(xcur_ref, xnext_ref, gmix_ref, wqk_ref, wu_ref, wgate_ref, wvo_ref,
                         bslab_ref, ghead_ref, wpool_ref, pscale_ref, wout_ref,
                         fg_ref, fu_ref, fd_ref, Cs_ref, qs_ref,
                         xo_ref, C_ref, n_ref, m_ref, pb_ref, fgb_ref, fub_ref, fdb_ref, cq_ref,
                         tok_a, tr_a, hn_a, tok_b, tr_b, hn_b, wtr_s, *, tm, nt, tiles):
    fgb_ref[...] = fg_ref[...].astype(BF16)
    fub_ref[...] = fu_ref[...].astype(BF16)
    fdb_ref[...] = fd_ref[...].astype(BF16)
    s = pl.program_id(0)

    @pl.when(s == 0)
    def _():
        cq_ref[...] = jnp.zeros_like(cq_ref)

    def sample_cq_items(h):
        lane_b = lax.broadcasted_iota(jnp.int32, cq_ref.shape[1:], 1)
        cq = cq_ref[h]
        for j in range(Cs_ref.shape[0]):
            b = s * Cs_ref.shape[0] + j
            col = jnp.sum(Cs_ref[j, h] * qs_ref[h, pl.ds(b, 1), :], axis=-1, keepdims=True)
            cq = jnp.where(lane_b == b, col, cq)
            yield
        cq_ref[h] = cq
        yield

    t0 = lax.rem(tiles * s, nt)
    proj_args = (wqk_ref, wu_ref, wgate_ref, wtr_s, bslab_ref)
    mix_args = (ghead_ref, wpool_ref, pscale_ref, wout_ref, C_ref, n_ref, m_ref, pb_ref)
    slots = ((tok_a, tr_a, hn_a), (tok_b, tr_b, hn_b))
    x_tile = lambda k: xcur_ref.at[k] if k < tiles else xnext_ref.at[k - tiles]

    @pl.when(s == 0)
    def _():
        wtr_s[...] = wvo_ref[...].T
        _norm_tile(x_tile(0), gmix_ref, hn_a)
        _interleave(_in_proj_items(hn_a, *proj_args, tok_a, tr_a))
        _norm_tile(x_tile(1), gmix_ref, hn_b)

    @pl.when(t0 == 0)
    def _():
        C_ref[...] = jnp.zeros_like(C_ref)
        n_ref[...] = jnp.zeros_like(n_ref)
        m_ref[...] = jnp.zeros_like(m_ref)
        pb_ref[...] = jnp.zeros_like(pb_ref)

    for k in range(tiles):
        tok, tr, hn = slots[k % 2]
        tok_n, tr_n, hn_n = slots[(k + 1) % 2]
        _interleave(
            _in_proj_items(hn_n, *proj_args, tok_n, tr_n,
                           then=functools.partial(_norm_tile, x_tile(k + 2), gmix_ref, hn)),
            _mix_items(x_tile(k), tok, tr, t0 + k, *mix_args, xo_ref.at[k], tm),
            *(sample_cq_items(h) for h in range(k, N_HEADS, tiles)))


def _const_spec(shape):
    nd = len(shape)
    return pl.BlockSpec(shape, lambda *_: (0,) * nd, pipeline_mode=pl.Buffered(1))


def _layer_spec(a, l):
    nd = a.ndim - 1
    return pl.BlockSpec((None,) + a.shape[1:], lambda *_: (l,) + (0,) * nd,
                        pipeline_mode=pl.Buffered(1))


def _layer_cols_spec(a, l, width, j):
    return pl.BlockSpec((None, a.shape[1], width), lambda *_: (l, 0, j),
                        pipeline_mode=pl.Buffered(1))


def _in_hbm(*arrays):
    return [pltpu.with_memory_space_constraint(a, pltpu.MemorySpace.HBM) for a in arrays]


def _layer_bytes(*arrays):
    return sum(a.size // a.shape[0] * a.dtype.itemsize for a in arrays)


def _vmem_limit(nbytes):
    return int(min(V7X_VMEM_BYTES - VMEM_RESERVE_BYTES, max(nbytes, VMEM_FLOOR_BYTES)))


def _slab_rows(rows, nsteps):
    for steps_per_slab in (1, 2):
        per = rows * steps_per_slab // nsteps
        if per * nsteps == rows * steps_per_slab and per % N_ROWS == 0:
            return per, steps_per_slab
    raise ValueError(f"cannot split {rows} weight rows over {nsteps} grid steps")


def _prompt_mixer(x, l, gmix, w_b, wu, wgate, bslab, ghead_rep, wpool, pscale, wout,
                  ffn_w, Cs, qs, *, tm, tiles):
    B, T, D = x.shape
    nt = T // tm
    assert tiles % 2 == 0 and nt % tiles == 0, "a grid step covers an even number of tiles of one sequence"
    ntiles = B * nt
    nsteps = ntiles // tiles
    xt = x.reshape(ntiles, tm, D)
    pair = pl.BlockSpec((tiles, tm, D), lambda s: (s, 0, 0))
    nxt = pl.BlockSpec((2, tm, D),
                       lambda s: (jnp.minimum((s + 1) * (tiles // 2), ntiles // 2 - 1), 0, 0))
    seq = lambda s: (tiles * s) // nt
    out_shapes = (
        jax.ShapeDtypeStruct((ntiles, tm, D), F32),
        jax.ShapeDtypeStruct((B, N_HEADS, HEAD_DIM, HEAD_DIM), F32),
        jax.ShapeDtypeStruct((B, N_HEADS, N_ROWS, HEAD_DIM), F32),
        jax.ShapeDtypeStruct((B, SUBLANES, LANES), F32),
        jax.ShapeDtypeStruct((B, POOL_CARRY, POOL_WIDTH), F32),
    )
    out_specs = (
        pair,
        pl.BlockSpec((None, N_HEADS, HEAD_DIM, HEAD_DIM), lambda s: (seq(s), 0, 0, 0)),
        pl.BlockSpec((None, N_HEADS, N_ROWS, HEAD_DIM), lambda s: (seq(s), 0, 0, 0)),
        pl.BlockSpec((None, SUBLANES, LANES), lambda s: (seq(s), 0, 0)),
        pl.BlockSpec((None, POOL_CARRY, POOL_WIDTH), lambda s: (seq(s), 0, 0)),
    )
    consts = (gmix, w_b, wu, wgate, w_b, bslab, ghead_rep, wpool, pscale, wout)
    in_specs = [pair, nxt] + [_layer_spec(a, l) for a in consts]
    in_specs[3] = _layer_cols_spec(w_b, l, 2 * MLSTM_WIDTH, 0)
    in_specs[6] = _layer_cols_spec(w_b, l, 2 * MLSTM_WIDTH, 1)
    slab_bytes = 0
    for w in ffn_w:
        rows, cols = w.shape[1:]
        per, stride = _slab_rows(rows, nsteps)
        slab_bytes += 2 * per * cols * (4 + 2)
        in_specs.append(pl.BlockSpec((None, per, cols), lambda s, k=stride: (l, s // k, 0)))
        out_specs += (pl.BlockSpec((per, cols), lambda s, k=stride: (s // k, 0)),)
        out_shapes += (jax.ShapeDtypeStruct((rows, cols), BF16),)
    nb = qs.shape[1]
    bbs = nb // nsteps
    assert bbs * nsteps == nb, "the sample sequences are spread evenly over the grid steps"
    in_specs += [pl.BlockSpec((None, bbs) + Cs.shape[2:], lambda s: (l, s, 0, 0, 0)),
                 _const_spec(qs.shape)]
    cq_shape = (N_HEADS, HEAD_DIM, nb)
    out_specs += (pl.BlockSpec(cq_shape, lambda s: (0, 0, 0)),)
    out_shapes += (jax.ShapeDtypeStruct(cq_shape, F32),)
    slab_bytes += 2 * bbs * Cs[0, 0].size * 4
    est = (_layer_bytes(wu, wgate, wpool, wout) + 2 * D * TR_ROWS * 2 + slab_bytes
           + 6 * tiles * tm * D * 4 + 12 * tm * SAMPLE_COLS * 4)
    scratch = [pltpu.VMEM((tm, TOK_COLS), F32), pltpu.VMEM((TR_ROWS, tm), F32),
               pltpu.VMEM((tm, D), BF16)] * 2 + [pltpu.VMEM((TR_ROWS, D), BF16)]
    outs = pl.pallas_call(
        functools.partial(_prompt_mixer_kernel, tm=tm, nt=nt, tiles=tiles),
        grid=(nsteps,), in_specs=in_specs, out_specs=out_specs, out_shape=out_shapes,
        scratch_shapes=scratch,
        compiler_params=pltpu.CompilerParams(
            dimension_semantics=("arbitrary",),
            vmem_limit_bytes=_vmem_limit(est)),
        name="prompt_mixer",
    )(*_in_hbm(xt, xt, *consts, *ffn_w, Cs, qs))
    return (outs[0].reshape(B, T, D),) + tuple(outs[1:])


def _ffn_tile(x, gffn_ref, wg_s, wu_s, wd_s, gfin_ref):
    h2 = _rms(x, gffn_ref[...]).astype(BF16)
    a = _dot(h2, wg_s[...])
    bu = _dot(h2, wu_s[...])
    act = (a * jax.nn.sigmoid(a) * bu).astype(BF16)
    y = x + _dot(act, wd_s[...])
    return y if gfin_ref is None else _rms(y, gfin_ref[...])


def _state_update_block(C_ref, g_ref, wvt_ref, k_ref, Co_ref, first_seq):
    nb = k_ref.shape[1]
    lane = lax.broadcasted_iota(jnp.int32, (HEAD_DIM, nb), 1)
    for h in range(N_HEADS):
        wvt = wvt_ref[h]
        for j in range(C_ref.shape[0]):
            b = first_seq + j
            col = jnp.sum(jnp.where(lane == b, wvt, 0.0), axis=-1, keepdims=True)
            k_row = k_ref[h, pl.ds(b, 1), :]
            Co_ref[j, h] = g_ref[h, pl.ds(b, 1), :] * C_ref[j, h] + col * k_row


def _ffn_kernel(*refs, final, n_tiles, upd_blocks):
    refs = list(refs)
    x_ref, xs_ref, mixs_ref, wout_ref, gffn_ref, wg_ref, wu_ref, wd_ref = refs[:8]
    refs = refs[8:]
    gfin_ref = refs.pop(0) if final else None
    if upd_blocks:
        C_ref, g_ref, wvt_ref, kb_ref = refs[:4]
        refs = refs[4:]
        out_ref, outs_ref, Co_ref = refs
    else:
        out_ref, outs_ref = refs
    i = pl.program_id(0)
    weights = (gffn_ref, wg_ref, wu_ref, wd_ref, gfin_ref)

    @pl.when(i < n_tiles)
    def _():
        for r0 in range(0, x_ref.shape[0], FF_SUBTILE):
            rows = slice(r0, r0 + FF_SUBTILE)
            out_ref[rows, :] = _ffn_tile(x_ref[rows, :], *weights)
        if upd_blocks:
            _state_update_block(C_ref, g_ref, wvt_ref, kb_ref, Co_ref,
                                lax.rem(i, upd_blocks) * C_ref.shape[0])

    @pl.when(i == n_tiles)
    def _():
        xs = xs_ref[...] + _dot(mixs_ref[...].astype(BF16), wout_ref[...])
        outs_ref[...] = _ffn_tile(xs, *weights)


def _ffn(x, xs, mixs, l, gffn, wg, wu, wd, wout, *, tm, gfin=None, state_update=None):
    M, D = x.shape
    dff = wg.shape[-1]
    n_tiles = M // tm
    assert n_tiles * tm == M
    last = lambda i: jnp.minimum(i, n_tiles - 1)
    tile = pl.BlockSpec((tm, D), lambda i: (last(i), 0))
    args = [x, xs, mixs, wout, gffn, wg, wu, wd]
    in_specs = [tile, _const_spec(xs.shape), _const_spec(mixs.shape), _layer_spec(wout, l),
                _layer_spec(gffn, l)] + [_const_spec(w.shape) for w in (wg, wu, wd)]
    if gfin is not None:
        args.append(gfin)
        in_specs.append(_const_spec(gfin.shape))
    out_specs = [tile, pl.BlockSpec(xs.shape, lambda i: (0, 0))]
    out_shapes = [jax.ShapeDtypeStruct((M, D), F32), jax.ShapeDtypeStruct(xs.shape, F32)]
    est = (3 * D * dff * 2 + _layer_bytes(wout) + 6 * xs.size * 4
           + 6 * tm * D * 4 + 4 * FF_SUBTILE * dff * 4)
    upd_blocks = 0
    if state_update is not None:
        C = state_update[0]
        depth, nb = C.shape[:2]
        assert n_tiles % depth == 0, "the state blocks of all layers are spread over the tile steps"
        upd_blocks = n_tiles // depth
        bbu = nb // upd_blocks
        assert bbu * upd_blocks == nb
        cblk = pl.BlockSpec((None, bbu) + C.shape[2:],
                            lambda i: (last(i) // upd_blocks, last(i) % upd_blocks, 0, 0, 0))
        per_layer = lambda a: pl.BlockSpec((None,) + a.shape[1:],
                                           lambda i: (last(i) // upd_blocks, 0, 0, 0))
        args += list(state_update)
        in_specs += [cblk] + [per_layer(a) for a in state_update[1:]]
        out_specs.append(cblk)
        out_shapes.append(jax.ShapeDtypeStruct(C.shape, F32))
        est += 4 * bbu * C[0, 0].size * 4 + 2 * _layer_bytes(*state_update[1:])
    return pl.pallas_call(
        functools.partial(_ffn_kernel, final=gfin is not None, n_tiles=n_tiles,
                          upd_blocks=upd_blocks),
        grid=(n_tiles + 1,), in_specs=in_specs, out_specs=tuple(out_specs),
        out_shape=tuple(out_shapes),
        compiler_params=pltpu.CompilerParams(
            dimension_semantics=("arbitrary",), vmem_limit_bytes=_vmem_limit(est)),
        name="ffn_final" if gfin is not None else "ffn",
    )(*_in_hbm(*args))


def _sample_proj_kernel(x_ref, gmix_ref, wqk_ref, wu_ref, wgate_ref, wvo_ref, bslab_ref, proj_ref):
    hn = _rms(x_ref[...], gmix_ref[...]).astype(BF16)
    proj_ref[:, Q0:U0] = _dot(hn, wqk_ref[...])
    proj_ref[:, U0:GI0] = _dot(hn, wu_ref[...])
    proj_ref[:, GI0:TOK_COLS] = _dot(hn, wgate_ref[...]) + bslab_ref[...]
    proj_ref[:, TOK_COLS:] = _dot(hn, wvo_ref[...])


def _sample_proj(x, l, gmix, w_b, wu, wgate, bslab):
    M = x.shape[0]
    consts = (gmix, w_b, wu, wgate, w_b, bslab)
    est = 2 * _layer_bytes(w_b, wu, wgate) + 8 * M * SAMPLE_COLS * 4
    return pl.pallas_call(
        _sample_proj_kernel,
        grid=(1,),
        in_specs=[_const_spec(x.shape), _layer_spec(gmix, l),
                  _layer_cols_spec(w_b, l, 2 * MLSTM_WIDTH, 0), _layer_spec(wu, l),
                  _layer_spec(wgate, l), _layer_cols_spec(w_b, l, 2 * MLSTM_WIDTH, 1),
                  _layer_spec(bslab, l)],
        out_specs=pl.BlockSpec((M, SAMPLE_COLS), lambda i: (0, 0)),
        out_shape=jax.ShapeDtypeStruct((M, SAMPLE_COLS), F32),
        compiler_params=pltpu.CompilerParams(
            dimension_semantics=("arbitrary",), vmem_limit_bytes=_vmem_limit(est)),
        name="sample_proj",
    )(*_in_hbm(x, *consts))


def _head_out(h, g_head, o):
    hn = h * lax.rsqrt(jnp.mean(h * h, axis=-1, keepdims=True) + EPS)
    return hn * g_head * jax.nn.sigmoid(o)


def _sample_step_kernel(proj_ref, ghead_ref, wpool_ref, pscale_ref, cq_ref, n_ref, m_ref, buf_ref,
                        mix_ref, no_ref, mo_ref, bufo_ref, g_ref, wvt_ref, kb_ref):
    nb = proj_ref.shape[0]
    scale = HEAD_DIM ** -0.5
    head = lambda c0, h: proj_ref[:, c0 + h * HEAD_DIM:c0 + (h + 1) * HEAD_DIM]

    def gate_terms(h):
        i_c = _lane_col(proj_ref[:, GI0:GI0 + GATE_SLAB], h)
        lf = _lane_col(_log_sigmoid(proj_ref[:, GF0:GF0 + GATE_SLAB]), h)
        m0 = _lane_col(m_ref[...], h)
        inter = lf + m0
        m_t = jnp.maximum(inter, i_c)
        return jnp.exp(i_c - m_t), jnp.exp(inter - m_t), m_t

    for h in range(N_HEADS):
        w, g, _ = gate_terms(h)
        g_ref[h] = jnp.broadcast_to(g, (nb, LANES))
        wvt_ref[h] = (w * head(SV0, h)).T
        kb_ref[h] = (head(K0, h) * scale).astype(BF16)

    def finish():
        heads, n_heads = [], []
        lane_m = lax.broadcasted_iota(jnp.int32, (nb, LANES), 1)
        m_out = jnp.zeros((nb, LANES), F32)
        for h in range(N_HEADS):
            cols = slice(h * HEAD_DIM, (h + 1) * HEAD_DIM)
            w, g, m_t = gate_terms(h)
            qf = head(Q0, h)
            kf = head(K0, h) * scale
            vf = head(SV0, h)
            n0 = n_ref[:, cols]
            qk = jnp.sum(qf.astype(BF16).astype(F32) * kf.astype(BF16).astype(F32),
                         axis=-1, keepdims=True)
            s = qk * w
            num = s * vf + g * cq_ref[h].T
            den = s + g * jnp.sum(n0 * qf, axis=-1, keepdims=True)
            hc = num * (1.0 / jnp.maximum(jnp.abs(den), jnp.exp(-m_t)))
            heads.append(_head_out(hc, ghead_ref[:, cols], head(SO0, h)))
            n_heads.append(g * n0 + w * kf)
            m_out = jnp.where(lane_m == h, m_t, m_out)
        no_ref[...] = jnp.concatenate(n_heads, axis=-1)
        mo_ref[...] = m_out

        u = proj_ref[:, U0:U0 + POOL_WIDTH]
        pooled = []
        for gi, win in enumerate(POOL_WINDOWS):
            cols = slice(gi * POOL_GROUP, (gi + 1) * POOL_GROUP)
            a = u[:, cols]
            for jrow in range(POOL_BUF - (win - 1), POOL_BUF):
                a = a + buf_ref[jrow, :, cols]
            p = a / float(win) - u[:, cols]
            pooled.append(_dot(p.astype(BF16), wpool_ref[gi]) * pscale_ref[:, cols])
        mix_ref[...] = jnp.concatenate(heads + pooled, axis=-1)
        for jrow in range(POOL_BUF - 1):
            bufo_ref[jrow] = buf_ref[jrow + 1]
        bufo_ref[POOL_BUF - 1] = u

    finish()


def _sample_step(proj, l, ghead, wpool, pscale, cq, n, m, buf_t):
    nb = proj.shape[0]
    D = MLSTM_WIDTH + POOL_WIDTH
    in_specs = [_const_spec(proj.shape)] + [_layer_spec(a, l) for a in (ghead, wpool, pscale)] + [
        _const_spec(cq.shape), _layer_spec(n, l), _layer_spec(m, l), _layer_spec(buf_t, l)]
    args = [proj, ghead, wpool, pscale, cq, n, m, buf_t]
    shapes = ((nb, D), n.shape[1:], (nb, LANES), buf_t.shape[1:],
              (N_HEADS, nb, LANES), (N_HEADS, HEAD_DIM, nb), (N_HEADS, nb, HEAD_DIM))
    dtypes = (F32,) * 6 + (BF16,)
    out_shapes = tuple(jax.ShapeDtypeStruct(s, d) for s, d in zip(shapes, dtypes))
    out_specs = tuple(pl.BlockSpec(s, lambda i, k=len(s): (0,) * k) for s in shapes)
    est = 4 * _layer_bytes(buf_t) + 8 * proj.size * 4
    return pl.pallas_call(
        _sample_step_kernel,
        grid=(1,), in_specs=in_specs, out_specs=out_specs, out_shape=out_shapes,
        compiler_params=pltpu.CompilerParams(
            dimension_semantics=("arbitrary",), vmem_limit_bytes=_vmem_limit(est)),
        name="sample_step",
    )(*_in_hbm(*args))


def _pack_w_in(w_in, b_gate):
    depth, d, _ = w_in.shape
    g0 = 4 * MLSTM_WIDTH
    w_b = w_in.astype(BF16)
    wu = w_b[:, :, g0 + 2 * N_HEADS:]
    pad = jnp.zeros((depth, d, GATE_SLAB - N_HEADS), BF16)
    gates = w_b[:, :, g0:g0 + 2 * N_HEADS]
    wgate = jnp.concatenate([gates[:, :, :N_HEADS], pad, gates[:, :, N_HEADS:], pad], axis=-1)
    bpad = jnp.zeros((depth, GATE_SLAB - N_HEADS), b_gate.dtype)
    bslab = jnp.concatenate([b_gate[:, :N_HEADS], bpad, b_gate[:, N_HEADS:], bpad], axis=-1)
    return w_b, wu, wgate, bslab[:, None, :]


def kernel(x_prompt, x_sample, state_mlstm_C, state_mlstm_n, state_mlstm_m, state_pool_buf,
           g_mix, w_in, b_gate, g_head, w_pool, pool_scale, w_out, g_ffn, w_gate, w_up,
           w_down, g_final):
    depth = w_in.shape[0]
    B, T, D = x_prompt.shape
    nb = x_sample.shape[0]
    tm_mixer, mixer_tiles, tm_ffn, tm_ffn_last = CHUNK, 4, 1024, 512

    win = _pack_w_in(w_in, b_gate)
    wpool = w_pool.astype(BF16)
    wout = w_out.astype(BF16)
    ffn_w = (w_gate, w_up, w_down)
    row = lambda a: a[:, None, :]
    gmix, ghead, pscale, gffn = row(g_mix), row(g_head), row(pool_scale), row(g_ffn)
    ghead_rep = jnp.broadcast_to(g_head[:, :, None], g_head.shape + (CHUNK,))
    gfin = g_final[None, :]

    n_in = state_mlstm_n.reshape(depth, nb, MLSTM_WIDTH)
    m_pad = jnp.pad(state_mlstm_m, ((0, 0), (0, 0), (0, LANES - N_HEADS)))
    buf_t = jnp.transpose(state_pool_buf, (0, 2, 1, 3))

    xp = x_prompt
    xs = x_sample.reshape(nb, D)
    Cp, npr, mp, bp, ns, ms, bs, upd = [], [], [], [], [], [], [], []
    for l in range(depth):
        proj = _sample_proj(xs, l, gmix, *win)
        xm, C1, n1, m1, pb1, wg, wu, wd, cq = _prompt_mixer(
            xp, l, gmix, *win, ghead_rep, wpool, pscale, wout, ffn_w, state_mlstm_C,
            jnp.swapaxes(proj[:, Q0:Q0 + MLSTM_WIDTH].reshape(nb, N_HEADS, HEAD_DIM), 0, 1),
            tm=tm_mixer, tiles=mixer_tiles)
        Cp.append(C1)
        npr.append(n1[:, :, 0, :])
        mp.append(m1[:, :N_HEADS, 0])
        bp.append(pb1[:, POOL_CARRY - POOL_BUF:, :])

        mix, n1, m1, buf1, *terms = _sample_step(
            proj, l, ghead, wpool, pscale, cq, n_in, m_pad, buf_t)
        ns.append(n1.reshape(nb, N_HEADS, HEAD_DIM))
        ms.append(m1[:, :N_HEADS])
        bs.append(jnp.transpose(buf1, (1, 0, 2)))
        upd.append(terms)

        last = l == depth - 1
        if last:
            g_all, wvt_all, kb_all = (jnp.stack(t, 0) for t in zip(*upd))
            terms = (g_all, wvt_all, kb_all.astype(F32))
        xp, xs, *C_sample = _ffn(
            xm.reshape(B * T, D), xs, mix, l, gffn, wg, wu, wd, wout,
            tm=tm_ffn_last if last else tm_ffn, gfin=gfin if last else None,
            state_update=(state_mlstm_C,) + terms if last else None)
        xp = xp.reshape(B, T, D)
    y_prompt = xp
    y_sample = xs.reshape(nb, 1, D)
    (C_sample,) = C_sample

    st = lambda xs_: jnp.stack(xs_, 0)
    return (y_prompt, y_sample, st(Cp), st(npr), st(mp), st(bp),
            C_sample, st(ns), st(ms), st(bs))
```

```python
import functools

import jax
import jax.numpy as jnp
from jax import lax
from jax.experimental import pallas as pl
from jax.experimental.pallas import tpu as pltpu

F32 = jnp.float32
BF16 = jnp.bfloat16

EPS = 1e-6
N_HEADS = 4
HEAD_DIM = 128
MLSTM_WIDTH = N_HEADS * HEAD_DIM
POOL_WINDOWS = (2, 4, 8, 16)
POOL_GROUP = 128
POOL_WIDTH = POOL_GROUP * len(POOL_WINDOWS)
POOL_BUF = max(POOL_WINDOWS) - 1
POOL_CARRY = POOL_BUF + 1
CHUNK = 256
LANES = 128
FF_SUBTILE = 256
SUBLANES = 8
N_ROWS = 2 * SUBLANES
GATE_SLAB = LANES

Q0, K0, U0 = 0, MLSTM_WIDTH, 2 * MLSTM_WIDTH
GI0 = U0 + POOL_WIDTH
GF0 = GI0 + GATE_SLAB
TOK_COLS = GF0 + GATE_SLAB
VT0, OT0 = 0, MLSTM_WIDTH
TR_ROWS = 2 * MLSTM_WIDTH
SV0 = TOK_COLS
SO0 = TOK_COLS + MLSTM_WIDTH
SAMPLE_COLS = TOK_COLS + TR_ROWS

V7X_VMEM_BYTES = 64 * 1024 * 1024
VMEM_RESERVE_BYTES = 4 * 1024 * 1024
VMEM_FLOOR_BYTES = 16 * 1024 * 1024

_NT = (((1,), (1,)), ((), ()))
_TN = (((0,), (0,)), ((), ()))


def _dot(a, b):
    return jnp.dot(a, b, preferred_element_type=F32)


def _dot_nt(a, b):
    return lax.dot_general(a, b, _NT, preferred_element_type=F32)


def _dot_tn(a, b):
    return lax.dot_general(a, b, _TN, preferred_element_type=F32)


def _rms(x, g):
    return x * lax.rsqrt(jnp.mean(x * x, axis=-1, keepdims=True) + EPS) * g


def _log_sigmoid(x):
    return jnp.minimum(x, 0.0) - jnp.log1p(jnp.exp(-jnp.abs(x)))


def _lane_col(slab, j):
    lane = lax.broadcasted_iota(jnp.int32, slab.shape, 1)
    return jnp.sum(jnp.where(lane == j, slab, 0.0), axis=-1, keepdims=True)


def _split3(x):
    hi = x.astype(BF16)
    r1 = x - hi.astype(F32)
    mid = r1.astype(BF16)
    lo = (r1 - mid.astype(F32)).astype(BF16)
    return hi, mid, lo


def _cumsum_rows(x):
    r = lax.broadcasted_iota(jnp.int32, (CHUNK, CHUNK), 0)
    c = lax.broadcasted_iota(jnp.int32, (CHUNK, CHUNK), 1)
    tril = (r >= c).astype(BF16)
    hi, mid, lo = _split3(x)
    return _dot(tril, hi) + _dot(tril, mid) + _dot(tril, lo)


def _head_stages(h, tok_ref, tr_ref, a_slab, a_t, cs_t, ghead_ref, C_ref, n_ref, m_ref,
                 mask_st, out):
    cols = slice(h * HEAD_DIM, (h + 1) * HEAD_DIM)
    head = lambda c0: slice(c0 + h * HEAD_DIM, c0 + (h + 1) * HEAD_DIM)
    C, n_rows, m = C_ref[h], n_ref[h], m_ref[h:h + 1, 0:1]
    qb = tok_ref[:, head(Q0)].astype(BF16)
    kb = tok_ref[:, head(K0)].astype(BF16)
    by_q = _dot_nt(jnp.concatenate([kb, C.astype(BF16), n_rows.astype(BF16)], axis=0), qb)
    yield
    sT = by_q[:CHUNK]
    cqT = by_q[CHUNK:CHUNK + HEAD_DIM]
    qn = by_q[CHUNK + HEAD_DIM:CHUNK + HEAD_DIM + 1]
    a_row, b_row = a_t[h:h + 1, :], cs_t[h:h + 1, :]
    aT = jnp.where(mask_st, a_slab[:, h:h + 1], -jnp.inf)
    M = jnp.maximum(jnp.max(aT, axis=0, keepdims=True), m)
    swT = sT * jnp.exp(aT - M)
    g = jnp.exp(m - M)
    den = jnp.sum(swT, axis=0, keepdims=True) + g * qn
    inv = 1.0 / jnp.maximum(jnp.abs(den), jnp.exp(-(b_row + M)))
    vT = tr_ref[head(VT0), :]
    hT = (_dot(vT.astype(BF16), swT.astype(BF16)) + g * cqT) * inv
    hnT = hT * lax.rsqrt(jnp.mean(hT * hT, axis=0, keepdims=True) + EPS)
    out[h] = (hnT * ghead_ref[cols, :] * jax.nn.sigmoid(tr_ref[head(OT0), :])).astype(BF16)
    yield
    M_last = M[:, CHUNK - 1:CHUNK]
    g_end = jnp.exp(m - M_last)
    w_end = jnp.exp(a_row - M_last)
    w_rows = jnp.broadcast_to(w_end, (N_ROWS, CHUNK))
    by_k = _dot(jnp.concatenate([(vT * w_end).astype(BF16), w_rows.astype(BF16)], axis=0), kb)
    C_ref[h] = g_end * C + by_k[:HEAD_DIM]
    n_ref[h] = g_end * n_rows + by_k[HEAD_DIM:]
    m_ref[h:h + 1, :] = jnp.broadcast_to(b_row[:, CHUNK - 1:CHUNK] + M_last, (1, LANES))
    yield


def _skewed(streams):
    streams = list(streams)
    live = []
    while streams or live:
        if streams:
            live.append(streams.pop(0))
        for g in list(live):
            try:
                next(g)
            except StopIteration:
                live.remove(g)
        if streams or live:
            yield


def _norm_tile(x_ref, gmix_ref, hn_ref):
    hn_ref[...] = _rms(x_ref[...], gmix_ref[...]).astype(BF16)


def _in_proj_items(hn_ref, wqk_ref, wu_ref, wgate_ref, wtr_ref, bslab_ref, tok_ref, tr_ref,
                   then=None):
    tok_ref[:, GI0:GI0 + 2 * GATE_SLAB] = _dot(hn_ref[...], wgate_ref[...]) + bslab_ref[...]
    yield
    tok_ref[:, Q0:Q0 + MLSTM_WIDTH] = _dot(hn_ref[...], wqk_ref[:, :MLSTM_WIDTH])
    yield
    tok_ref[:, K0:K0 + MLSTM_WIDTH] = (_dot(hn_ref[...], wqk_ref[:, MLSTM_WIDTH:])
                                       * (HEAD_DIM ** -0.5))
    yield
    tok_ref[:, U0:U0 + POOL_WIDTH] = _dot(hn_ref[...], wu_ref[...])
    yield
    for r0 in (VT0, OT0):
        tr_ref[r0:r0 + MLSTM_WIDTH, :] = _dot_nt(wtr_ref[r0:r0 + MLSTM_WIDTH, :], hn_ref[...])
        yield
    if then is not None:
        then()
        yield


def _mix_items(x_ref, tok_ref, tr_ref, t, ghead_ref, wpool_ref, pscale_ref, wout_ref,
               C_ref, n_ref, m_ref, pb_ref, out_ref, tm):
    r = lax.broadcasted_iota(jnp.int32, (CHUNK, CHUNK), 0)
    c = lax.broadcasted_iota(jnp.int32, (CHUNK, CHUNK), 1)
    mask_st = r <= c

    assert tm == CHUNK, "one mLSTM chunk per tile"
    cs = _cumsum_rows(_log_sigmoid(tok_ref[:, GF0:GF0 + GATE_SLAB]))
    a_slab = tok_ref[:, GI0:GI0 + GATE_SLAB] - cs
    cs_t = cs.T
    a_t = a_slab.T
    yield
    heads = [None] * N_HEADS
    yield from _skewed(
        _head_stages(h, tok_ref, tr_ref, a_slab, a_t, cs_t, ghead_ref, C_ref, n_ref, m_ref,
                     mask_st, heads)
        for h in range(N_HEADS))
    yield
    hmT = jnp.concatenate(heads, axis=0)

    u = tok_ref[:, U0:U0 + POOL_WIDTH]
    ext = jnp.concatenate([pb_ref[...], u], axis=0)
    pb_ref[...] = u[tm - POOL_CARRY:, :]
    pos = t * tm + lax.broadcasted_iota(jnp.int32, (tm, POOL_GROUP), 0)
    pooled = []
    for gi, win in enumerate(POOL_WINDOWS):
        cols = slice(gi * POOL_GROUP, (gi + 1) * POOL_GROUP)
        a = ext[:, cols]
        step = 1
        while step < win:
            a = a[step:, :] + a[:a.shape[0] - step, :]
            step *= 2
        a = a[a.shape[0] - tm:, :]
        cnt = jnp.minimum(win, pos + 1).astype(F32)
        p = a / cnt - u[:, cols]
        pooled.append((_dot(p.astype(BF16), wpool_ref[gi]) * pscale_ref[:, cols]).astype(BF16))
    yield
    pm = jnp.concatenate(pooled, axis=-1)
    out_ref[...] = (x_ref[...] + _dot_tn(hmT, wout_ref[:MLSTM_WIDTH, :])
                    + _dot(pm, wout_ref[MLSTM_WIDTH:, :]))
    yield


def _interleave(*streams):
    streams = list(streams)
    while streams:
        for s in list(streams):
            try:
                next(s)
            except StopIteration:
                streams.remove(s)


def _prompt_mixer_kernel(xcur_ref, xnext_ref, gmix_ref, wqk_ref, wu_ref, wgate_ref, wvo_ref,
                         bslab_ref, ghead_ref, wpool_ref, pscale_ref, wout_ref,
                         fg_ref, fu_ref, fd_ref, Cs_ref, xs_ref,
                         xo_ref, C_ref, n_ref, m_ref, pb_ref, fgb_ref, fub_ref, fdb_ref, cq_ref,
                         projs_ref,
                         tok_a, tr_a, hn_a, tok_b, tr_b, hn_b, wtr_s, qs_s, *, tm, nt, tiles):
    fgb_ref[...] = fg_ref[...].astype(BF16)
    fub_ref[...] = fu_ref[...].astype(BF16)
    fdb_ref[...] = fd_ref[...].astype(BF16)
    s = pl.program_id(0)

    @pl.when(s == 0)
    def _():
        cq_ref[...] = jnp.zeros_like(cq_ref)
        hn_s = _rms(xs_ref[...], gmix_ref[...]).astype(BF16)
        qk = _dot(hn_s, wqk_ref[...])
        projs_ref[:, Q0:U0] = qk
        projs_ref[:, U0:GI0] = _dot(hn_s, wu_ref[...])
        projs_ref[:, GI0:TOK_COLS] = _dot(hn_s, wgate_ref[...]) + bslab_ref[...]
        projs_ref[:, TOK_COLS:] = _dot(hn_s, wvo_ref[...])
        for h in range(N_HEADS):
            qs_s[h] = qk[:, Q0 + h * HEAD_DIM:Q0 + (h + 1) * HEAD_DIM]

    def sample_cq_items(h):
        lane_b = lax.broadcasted_iota(jnp.int32, cq_ref.shape[1:], 1)
        cq = cq_ref[h]
        for j in range(Cs_ref.shape[0]):
            b = s * Cs_ref.shape[0] + j
            col = jnp.sum(Cs_ref[j, h] * qs_s[h, pl.ds(b, 1), :], axis=-1, keepdims=True)
            cq = jnp.where(lane_b == b, col, cq)
            yield
        cq_ref[h] = cq
        yield

    t0 = lax.rem(tiles * s, nt)
    proj_args = (wqk_ref, wu_ref, wgate_ref, wtr_s, bslab_ref)
    mix_args = (ghead_ref, wpool_ref, pscale_ref, wout_ref, C_ref, n_ref, m_ref, pb_ref)
    slots = ((tok_a, tr_a, hn_a), (tok_b, tr_b, hn_b))
    x_tile = lambda k: xcur_ref.at[k] if k < tiles else xnext_ref.at[k - tiles]

    @pl.when(s == 0)
    def _():
        wtr_s[...] = wvo_ref[...].T
        _norm_tile(x_tile(0), gmix_ref, hn_a)
        _interleave(_in_proj_items(hn_a, *proj_args, tok_a, tr_a))
        _norm_tile(x_tile(1), gmix_ref, hn_b)

    @pl.when(t0 == 0)
    def _():
        C_ref[...] = jnp.zeros_like(C_ref)
        n_ref[...] = jnp.zeros_like(n_ref)
        m_ref[...] = jnp.zeros_like(m_ref)
        pb_ref[...] = jnp.zeros_like(pb_ref)

    for k in range(tiles):
        tok, tr, hn = slots[k % 2]
        tok_n, tr_n, hn_n = slots[(k + 1) % 2]
        _interleave(
            _in_proj_items(hn_n, *proj_args, tok_n, tr_n,
                           then=functools.partial(_norm_tile, x_tile(k + 2), gmix_ref, hn)),
            _mix_items(x_tile(k), tok, tr, t0 + k, *mix_args, xo_ref.at[k], tm),
            *(sample_cq_items(h) for h in range(k, N_HEADS, tiles)))


def _const_spec(shape):
    nd = len(shape)
    return pl.BlockSpec(shape, lambda *_: (0,) * nd, pipeline_mode=pl.Buffered(1))


def _layer_spec(a, l):
    nd = a.ndim - 1
    return pl.BlockSpec((None,) + a.shape[1:], lambda *_: (l,) + (0,) * nd,
                        pipeline_mode=pl.Buffered(1))


def _layer_cols_spec(a, l, width, j):
    return pl.BlockSpec((None, a.shape[1], width), lambda *_: (l, 0, j),
                        pipeline_mode=pl.Buffered(1))


def _in_hbm(*arrays):
    return [pltpu.with_memory_space_constraint(a, pltpu.MemorySpace.HBM) for a in arrays]


def _layer_bytes(*arrays):
    return sum(a.size // a.shape[0] * a.dtype.itemsize for a in arrays)


def _vmem_limit(nbytes):
    return int(min(V7X_VMEM_BYTES - VMEM_RESERVE_BYTES, max(nbytes, VMEM_FLOOR_BYTES)))


def _slab_rows(rows, nsteps):
    for steps_per_slab in (1, 2):
        per = rows * steps_per_slab // nsteps
        if per * nsteps == rows * steps_per_slab and per % N_ROWS == 0:
            return per, steps_per_slab
    raise ValueError(f"cannot split {rows} weight rows over {nsteps} grid steps")


def _prompt_mixer(x, l, gmix, w_b, wu, wgate, bslab, ghead_rep, wpool, pscale, wout,
                  ffn_w, Cs, xs, *, tm, tiles):
    B, T, D = x.shape
    nt = T // tm
    assert tiles % 2 == 0 and nt % tiles == 0, "a grid step covers an even number of tiles of one sequence"
    ntiles = B * nt
    nsteps = ntiles // tiles
    xt = x.reshape(ntiles, tm, D)
    pair = pl.BlockSpec((tiles, tm, D), lambda s: (s, 0, 0))
    nxt = pl.BlockSpec((2, tm, D),
                       lambda s: (jnp.minimum((s + 1) * (tiles // 2), ntiles // 2 - 1), 0, 0))
    seq = lambda s: (tiles * s) // nt
    out_shapes = (
        jax.ShapeDtypeStruct((ntiles, tm, D), F32),
        jax.ShapeDtypeStruct((B, N_HEADS, HEAD_DIM, HEAD_DIM), F32),
        jax.ShapeDtypeStruct((B, N_HEADS, N_ROWS, HEAD_DIM), F32),
        jax.ShapeDtypeStruct((B, SUBLANES, LANES), F32),
        jax.ShapeDtypeStruct((B, POOL_CARRY, POOL_WIDTH), F32),
    )
    out_specs = (
        pair,
        pl.BlockSpec((None, N_HEADS, HEAD_DIM, HEAD_DIM), lambda s: (seq(s), 0, 0, 0)),
        pl.BlockSpec((None, N_HEADS, N_ROWS, HEAD_DIM), lambda s: (seq(s), 0, 0, 0)),
        pl.BlockSpec((None, SUBLANES, LANES), lambda s: (seq(s), 0, 0)),
        pl.BlockSpec((None, POOL_CARRY, POOL_WIDTH), lambda s: (seq(s), 0, 0)),
    )
    consts = (gmix, w_b, wu, wgate, w_b, bslab, ghead_rep, wpool, pscale, wout)
    in_specs = [pair, nxt] + [_layer_spec(a, l) for a in consts]
    in_specs[3] = _layer_cols_spec(w_b, l, 2 * MLSTM_WIDTH, 0)
    in_specs[6] = _layer_cols_spec(w_b, l, 2 * MLSTM_WIDTH, 1)
    slab_bytes = 0
    for w in ffn_w:
        rows, cols = w.shape[1:]
        per, stride = _slab_rows(rows, nsteps)
        slab_bytes += 2 * per * cols * (4 + 2)
        in_specs.append(pl.BlockSpec((None, per, cols), lambda s, k=stride: (l, s // k, 0)))
        out_specs += (pl.BlockSpec((per, cols), lambda s, k=stride: (s // k, 0)),)
        out_shapes += (jax.ShapeDtypeStruct((rows, cols), BF16),)
    nb = xs.shape[0]
    bbs = nb // nsteps
    assert bbs * nsteps == nb, "the sample sequences are spread evenly over the grid steps"
    in_specs += [pl.BlockSpec((None, bbs) + Cs.shape[2:], lambda s: (l, s, 0, 0, 0)),
                 _const_spec(xs.shape)]
    cq_shape = (N_HEADS, HEAD_DIM, nb)
    out_specs += (pl.BlockSpec(cq_shape, lambda s: (0, 0, 0)),
                  pl.BlockSpec((nb, SAMPLE_COLS), lambda s: (0, 0)))
    out_shapes += (jax.ShapeDtypeStruct(cq_shape, F32),
                   jax.ShapeDtypeStruct((nb, SAMPLE_COLS), F32))
    slab_bytes += 2 * bbs * Cs[0, 0].size * 4 + 3 * nb * SAMPLE_COLS * 4
    est = (_layer_bytes(wu, wgate, wpool, wout) + 2 * D * TR_ROWS * 2 + slab_bytes
           + 6 * tiles * tm * D * 4 + 12 * tm * SAMPLE_COLS * 4)
    scratch = [pltpu.VMEM((tm, TOK_COLS), F32), pltpu.VMEM((TR_ROWS, tm), F32),
               pltpu.VMEM((tm, D), BF16)] * 2 + [pltpu.VMEM((TR_ROWS, D), BF16),
                                                  pltpu.VMEM((N_HEADS, nb, HEAD_DIM), F32)]
    outs = pl.pallas_call(
        functools.partial(_prompt_mixer_kernel, tm=tm, nt=nt, tiles=tiles),
        grid=(nsteps,), in_specs=in_specs, out_specs=out_specs, out_shape=out_shapes,
        scratch_shapes=scratch,
        compiler_params=pltpu.CompilerParams(
            dimension_semantics=("arbitrary",),
            vmem_limit_bytes=_vmem_limit(est)),
        name="prompt_mixer",
    )(*_in_hbm(xt, xt, *consts, *ffn_w, Cs, xs))
    return (outs[0].reshape(B, T, D),) + tuple(outs[1:])


def _ffn_tile(x, gffn_ref, wg_s, wu_s, wd_s, gfin_ref):
    h2 = _rms(x, gffn_ref[...]).astype(BF16)
    a = _dot(h2, wg_s[...])
    bu = _dot(h2, wu_s[...])
    act = (a * jax.nn.sigmoid(a) * bu).astype(BF16)
    y = x + _dot(act, wd_s[...])
    return y if gfin_ref is None else _rms(y, gfin_ref[...])


def _state_update_block(C_ref, g_ref, wvt_ref, k_ref, Co_ref, first_seq):
    nb = k_ref.shape[1]
    lane = lax.broadcasted_iota(jnp.int32, (HEAD_DIM, nb), 1)
    for h in range(N_HEADS):
        wvt = wvt_ref[h]
        for j in range(C_ref.shape[0]):
            b = first_seq + j
            col = jnp.sum(jnp.where(lane == b, wvt, 0.0), axis=-1, keepdims=True)
            k_row = k_ref[h, pl.ds(b, 1), :]
            Co_ref[j, h] = g_ref[h, pl.ds(b, 1), :] * C_ref[j, h] + col * k_row


def _ffn_kernel(*refs, final, n_tiles, upd_blocks):
    refs = list(refs)
    x_ref, xs_ref, mixs_ref, wout_ref, gffn_ref, wg_ref, wu_ref, wd_ref = refs[:8]
    refs = refs[8:]
    gfin_ref = refs.pop(0) if final else None
    if upd_blocks:
        C_ref, g_ref, wvt_ref, kb_ref = refs[:4]
        refs = refs[4:]
        out_ref, outs_ref, Co_ref = refs
    else:
        out_ref, outs_ref = refs
    i = pl.program_id(0)
    weights = (gffn_ref, wg_ref, wu_ref, wd_ref, gfin_ref)

    @pl.when(i < n_tiles)
    def _():
        for r0 in range(0, x_ref.shape[0], FF_SUBTILE):
            rows = slice(r0, r0 + FF_SUBTILE)
            out_ref[rows, :] = _ffn_tile(x_ref[rows, :], *weights)
        if upd_blocks:
            _state_update_block(C_ref, g_ref, wvt_ref, kb_ref, Co_ref,
                                lax.rem(i, upd_blocks) * C_ref.shape[0])

    @pl.when(i == n_tiles)
    def _():
        xs = xs_ref[...] + _dot(mixs_ref[...].astype(BF16), wout_ref[...])
        outs_ref[...] = _ffn_tile(xs, *weights)


def _ffn(x, xs, mixs, l, gffn, wg, wu, wd, wout, *, tm, gfin=None, state_update=None):
    M, D = x.shape
    dff = wg.shape[-1]
    n_tiles = M // tm
    assert n_tiles * tm == M
    last = lambda i: jnp.minimum(i, n_tiles - 1)
    tile = pl.BlockSpec((tm, D), lambda i: (last(i), 0))
    args = [x, xs, mixs, wout, gffn, wg, wu, wd]
    in_specs = [tile, _const_spec(xs.shape), _const_spec(mixs.shape), _layer_spec(wout, l),
                _layer_spec(gffn, l)] + [_const_spec(w.shape) for w in (wg, wu, wd)]
    if gfin is not None:
        args.append(gfin)
        in_specs.append(_const_spec(gfin.shape))
    out_specs = [tile, pl.BlockSpec(xs.shape, lambda i: (0, 0))]
    out_shapes = [jax.ShapeDtypeStruct((M, D), F32), jax.ShapeDtypeStruct(xs.shape, F32)]
    est = (3 * D * dff * 2 + _layer_bytes(wout) + 6 * xs.size * 4
           + 6 * tm * D * 4 + 4 * FF_SUBTILE * dff * 4)
    upd_blocks = 0
    if state_update is not None:
        C = state_update[0]
        depth, nb = C.shape[:2]
        assert n_tiles % depth == 0, "the state blocks of all layers are spread over the tile steps"
        upd_blocks = n_tiles // depth
        bbu = nb // upd_blocks
        assert bbu * upd_blocks == nb
        cblk = pl.BlockSpec((None, bbu) + C.shape[2:],
                            lambda i: (last(i) // upd_blocks, last(i) % upd_blocks, 0, 0, 0))
        per_layer = lambda a: pl.BlockSpec((None,) + a.shape[1:],
                                           lambda i: (last(i) // upd_blocks, 0, 0, 0))
        args += list(state_update)
        in_specs += [cblk] + [per_layer(a) for a in state_update[1:]]
        out_specs.append(cblk)
        out_shapes.append(jax.ShapeDtypeStruct(C.shape, F32))
        est += 4 * bbu * C[0, 0].size * 4 + 2 * _layer_bytes(*state_update[1:])
    return pl.pallas_call(
        functools.partial(_ffn_kernel, final=gfin is not None, n_tiles=n_tiles,
                          upd_blocks=upd_blocks),
        grid=(n_tiles + 1,), in_specs=in_specs, out_specs=tuple(out_specs),
        out_shape=tuple(out_shapes),
        compiler_params=pltpu.CompilerParams(
            dimension_semantics=("arbitrary",), vmem_limit_bytes=_vmem_limit(est)),
        name="ffn_final" if gfin is not None else "ffn",
    )(*_in_hbm(*args))


def _head_out(h, g_head, o):
    hn = h * lax.rsqrt(jnp.mean(h * h, axis=-1, keepdims=True) + EPS)
    return hn * g_head * jax.nn.sigmoid(o)


def _sample_step_kernel(proj_ref, ghead_ref, wpool_ref, pscale_ref, cq_ref, n_ref, m_ref, buf_ref,
                        mix_ref, no_ref, mo_ref, bufo_ref, g_ref, wvt_ref, kb_ref):
    nb = proj_ref.shape[0]
    scale = HEAD_DIM ** -0.5
    head = lambda c0, h: proj_ref[:, c0 + h * HEAD_DIM:c0 + (h + 1) * HEAD_DIM]

    def gate_terms(h):
        i_c = _lane_col(proj_ref[:, GI0:GI0 + GATE_SLAB], h)
        lf = _lane_col(_log_sigmoid(proj_ref[:, GF0:GF0 + GATE_SLAB]), h)
        m0 = _lane_col(m_ref[...], h)
        inter = lf + m0
        m_t = jnp.maximum(inter, i_c)
        return jnp.exp(i_c - m_t), jnp.exp(inter - m_t), m_t

    for h in range(N_HEADS):
        w, g, _ = gate_terms(h)
        g_ref[h] = jnp.broadcast_to(g, (nb, LANES))
        wvt_ref[h] = (w * head(SV0, h)).T
        kb_ref[h] = (head(K0, h) * scale).astype(BF16)

    def finish():
        heads, n_heads = [], []
        lane_m = lax.broadcasted_iota(jnp.int32, (nb, LANES), 1)
        m_out = jnp.zeros((nb, LANES), F32)
        for h in range(N_HEADS):
            cols = slice(h * HEAD_DIM, (h + 1) * HEAD_DIM)
            w, g, m_t = gate_terms(h)
            qf = head(Q0, h)
            kf = head(K0, h) * scale
            vf = head(SV0, h)
            n0 = n_ref[:, cols]
            qk = jnp.sum(qf.astype(BF16).astype(F32) * kf.astype(BF16).astype(F32),
                         axis=-1, keepdims=True)
            s = qk * w
            num = s * vf + g * cq_ref[h].T
            den = s + g * jnp.sum(n0 * qf, axis=-1, keepdims=True)
            hc = num * (1.0 / jnp.maximum(jnp.abs(den), jnp.exp(-m_t)))
            heads.append(_head_out(hc, ghead_ref[:, cols], head(SO0, h)))
            n_heads.append(g * n0 + w * kf)
            m_out = jnp.where(lane_m == h, m_t, m_out)
        no_ref[...] = jnp.concatenate(n_heads, axis=-1)
        mo_ref[...] = m_out

        u = proj_ref[:, U0:U0 + POOL_WIDTH]
        pooled = []
        for gi, win in enumerate(POOL_WINDOWS):
            cols = slice(gi * POOL_GROUP, (gi + 1) * POOL_GROUP)
            a = u[:, cols]
            for jrow in range(POOL_BUF - (win - 1), POOL_BUF):
                a = a + buf_ref[jrow, :, cols]
            p = a / float(win) - u[:, cols]
            pooled.append(_dot(p.astype(BF16), wpool_ref[gi]) * pscale_ref[:, cols])
        mix_ref[...] = jnp.concatenate(heads + pooled, axis=-1)
        for jrow in range(POOL_BUF - 1):
            bufo_ref[jrow] = buf_ref[jrow + 1]
        bufo_ref[POOL_BUF - 1] = u

    finish()


def _sample_step(proj, l, ghead, wpool, pscale, cq, n, m, buf_t):
    nb = proj.shape[0]
    D = MLSTM_WIDTH + POOL_WIDTH
    in_specs = [_const_spec(proj.shape)] + [_layer_spec(a, l) for a in (ghead, wpool, pscale)] + [
        _const_spec(cq.shape), _layer_spec(n, l), _layer_spec(m, l), _layer_spec(buf_t, l)]
    args = [proj, ghead, wpool, pscale, cq, n, m, buf_t]
    shapes = ((nb, D), n.shape[1:], (nb, LANES), buf_t.shape[1:],
              (N_HEADS, nb, LANES), (N_HEADS, HEAD_DIM, nb), (N_HEADS, nb, HEAD_DIM))
    dtypes = (F32,) * 6 + (BF16,)
    out_shapes = tuple(jax.ShapeDtypeStruct(s, d) for s, d in zip(shapes, dtypes))
    out_specs = tuple(pl.BlockSpec(s, lambda i, k=len(s): (0,) * k) for s in shapes)
    est = 4 * _layer_bytes(buf_t) + 8 * proj.size * 4
    return pl.pallas_call(
        _sample_step_kernel,
        grid=(1,), in_specs=in_specs, out_specs=out_specs, out_shape=out_shapes,
        compiler_params=pltpu.CompilerParams(
            dimension_semantics=("arbitrary",), vmem_limit_bytes=_vmem_limit(est)),
        name="sample_step",
    )(*_in_hbm(*args))


def _pack_w_in(w_in, b_gate):
    depth, d, _ = w_in.shape
    g0 = 4 * MLSTM_WIDTH
    w_b = w_in.astype(BF16)
    wu = w_b[:, :, g0 + 2 * N_HEADS:]
    pad = jnp.zeros((depth, d, GATE_SLAB - N_HEADS), BF16)
    gates = w_b[:, :, g0:g0 + 2 * N_HEADS]
    wgate = jnp.concatenate([gates[:, :, :N_HEADS], pad, gates[:, :, N_HEADS:], pad], axis=-1)
    bpad = jnp.zeros((depth, GATE_SLAB - N_HEADS), b_gate.dtype)
    bslab = jnp.concatenate([b_gate[:, :N_HEADS], bpad, b_gate[:, N_HEADS:], bpad], axis=-1)
    return w_b, wu, wgate, bslab[:, None, :]


def kernel(x_prompt, x_sample, state_mlstm_C, state_mlstm_n, state_mlstm_m, state_pool_buf,
           g_mix, w_in, b_gate, g_head, w_pool, pool_scale, w_out, g_ffn, w_gate, w_up,
           w_down, g_final):
    depth = w_in.shape[0]
    B, T, D = x_prompt.shape
    nb = x_sample.shape[0]
    tm_mixer, mixer_tiles, tm_ffn, tm_ffn_last = CHUNK, 4, 1024, 512

    win = _pack_w_in(w_in, b_gate)
    wpool = w_pool.astype(BF16)
    wout = w_out.astype(BF16)
    ffn_w = (w_gate, w_up, w_down)
    row = lambda a: a[:, None, :]
    gmix, ghead, pscale, gffn = row(g_mix), row(g_head), row(pool_scale), row(g_ffn)
    ghead_rep = jnp.broadcast_to(g_head[:, :, None], g_head.shape + (CHUNK,))
    gfin = g_final[None, :]

    n_in = state_mlstm_n.reshape(depth, nb, MLSTM_WIDTH)
    m_pad = jnp.pad(state_mlstm_m, ((0, 0), (0, 0), (0, LANES - N_HEADS)))
    buf_t = jnp.transpose(state_pool_buf, (0, 2, 1, 3))

    xp = x_prompt
    xs = x_sample.reshape(nb, D)
    Cp, npr, mp, bp, ns, ms, bs, upd = [], [], [], [], [], [], [], []
    for l in range(depth):
        xm, C1, n1, m1, pb1, wg, wu, wd, cq, proj = _prompt_mixer(
            xp, l, gmix, *win, ghead_rep, wpool, pscale, wout, ffn_w, state_mlstm_C, xs,
            tm=tm_mixer, tiles=mixer_tiles)
        Cp.append(C1)
        npr.append(n1[:, :, 0, :])
        mp.append(m1[:, :N_HEADS, 0])
        bp.append(pb1[:, POOL_CARRY - POOL_BUF:, :])

        mix, n1, m1, buf1, *terms = _sample_step(
            proj, l, ghead, wpool, pscale, cq, n_in, m_pad, buf_t)
        ns.append(n1.reshape(nb, N_HEADS, HEAD_DIM))
        ms.append(m1[:, :N_HEADS])
        bs.append(jnp.transpose(buf1, (1, 0, 2)))
        upd.append(terms)

        last = l == depth - 1
        if last:
            g_all, wvt_all, kb_all = (jnp.stack(t, 0) for t in zip(*upd))
            terms = (g_all, wvt_all, kb_all.astype(F32))
        xp, xs, *C_sample = _ffn(
            xm.reshape(B * T, D), xs, mix, l, gffn, wg, wu, wd, wout,
            tm=tm_ffn_last if last else tm_ffn, gfin=gfin if last else None,
            state_update=(state_mlstm_C,) + terms if last else None)
        xp = xp.reshape(B, T, D)
    y_prompt = xp
    y_sample = xs.reshape(nb, 1, D)
    (C_sample,) = C_sample

    st = lambda xs_: jnp.stack(xs_, 0)
    return (y_prompt, y_sample, st(Cp), st(npr), st(mp), st(bp),
            C_sample, st(ns), st(ms), st(bs))
```

```python
import functools

import jax
import jax.numpy as jnp
from jax import lax
from jax.experimental import pallas as pl
from jax.experimental.pallas import tpu as pltpu

F32 = jnp.float32
BF16 = jnp.bfloat16

EPS = 1e-6
N_HEADS = 4
HEAD_DIM = 128
MLSTM_WIDTH = N_HEADS * HEAD_DIM
POOL_WINDOWS = (2, 4, 8, 16)
POOL_GROUP = 128
POOL_WIDTH = POOL_GROUP * len(POOL_WINDOWS)
POOL_BUF = max(POOL_WINDOWS) - 1
POOL_CARRY = POOL_BUF + 1
CHUNK = 256
LANES = 128
FF_SUBTILE = 256
SUBLANES = 8
N_ROWS = 2 * SUBLANES
GATE_SLAB = LANES

Q0, K0, U0 = 0, MLSTM_WIDTH, 2 * MLSTM_WIDTH
GI0 = U0 + POOL_WIDTH
GF0 = GI0 + GATE_SLAB
TOK_COLS = GF0 + GATE_SLAB
VT0, OT0 = 0, MLSTM_WIDTH
TR_ROWS = 2 * MLSTM_WIDTH
SV0 = TOK_COLS
SO0 = TOK_COLS + MLSTM_WIDTH
SAMPLE_COLS = TOK_COLS + TR_ROWS

V7X_VMEM_BYTES = 64 * 1024 * 1024
VMEM_RESERVE_BYTES = 4 * 1024 * 1024
VMEM_FLOOR_BYTES = 16 * 1024 * 1024

_NT = (((1,), (1,)), ((), ()))
_TN = (((0,), (0,)), ((), ()))


def _dot(a, b):
    return jnp.dot(a, b, preferred_element_type=F32)


def _dot_nt(a, b):
    return lax.dot_general(a, b, _NT, preferred_element_type=F32)


def _dot_tn(a, b):
    return lax.dot_general(a, b, _TN, preferred_element_type=F32)


def _rms(x, g):
    return x * lax.rsqrt(jnp.mean(x * x, axis=-1, keepdims=True) + EPS) * g


def _log_sigmoid(x):
    return jnp.minimum(x, 0.0) - jnp.log1p(jnp.exp(-jnp.abs(x)))


def _lane_col(slab, j):
    lane = lax.broadcasted_iota(jnp.int32, slab.shape, 1)
    return jnp.sum(jnp.where(lane == j, slab, 0.0), axis=-1, keepdims=True)


def _split3(x):
    hi = x.astype(BF16)
    r1 = x - hi.astype(F32)
    mid = r1.astype(BF16)
    lo = (r1 - mid.astype(F32)).astype(BF16)
    return hi, mid, lo


def _cumsum_rows(x):
    r = lax.broadcasted_iota(jnp.int32, (CHUNK, CHUNK), 0)
    c = lax.broadcasted_iota(jnp.int32, (CHUNK, CHUNK), 1)
    tril = (r >= c).astype(BF16)
    hi, mid, lo = _split3(x)
    return _dot(tril, hi) + _dot(tril, mid) + _dot(tril, lo)


def _head_stages(h, tok_ref, tr_ref, a_slab, a_t, cs_t, ghead_ref, C_ref, n_ref, m_ref,
                 mask_st, out):
    cols = slice(h * HEAD_DIM, (h + 1) * HEAD_DIM)
    head = lambda c0: slice(c0 + h * HEAD_DIM, c0 + (h + 1) * HEAD_DIM)
    C, n_rows, m = C_ref[h], n_ref[h], m_ref[h:h + 1, 0:1]
    qb = tok_ref[:, head(Q0)].astype(BF16)
    kb = tok_ref[:, head(K0)].astype(BF16)
    by_q = _dot_nt(jnp.concatenate([kb, C.astype(BF16), n_rows.astype(BF16)], axis=0), qb)
    yield
    sT = by_q[:CHUNK]
    cqT = by_q[CHUNK:CHUNK + HEAD_DIM]
    qn = by_q[CHUNK + HEAD_DIM:CHUNK + HEAD_DIM + 1]
    a_row, b_row = a_t[h:h + 1, :], cs_t[h:h + 1, :]
    del mask_st
    M_parts, sw_parts, den_parts = [], [], []
    rs = lax.broadcasted_iota(jnp.int32, (CHUNK, LANES), 0)
    ct = lax.broadcasted_iota(jnp.int32, (CHUNK, LANES), 1)
    for c0 in range(0, CHUNK, LANES):
        aT = jnp.where(rs <= ct + c0, a_slab[:, h:h + 1], -jnp.inf)
        M_p = jnp.maximum(jnp.max(aT, axis=0, keepdims=True), m)
        sw_p = sT[:, c0:c0 + LANES] * jnp.exp(aT - M_p)
        M_parts.append(M_p)
        den_parts.append(jnp.sum(sw_p, axis=0, keepdims=True))
        sw_parts.append(sw_p.astype(BF16))
    M = jnp.concatenate(M_parts, axis=1)
    g = jnp.exp(m - M)
    den = jnp.concatenate(den_parts, axis=1) + g * qn
    inv = 1.0 / jnp.maximum(jnp.abs(den), jnp.exp(-(b_row + M)))
    vT = tr_ref[head(VT0), :]
    hT = (_dot(vT.astype(BF16), jnp.concatenate(sw_parts, axis=1)) + g * cqT) * inv
    hnT = hT * lax.rsqrt(jnp.mean(hT * hT, axis=0, keepdims=True) + EPS)
    out[h] = (hnT * ghead_ref[cols, :] * jax.nn.sigmoid(tr_ref[head(OT0), :])).astype(BF16)
    yield
    M_last = M[:, CHUNK - 1:CHUNK]
    g_end = jnp.exp(m - M_last)
    w_end = jnp.exp(a_row - M_last)
    w_rows = jnp.broadcast_to(w_end, (N_ROWS, CHUNK))
    by_k = _dot(jnp.concatenate([(vT * w_end).astype(BF16), w_rows.astype(BF16)], axis=0), kb)
    C_ref[h] = g_end * C + by_k[:HEAD_DIM]
    n_ref[h] = g_end * n_rows + by_k[HEAD_DIM:]
    m_ref[h:h + 1, :] = jnp.broadcast_to(b_row[:, CHUNK - 1:CHUNK] + M_last, (1, LANES))
    yield


def _skewed(streams):
    streams = list(streams)
    live = []
    while streams or live:
        if streams:
            live.append(streams.pop(0))
        for g in list(live):
            try:
                next(g)
            except StopIteration:
                live.remove(g)
        if streams or live:
            yield


def _norm_tile(x_ref, gmix_ref, hn_ref):
    hn_ref[...] = _rms(x_ref[...], gmix_ref[...]).astype(BF16)


def _in_proj_items(hn_ref, wqk_ref, wu_ref, wgate_ref, wtr_ref, bslab_ref, tok_ref, tr_ref,
                   then=None):
    tok_ref[:, GI0:GI0 + 2 * GATE_SLAB] = _dot(hn_ref[...], wgate_ref[...]) + bslab_ref[...]
    yield
    tok_ref[:, Q0:Q0 + MLSTM_WIDTH] = _dot(hn_ref[...], wqk_ref[:, :MLSTM_WIDTH])
    yield
    tok_ref[:, K0:K0 + MLSTM_WIDTH] = (_dot(hn_ref[...], wqk_ref[:, MLSTM_WIDTH:])
                                       * (HEAD_DIM ** -0.5))
    yield
    tok_ref[:, U0:U0 + POOL_WIDTH] = _dot(hn_ref[...], wu_ref[...])
    yield
    for r0 in (VT0, OT0):
        tr_ref[r0:r0 + MLSTM_WIDTH, :] = _dot_nt(wtr_ref[r0:r0 + MLSTM_WIDTH, :], hn_ref[...])
        yield
    if then is not None:
        then()
        yield


def _mix_items(x_ref, tok_ref, tr_ref, t, ghead_ref, wpool_ref, pscale_ref, wout_ref,
               C_ref, n_ref, m_ref, pb_ref, out_ref, tm):
    r = lax.broadcasted_iota(jnp.int32, (CHUNK, CHUNK), 0)
    c = lax.broadcasted_iota(jnp.int32, (CHUNK, CHUNK), 1)
    mask_st = r <= c

    assert tm == CHUNK, "one mLSTM chunk per tile"
    cs = _cumsum_rows(_log_sigmoid(tok_ref[:, GF0:GF0 + GATE_SLAB]))
    a_slab = tok_ref[:, GI0:GI0 + GATE_SLAB] - cs
    cs_t = cs.T
    a_t = a_slab.T
    yield
    heads = [None] * N_HEADS
    yield from _skewed(
        _head_stages(h, tok_ref, tr_ref, a_slab, a_t, cs_t, ghead_ref, C_ref, n_ref, m_ref,
                     mask_st, heads)
        for h in range(N_HEADS))
    yield
    hmT = jnp.concatenate(heads, axis=0)

    u = tok_ref[:, U0:U0 + POOL_WIDTH]
    ext = jnp.concatenate([pb_ref[...], u], axis=0)
    pb_ref[...] = u[tm - POOL_CARRY:, :]
    pos = t * tm + lax.broadcasted_iota(jnp.int32, (tm, POOL_GROUP), 0)
    pooled = []
    for gi, win in enumerate(POOL_WINDOWS):
        cols = slice(gi * POOL_GROUP, (gi + 1) * POOL_GROUP)
        a = ext[:, cols]
        step = 1
        while step < win:
            a = a[step:, :] + a[:a.shape[0] - step, :]
            step *= 2
        a = a[a.shape[0] - tm:, :]
        cnt = jnp.minimum(win, pos + 1).astype(F32)
        p = a / cnt - u[:, cols]
        pooled.append((_dot(p.astype(BF16), wpool_ref[gi]) * pscale_ref[:, cols]).astype(BF16))
    yield
    pm = jnp.concatenate(pooled, axis=-1)
    out_ref[...] = (x_ref[...] + _dot_tn(hmT, wout_ref[:MLSTM_WIDTH, :])
                    + _dot(pm, wout_ref[MLSTM_WIDTH:, :]))
    yield


def _interleave(*streams):
    streams = list(streams)
    while streams:
        for s in list(streams):
            try:
                next(s)
            except StopIteration:
                streams.remove(s)


def _prompt_mixer_kernel(xcur_ref, xnext_ref, gmix_ref, wqk_ref, wu_ref, wgate_ref, wvo_ref,
                         bslab_ref, ghead_ref, wpool_ref, pscale_ref, wout_ref,
                         fg_ref, fu_ref, fd_ref, Cs_ref, xs_ref,
                         xo_ref, C_ref, n_ref, m_ref, pb_ref, fgb_ref, fub_ref, fdb_ref, cq_ref,
                         projs_ref,
                         tok_a, tr_a, hn_a, tok_b, tr_b, hn_b, wtr_s, qs_s, *, tm, nt, tiles):
    fgb_ref[...] = fg_ref[...].astype(BF16)
    fub_ref[...] = fu_ref[...].astype(BF16)
    fdb_ref[...] = fd_ref[...].astype(BF16)
    s = pl.program_id(0)

    @pl.when(s == 0)
    def _():
        cq_ref[...] = jnp.zeros_like(cq_ref)
        hn_s = _rms(xs_ref[...], gmix_ref[...]).astype(BF16)
        qk = _dot(hn_s, wqk_ref[...])
        projs_ref[:, Q0:U0] = qk
        projs_ref[:, U0:GI0] = _dot(hn_s, wu_ref[...])
        projs_ref[:, GI0:TOK_COLS] = _dot(hn_s, wgate_ref[...]) + bslab_ref[...]
        projs_ref[:, TOK_COLS:] = _dot(hn_s, wvo_ref[...])
        for h in range(N_HEADS):
            qs_s[h] = qk[:, Q0 + h * HEAD_DIM:Q0 + (h + 1) * HEAD_DIM]

    def sample_cq_items(h):
        lane_b = lax.broadcasted_iota(jnp.int32, cq_ref.shape[1:], 1)
        cq = cq_ref[h]
        for j in range(Cs_ref.shape[0]):
            b = s * Cs_ref.shape[0] + j
            col = jnp.sum(Cs_ref[j, h] * qs_s[h, pl.ds(b, 1), :], axis=-1, keepdims=True)
            cq = jnp.where(lane_b == b, col, cq)
            yield
        cq_ref[h] = cq
        yield

    t0 = lax.rem(tiles * s, nt)
    proj_args = (wqk_ref, wu_ref, wgate_ref, wtr_s, bslab_ref)
    mix_args = (ghead_ref, wpool_ref, pscale_ref, wout_ref, C_ref, n_ref, m_ref, pb_ref)
    slots = ((tok_a, tr_a, hn_a), (tok_b, tr_b, hn_b))
    x_tile = lambda k: xcur_ref.at[k] if k < tiles else xnext_ref.at[k - tiles]

    @pl.when(s == 0)
    def _():
        wtr_s[...] = wvo_ref[...].T
        _norm_tile(x_tile(0), gmix_ref, hn_a)
        _interleave(_in_proj_items(hn_a, *proj_args, tok_a, tr_a))
        _norm_tile(x_tile(1), gmix_ref, hn_b)

    @pl.when(t0 == 0)
    def _():
        C_ref[...] = jnp.zeros_like(C_ref)
        n_ref[...] = jnp.zeros_like(n_ref)
        m_ref[...] = jnp.zeros_like(m_ref)
        pb_ref[...] = jnp.zeros_like(pb_ref)

    for k in range(tiles):
        tok, tr, hn = slots[k % 2]
        tok_n, tr_n, hn_n = slots[(k + 1) % 2]
        _interleave(
            _in_proj_items(hn_n, *proj_args, tok_n, tr_n,
                           then=functools.partial(_norm_tile, x_tile(k + 2), gmix_ref, hn)),
            _mix_items(x_tile(k), tok, tr, t0 + k, *mix_args, xo_ref.at[k], tm),
            *(sample_cq_items(h) for h in range(k, N_HEADS, tiles)))


def _const_spec(shape):
    nd = len(shape)
    return pl.BlockSpec(shape, lambda *_: (0,) * nd, pipeline_mode=pl.Buffered(1))


def _layer_spec(a, l):
    nd = a.ndim - 1
    return pl.BlockSpec((None,) + a.shape[1:], lambda *_: (l,) + (0,) * nd,
                        pipeline_mode=pl.Buffered(1))


def _layer_cols_spec(a, l, width, j):
    return pl.BlockSpec((None, a.shape[1], width), lambda *_: (l, 0, j),
                        pipeline_mode=pl.Buffered(1))


def _in_hbm(*arrays):
    return [pltpu.with_memory_space_constraint(a, pltpu.MemorySpace.HBM) for a in arrays]


def _layer_bytes(*arrays):
    return sum(a.size // a.shape[0] * a.dtype.itemsize for a in arrays)


def _vmem_limit(nbytes):
    return int(min(V7X_VMEM_BYTES - VMEM_RESERVE_BYTES, max(nbytes, VMEM_FLOOR_BYTES)))


def _slab_rows(rows, nsteps):
    for steps_per_slab in (1, 2):
        per = rows * steps_per_slab // nsteps
        if per * nsteps == rows * steps_per_slab and per % N_ROWS == 0:
            return per, steps_per_slab
    raise ValueError(f"cannot split {rows} weight rows over {nsteps} grid steps")


def _prompt_mixer(x, l, gmix, w_b, wu, wgate, bslab, ghead_rep, wpool, pscale, wout,
                  ffn_w, Cs, xs, *, tm, tiles):
    B, T, D = x.shape
    nt = T // tm
    assert tiles % 2 == 0 and nt % tiles == 0, "a grid step covers an even number of tiles of one sequence"
    ntiles = B * nt
    nsteps = ntiles // tiles
    xt = x.reshape(ntiles, tm, D)
    pair = pl.BlockSpec((tiles, tm, D), lambda s: (s, 0, 0))
    nxt = pl.BlockSpec((2, tm, D),
                       lambda s: (jnp.minimum((s + 1) * (tiles // 2), ntiles // 2 - 1), 0, 0))
    seq = lambda s: (tiles * s) // nt
    out_shapes = (
        jax.ShapeDtypeStruct((ntiles, tm, D), F32),
        jax.ShapeDtypeStruct((B, N_HEADS, HEAD_DIM, HEAD_DIM), F32),
        jax.ShapeDtypeStruct((B, N_HEADS, N_ROWS, HEAD_DIM), F32),
        jax.ShapeDtypeStruct((B, SUBLANES, LANES), F32),
        jax.ShapeDtypeStruct((B, POOL_CARRY, POOL_WIDTH), F32),
    )
    out_specs = (
        pair,
        pl.BlockSpec((None, N_HEADS, HEAD_DIM, HEAD_DIM), lambda s: (seq(s), 0, 0, 0)),
        pl.BlockSpec((None, N_HEADS, N_ROWS, HEAD_DIM), lambda s: (seq(s), 0, 0, 0)),
        pl.BlockSpec((None, SUBLANES, LANES), lambda s: (seq(s), 0, 0)),
        pl.BlockSpec((None, POOL_CARRY, POOL_WIDTH), lambda s: (seq(s), 0, 0)),
    )
    consts = (gmix, w_b, wu, wgate, w_b, bslab, ghead_rep, wpool, pscale, wout)
    in_specs = [pair, nxt] + [_layer_spec(a, l) for a in consts]
    in_specs[3] = _layer_cols_spec(w_b, l, 2 * MLSTM_WIDTH, 0)
    in_specs[6] = _layer_cols_spec(w_b, l, 2 * MLSTM_WIDTH, 1)
    slab_bytes = 0
    for w in ffn_w:
        rows, cols = w.shape[1:]
        per, stride = _slab_rows(rows, nsteps)
        slab_bytes += 2 * per * cols * (4 + 2)
        in_specs.append(pl.BlockSpec((None, per, cols), lambda s, k=stride: (l, s // k, 0)))
        out_specs += (pl.BlockSpec((per, cols), lambda s, k=stride: (s // k, 0)),)
        out_shapes += (jax.ShapeDtypeStruct((rows, cols), BF16),)
    nb = xs.shape[0]
    bbs = nb // nsteps
    assert bbs * nsteps == nb, "the sample sequences are spread evenly over the grid steps"
    in_specs += [pl.BlockSpec((None, bbs) + Cs.shape[2:], lambda s: (l, s, 0, 0, 0)),
                 _const_spec(xs.shape)]
    cq_shape = (N_HEADS, HEAD_DIM, nb)
    out_specs += (pl.BlockSpec(cq_shape, lambda s: (0, 0, 0)),
                  pl.BlockSpec((nb, SAMPLE_COLS), lambda s: (0, 0)))
    out_shapes += (jax.ShapeDtypeStruct(cq_shape, F32),
                   jax.ShapeDtypeStruct((nb, SAMPLE_COLS), F32))
    slab_bytes += 2 * bbs * Cs[0, 0].size * 4 + 3 * nb * SAMPLE_COLS * 4
    est = (_layer_bytes(wu, wgate, wpool, wout) + 2 * D * TR_ROWS * 2 + slab_bytes
           + 6 * tiles * tm * D * 4 + 12 * tm * SAMPLE_COLS * 4)
    scratch = [pltpu.VMEM((tm, TOK_COLS), F32), pltpu.VMEM((TR_ROWS, tm), F32),
               pltpu.VMEM((tm, D), BF16)] * 2 + [pltpu.VMEM((TR_ROWS, D), BF16),
                                                  pltpu.VMEM((N_HEADS, nb, HEAD_DIM), F32)]
    outs = pl.pallas_call(
        functools.partial(_prompt_mixer_kernel, tm=tm, nt=nt, tiles=tiles),
        grid=(nsteps,), in_specs=in_specs, out_specs=out_specs, out_shape=out_shapes,
        scratch_shapes=scratch,
        compiler_params=pltpu.CompilerParams(
            dimension_semantics=("arbitrary",),
            vmem_limit_bytes=_vmem_limit(est)),
        name="prompt_mixer",
    )(*_in_hbm(xt, xt, *consts, *ffn_w, Cs, xs))
    return (outs[0].reshape(B, T, D),) + tuple(outs[1:])


def _ffn_tile(x, gffn_ref, wg_s, wu_s, wd_s, gfin_ref):
    h2 = _rms(x, gffn_ref[...]).astype(BF16)
    a = _dot(h2, wg_s[...])
    bu = _dot(h2, wu_s[...])
    act = (a * jax.nn.sigmoid(a) * bu).astype(BF16)
    y = x + _dot(act, wd_s[...])
    return y if gfin_ref is None else _rms(y, gfin_ref[...])


def _state_update_block(C_ref, g_ref, wvt_ref, k_ref, Co_ref, first_seq):
    nb = k_ref.shape[1]
    lane = lax.broadcasted_iota(jnp.int32, (HEAD_DIM, nb), 1)
    for h in range(N_HEADS):
        wvt = wvt_ref[h]
        for j in range(C_ref.shape[0]):
            b = first_seq + j
            col = jnp.sum(jnp.where(lane == b, wvt, 0.0), axis=-1, keepdims=True)
            k_row = k_ref[h, pl.ds(b, 1), :]
            Co_ref[j, h] = g_ref[h, pl.ds(b, 1), :] * C_ref[j, h] + col * k_row


def _ffn_kernel(*refs, final, n_tiles, upd_blocks):
    refs = list(refs)
    x_ref, xs_ref, mixs_ref, wout_ref, gffn_ref, wg_ref, wu_ref, wd_ref = refs[:8]
    refs = refs[8:]
    gfin_ref = refs.pop(0) if final else None
    if upd_blocks:
        C_ref, g_ref, wvt_ref, kb_ref = refs[:4]
        refs = refs[4:]
        out_ref, outs_ref, Co_ref = refs
    else:
        out_ref, outs_ref = refs
    i = pl.program_id(0)
    weights = (gffn_ref, wg_ref, wu_ref, wd_ref, gfin_ref)

    @pl.when(i < n_tiles)
    def _():
        for r0 in range(0, x_ref.shape[0], FF_SUBTILE):
            rows = slice(r0, r0 + FF_SUBTILE)
            out_ref[rows, :] = _ffn_tile(x_ref[rows, :], *weights)
        if upd_blocks:
            _state_update_block(C_ref, g_ref, wvt_ref, kb_ref, Co_ref,
                                lax.rem(i, upd_blocks) * C_ref.shape[0])

    @pl.when(i == n_tiles)
    def _():
        xs = xs_ref[...] + _dot(mixs_ref[...].astype(BF16), wout_ref[...])
        outs_ref[...] = _ffn_tile(xs, *weights)


def _ffn(x, xs, mixs, l, gffn, wg, wu, wd, wout, *, tm, gfin=None, state_update=None):
    M, D = x.shape
    dff = wg.shape[-1]
    n_tiles = M // tm
    assert n_tiles * tm == M
    last = lambda i: jnp.minimum(i, n_tiles - 1)
    tile = pl.BlockSpec((tm, D), lambda i: (last(i), 0))
    args = [x, xs, mixs, wout, gffn, wg, wu, wd]
    in_specs = [tile, _const_spec(xs.shape), _const_spec(mixs.shape), _layer_spec(wout, l),
                _layer_spec(gffn, l)] + [_const_spec(w.shape) for w in (wg, wu, wd)]
    if gfin is not None:
        args.append(gfin)
        in_specs.append(_const_spec(gfin.shape))
    out_specs = [tile, pl.BlockSpec(xs.shape, lambda i: (0, 0))]
    out_shapes = [jax.ShapeDtypeStruct((M, D), F32), jax.ShapeDtypeStruct(xs.shape, F32)]
    est = (3 * D * dff * 2 + _layer_bytes(wout) + 6 * xs.size * 4
           + 6 * tm * D * 4 + 4 * FF_SUBTILE * dff * 4)
    upd_blocks = 0
    if state_update is not None:
        C = state_update[0]
        depth, nb = C.shape[:2]
        assert n_tiles % depth == 0, "the state blocks of all layers are spread over the tile steps"
        upd_blocks = n_tiles // depth
        bbu = nb // upd_blocks
        assert bbu * upd_blocks == nb
        cblk = pl.BlockSpec((None, bbu) + C.shape[2:],
                            lambda i: (last(i) // upd_blocks, last(i) % upd_blocks, 0, 0, 0))
        per_layer = lambda a: pl.BlockSpec((None,) + a.shape[1:],
                                           lambda i: (last(i) // upd_blocks, 0, 0, 0))
        args += list(state_update)
        in_specs += [cblk] + [per_layer(a) for a in state_update[1:]]
        out_specs.append(cblk)
        out_shapes.append(jax.ShapeDtypeStruct(C.shape, F32))
        est += 4 * bbu * C[0, 0].size * 4 + 2 * _layer_bytes(*state_update[1:])
    return pl.pallas_call(
        functools.partial(_ffn_kernel, final=gfin is not None, n_tiles=n_tiles,
                          upd_blocks=upd_blocks),
        grid=(n_tiles + 1,), in_specs=in_specs, out_specs=tuple(out_specs),
        out_shape=tuple(out_shapes),
        compiler_params=pltpu.CompilerParams(
            dimension_semantics=("arbitrary",), vmem_limit_bytes=_vmem_limit(est)),
        name="ffn_final" if gfin is not None else "ffn",
    )(*_in_hbm(*args))


def _head_out(h, g_head, o):
    hn = h * lax.rsqrt(jnp.mean(h * h, axis=-1, keepdims=True) + EPS)
    return hn * g_head * jax.nn.sigmoid(o)


def _sample_step_kernel(proj_ref, ghead_ref, wpool_ref, pscale_ref, cq_ref, n_ref, m_ref, buf_ref,
                        mix_ref, no_ref, mo_ref, bufo_ref, g_ref, wvt_ref, kb_ref):
    nb = proj_ref.shape[0]
    scale = HEAD_DIM ** -0.5
    head = lambda c0, h: proj_ref[:, c0 + h * HEAD_DIM:c0 + (h + 1) * HEAD_DIM]

    def gate_terms(h):
        i_c = _lane_col(proj_ref[:, GI0:GI0 + GATE_SLAB], h)
        lf = _lane_col(_log_sigmoid(proj_ref[:, GF0:GF0 + GATE_SLAB]), h)
        m0 = _lane_col(m_ref[...], h)
        inter = lf + m0
        m_t = jnp.maximum(inter, i_c)
        return jnp.exp(i_c - m_t), jnp.exp(inter - m_t), m_t

    for h in range(N_HEADS):
        w, g, _ = gate_terms(h)
        g_ref[h] = jnp.broadcast_to(g, (nb, LANES))
        wvt_ref[h] = (w * head(SV0, h)).T
        kb_ref[h] = (head(K0, h) * scale).astype(BF16)

    def finish():
        heads, n_heads = [], []
        lane_m = lax.broadcasted_iota(jnp.int32, (nb, LANES), 1)
        m_out = jnp.zeros((nb, LANES), F32)
        for h in range(N_HEADS):
            cols = slice(h * HEAD_DIM, (h + 1) * HEAD_DIM)
            w, g, m_t = gate_terms(h)
            qf = head(Q0, h)
            kf = head(K0, h) * scale
            vf = head(SV0, h)
            n0 = n_ref[:, cols]
            qk = jnp.sum(qf.astype(BF16).astype(F32) * kf.astype(BF16).astype(F32),
                         axis=-1, keepdims=True)
            s = qk * w
            num = s * vf + g * cq_ref[h].T
            den = s + g * jnp.sum(n0 * qf, axis=-1, keepdims=True)
            hc = num * (1.0 / jnp.maximum(jnp.abs(den), jnp.exp(-m_t)))
            heads.append(_head_out(hc, ghead_ref[:, cols], head(SO0, h)))
            n_heads.append(g * n0 + w * kf)
            m_out = jnp.where(lane_m == h, m_t, m_out)
        no_ref[...] = jnp.concatenate(n_heads, axis=-1)
        mo_ref[...] = m_out

        u = proj_ref[:, U0:U0 + POOL_WIDTH]
        pooled = []
        for gi, win in enumerate(POOL_WINDOWS):
            cols = slice(gi * POOL_GROUP, (gi + 1) * POOL_GROUP)
            a = u[:, cols]
            for jrow in range(POOL_BUF - (win - 1), POOL_BUF):
                a = a + buf_ref[jrow, :, cols]
            p = a / float(win) - u[:, cols]
            pooled.append(_dot(p.astype(BF16), wpool_ref[gi]) * pscale_ref[:, cols])
        mix_ref[...] = jnp.concatenate(heads + pooled, axis=-1)
        for jrow in range(POOL_BUF - 1):
            bufo_ref[jrow] = buf_ref[jrow + 1]
        bufo_ref[POOL_BUF - 1] = u

    finish()


def _sample_step(proj, l, ghead, wpool, pscale, cq, n, m, buf_t):
    nb = proj.shape[0]
    D = MLSTM_WIDTH + POOL_WIDTH
    in_specs = [_const_spec(proj.shape)] + [_layer_spec(a, l) for a in (ghead, wpool, pscale)] + [
        _const_spec(cq.shape), _layer_spec(n, l), _layer_spec(m, l), _layer_spec(buf_t, l)]
    args = [proj, ghead, wpool, pscale, cq, n, m, buf_t]
    shapes = ((nb, D), n.shape[1:], (nb, LANES), buf_t.shape[1:],
              (N_HEADS, nb, LANES), (N_HEADS, HEAD_DIM, nb), (N_HEADS, nb, HEAD_DIM))
    dtypes = (F32,) * 6 + (BF16,)
    out_shapes = tuple(jax.ShapeDtypeStruct(s, d) for s, d in zip(shapes, dtypes))
    out_specs = tuple(pl.BlockSpec(s, lambda i, k=len(s): (0,) * k) for s in shapes)
    est = 4 * _layer_bytes(buf_t) + 8 * proj.size * 4
    return pl.pallas_call(
        _sample_step_kernel,
        grid=(1,), in_specs=in_specs, out_specs=out_specs, out_shape=out_shapes,
        compiler_params=pltpu.CompilerParams(
            dimension_semantics=("arbitrary",), vmem_limit_bytes=_vmem_limit(est)),
        name="sample_step",
    )(*_in_hbm(*args))


def _pack_w_in(w_in, b_gate):
    depth, d, _ = w_in.shape
    g0 = 4 * MLSTM_WIDTH
    w_b = w_in.astype(BF16)
    wu = w_b[:, :, g0 + 2 * N_HEADS:]
    pad = jnp.zeros((depth, d, GATE_SLAB - N_HEADS), BF16)
    gates = w_b[:, :, g0:g0 + 2 * N_HEADS]
    wgate = jnp.concatenate([gates[:, :, :N_HEADS], pad, gates[:, :, N_HEADS:], pad], axis=-1)
    bpad = jnp.zeros((depth, GATE_SLAB - N_HEADS), b_gate.dtype)
    bslab = jnp.concatenate([b_gate[:, :N_HEADS], bpad, b_gate[:, N_HEADS:], bpad], axis=-1)
    return w_b, wu, wgate, bslab[:, None, :]


def kernel(x_prompt, x_sample, state_mlstm_C, state_mlstm_n, state_mlstm_m, state_pool_buf,
           g_mix, w_in, b_gate, g_head, w_pool, pool_scale, w_out, g_ffn, w_gate, w_up,
           w_down, g_final):
    depth = w_in.shape[0]
    B, T, D = x_prompt.shape
    nb = x_sample.shape[0]
    tm_mixer, mixer_tiles, tm_ffn, tm_ffn_last = CHUNK, 4, 1024, 512

    win = _pack_w_in(w_in, b_gate)
    wpool = w_pool.astype(BF16)
    wout = w_out.astype(BF16)
    ffn_w = (w_gate, w_up, w_down)
    row = lambda a: a[:, None, :]
    gmix, ghead, pscale, gffn = row(g_mix), row(g_head), row(pool_scale), row(g_ffn)
    ghead_rep = jnp.broadcast_to(g_head[:, :, None], g_head.shape + (CHUNK,))
    gfin = g_final[None, :]

    n_in = state_mlstm_n.reshape(depth, nb, MLSTM_WIDTH)
    m_pad = jnp.pad(state_mlstm_m, ((0, 0), (0, 0), (0, LANES - N_HEADS)))
    buf_t = jnp.transpose(state_pool_buf, (0, 2, 1, 3))

    xp = x_prompt
    xs = x_sample.reshape(nb, D)
    Cp, npr, mp, bp, ns, ms, bs, upd = [], [], [], [], [], [], [], []
    for l in range(depth):
        xm, C1, n1, m1, pb1, wg, wu, wd, cq, proj = _prompt_mixer(
            xp, l, gmix, *win, ghead_rep, wpool, pscale, wout, ffn_w, state_mlstm_C, xs,
            tm=tm_mixer, tiles=mixer_tiles)
        Cp.append(C1)
        npr.append(n1[:, :, 0, :])
        mp.append(m1[:, :N_HEADS, 0])
        bp.append(pb1[:, POOL_CARRY - POOL_BUF:, :])

        mix, n1, m1, buf1, *terms = _sample_step(
            proj, l, ghead, wpool, pscale, cq, n_in, m_pad, buf_t)
        ns.append(n1.reshape(nb, N_HEADS, HEAD_DIM))
        ms.append(m1[:, :N_HEADS])
        bs.append(jnp.transpose(buf1, (1, 0, 2)))
        upd.append(terms)

        last = l == depth - 1
        if last:
            g_all, wvt_all, kb_all = (jnp.stack(t, 0) for t in zip(*upd))
            terms = (g_all, wvt_all, kb_all.astype(F32))
        xp, xs, *C_sample = _ffn(
            xm.reshape(B * T, D), xs, mix, l, gffn, wg, wu, wd, wout,
            tm=tm_ffn_last if last else tm_ffn, gfin=gfin if last else None,
            state_update=(state_mlstm_C,) + terms if last else None)
        xp = xp.reshape(B, T, D)
    y_prompt = xp
    y_sample = xs.reshape(nb, 1, D)
    (C_sample,) = C_sample

    st = lambda xs_: jnp.stack(xs_, 0)
    return (y_prompt, y_sample, st(Cp), st(npr), st(mp), st(bp),
            C_sample, st(ns), st(ms), st(bs))
```
